```python
import math
import jax, jax.numpy as jnp
from jax import lax
import numpy as np

D_MODEL = 1024
BATCH = 8
SEQ = 2048
DEPTH = 1
DEC_BATCH = 128
DEC_SEQ = 1
PAST_LEN = 16384
PAGE_SIZE = 128

D_MIX = 2 * D_MODEL
SSD_WIDTH = (3 * D_MIX) // 4
SSD_HEAD_DIM = 64
SSD_HEADS = SSD_WIDTH // SSD_HEAD_DIM
SSD_GROUPS = 4
SSD_HEADS_PER_GROUP = SSD_HEADS // SSD_GROUPS
SSD_STATE = 64
SSD_CONV_W = 4
SSD_CHUNK = 128
CONV_DIM = SSD_WIDTH + 2 * SSD_GROUPS * SSD_STATE
POOL_WIDTH = D_MIX - SSD_WIDTH
POOL_WINDOWS = (2, 4, 8, 16)
POOL_GROUPS = len(POOL_WINDOWS)
POOL_GROUP_DIM = POOL_WIDTH // POOL_GROUPS
POOL_MAX = max(POOL_WINDOWS)
D_FF = 128 * ((8 * D_MODEL // 3 + 127) // 128)
FFN_CONV_W = 3
IN_COLS = SSD_WIDTH + CONV_DIM + SSD_HEADS + POOL_WIDTH
EPS = 1e-6

kernel_name = "hymba_ssd_pool_convffn_step"


def rmsnorm(x, w):
    xf = x.astype(jnp.float32)
    y = xf * lax.rsqrt(jnp.mean(xf * xf, axis=-1, keepdims=True) + EPS) * w.astype(jnp.float32)
    return y.astype(x.dtype)


def causal_dwconv(u, buf, w, b):
    K = w.shape[0]
    T = u.shape[1]
    ext = jnp.concatenate([buf.astype(u.dtype), u], axis=1)
    out = b
    for k in range(K):
        out = out + ext[:, k:k + T] * w[k]
    return out, ext[:, ext.shape[1] - (K - 1):]


def pad_time(arr, pad):
    return jnp.pad(arr, [(0, 0), (0, pad)] + [(0, 0)] * (arr.ndim - 2))


def ssd_scan(x, dt, a, bm, cm, h0):
    f32 = jnp.float32
    bsz, T = x.shape[0], x.shape[1]
    G, R, P, N = SSD_GROUPS, SSD_HEADS_PER_GROUP, SSD_HEAD_DIM, SSD_STATE
    Q = min(SSD_CHUNK, T)
    pad = (-T) % Q
    if pad:
        x, dt, bm, cm = pad_time(x, pad), pad_time(dt, pad), pad_time(bm, pad), pad_time(cm, pad)
    nc = (T + pad) // Q
    xc = x.astype(f32).reshape(bsz, nc, Q, G, R, P)
    dtc = dt.astype(f32).reshape(bsz, nc, Q, G, R)
    bc = bm.astype(f32).reshape(bsz, nc, Q, G, N)
    cc = cm.astype(f32).reshape(bsz, nc, Q, G, N)
    acum = jnp.cumsum(dtc * a.reshape(G, R), axis=2)
    diff = acum[:, :, :, None] - acum[:, :, None]
    causal = jnp.tril(jnp.ones((Q, Q), dtype=bool))[:, :, None, None]
    decay = jnp.exp(jnp.where(causal, diff, -jnp.inf))
    cb = jnp.einsum('bclgn,bcsgn->bclsg', cc, bc)
    mix = cb[..., None] * decay * dtc[:, :, None]
    y_diag = jnp.einsum('bclsgr,bcsgrp->bclgrp', mix, xc)
    decay_to_end = jnp.exp(acum[:, :, -1:] - acum)
    chunk_states = jnp.einsum('bclgn,bclgr,bclgrp->bcgrpn', bc, decay_to_end * dtc, xc)
    chunk_decay = jnp.exp(acum[:, :, -1])

    def step(h, inp):
        cs, cd = inp
        return cd[..., None, None] * h + cs, h

    h_init = h0.astype(f32).reshape(bsz, G, R, P, N)
    h_final, h_in = lax.scan(step, h_init,
                             (jnp.moveaxis(chunk_states, 1, 0), jnp.moveaxis(chunk_decay, 1, 0)))
    h_in = jnp.moveaxis(h_in, 0, 1)
    y_off = jnp.einsum('bclgn,bcgrpn,bclgr->bclgrp', cc, h_in, jnp.exp(acum))
    y = (y_diag + y_off).reshape(bsz, nc * Q, SSD_HEADS, P)[:, :T]
    return y, h_final.reshape(bsz, SSD_HEADS, P, N)


def pool_mixer(v, buf, pos0, pool_w, pool_scale):
    f32 = jnp.float32
    bsz, T = v.shape[0], v.shape[1]
    P = POOL_MAX - 1
    ext = jnp.concatenate([buf.astype(v.dtype), v], axis=1)
    cs = jnp.cumsum(ext.astype(f32), axis=1)
    cs = jnp.concatenate([jnp.zeros((bsz, 1, POOL_WIDTH), f32), cs], axis=1)
    pos = pos0 + jnp.arange(T, dtype=jnp.int32)
    vf = v.astype(f32)
    outs = []
    for g, w in enumerate(POOL_WINDOWS):
        c0, c1 = g * POOL_GROUP_DIM, (g + 1) * POOL_GROUP_DIM
        win_sum = cs[:, P + 1:P + 1 + T, c0:c1] - cs[:, P + 1 - w:P + 1 - w + T, c0:c1]
        cnt = jnp.minimum(pos + 1, w).astype(f32)[None, :, None]
        outs.append(win_sum / cnt - vf[:, :, c0:c1])
    m = jnp.stack(outs, axis=2)
    y = jnp.einsum('btgc,gcd->btgd', m, pool_w.astype(f32)).reshape(bsz, T, POOL_WIDTH)
    y = y * pool_scale.astype(f32)
    return y.astype(v.dtype), ext[:, ext.shape[1] - P:]


def hybrid_layer(x, pos0, h0, conv_buf, pool_buf, ffn_buf,
                 norm1_w, w_in, conv_w, conv_b, dt_bias, a_log, d_skip, ssd_norm_w,
                 pool_w, pool_scale, w_out, norm2_w, w_up, ffn_conv_w, ffn_conv_b, w_down):
    f32 = jnp.float32
    bsz, T, _ = x.shape
    hn = rmsnorm(x, norm1_w)
    proj = hn @ w_in
    z, xbc, dt_raw, v = jnp.split(
        proj, [SSD_WIDTH, SSD_WIDTH + CONV_DIM, SSD_WIDTH + CONV_DIM + SSD_HEADS], axis=-1)
    xbc, conv_new = causal_dwconv(xbc, conv_buf, conv_w, conv_b)
    xbc = jax.nn.silu(xbc)
    xs, bm, cm = jnp.split(xbc, [SSD_WIDTH, SSD_WIDTH + SSD_GROUPS * SSD_STATE], axis=-1)
    xs = xs.reshape(bsz, T, SSD_HEADS, SSD_HEAD_DIM)
    bm = bm.reshape(bsz, T, SSD_GROUPS, SSD_STATE)
    cm = cm.reshape(bsz, T, SSD_GROUPS, SSD_STATE)
    dt = jax.nn.softplus(dt_raw.astype(f32) + dt_bias.astype(f32))
    a = -jnp.exp(a_log.astype(f32))
    y_ssd, h_new = ssd_scan(xs, dt, a, bm, cm, h0)
    y_ssd = y_ssd + d_skip.astype(f32)[:, None] * xs.astype(f32)
    y_ssd = y_ssd.reshape(bsz, T, SSD_WIDTH) * jax.nn.silu(z.astype(f32))
    y_ssd = rmsnorm(y_ssd.reshape(bsz, T, SSD_GROUPS, SSD_WIDTH // SSD_GROUPS),
                    ssd_norm_w.reshape(SSD_GROUPS, SSD_WIDTH // SSD_GROUPS))
    y_ssd = y_ssd.reshape(bsz, T, SSD_WIDTH).astype(x.dtype)
    y_pool, pool_new = pool_mixer(v, pool_buf, pos0, pool_w, pool_scale)
    x = x + jnp.concatenate([y_ssd, y_pool], axis=-1) @ w_out
    hn = rmsnorm(x, norm2_w)
    u = hn @ w_up
    u, ffn_new = causal_dwconv(u, ffn_buf, ffn_conv_w, ffn_conv_b)
    gate, val = jnp.split(u, 2, axis=-1)
    x = x + (jax.nn.silu(gate) * val) @ w_down
    return x, h_new.astype(x.dtype), conv_new, pool_new, ffn_new


def setup_inputs(seed: int = 0) -> dict:
    key = jax.random.key(seed)
    ks = jax.random.split(key, 24)
    f32 = jnp.float32
    L = DEPTH

    def nrm(k, shape, scale):
        return jax.random.normal(k, shape, f32) * scale

    dt0 = jnp.exp(jax.random.uniform(ks[10], (L, SSD_HEADS), f32, math.log(1e-3), math.log(1e-1)))
    return {
        "x_prompt": nrm(ks[0], (BATCH, SEQ, D_MODEL), 1.0),
        "x_sample": nrm(ks[1], (DEC_BATCH, DEC_SEQ, D_MODEL), 1.0),
        "state_ssm": nrm(ks[2], (L, DEC_BATCH, SSD_HEADS, SSD_HEAD_DIM, SSD_STATE), 0.1),
        "state_conv": nrm(ks[3], (L, DEC_BATCH, SSD_CONV_W - 1, CONV_DIM), 1.0),
        "state_pool": nrm(ks[4], (L, DEC_BATCH, POOL_MAX - 1, POOL_WIDTH), 1.0),
        "state_ffn_conv": nrm(ks[5], (L, DEC_BATCH, FFN_CONV_W - 1, 2 * D_FF), 1.0),
        "norm1_w": 1.0 + nrm(ks[6], (L, D_MODEL), 0.02),
        "w_in": nrm(ks[7], (L, D_MODEL, IN_COLS), D_MODEL ** -0.5),
        "conv_w": nrm(ks[8], (L, SSD_CONV_W, CONV_DIM), SSD_CONV_W ** -0.5),
        "conv_b": nrm(ks[9], (L, CONV_DIM), 0.02),
        "dt_bias": dt0 + jnp.log(-jnp.expm1(-dt0)),
        "a_log": jnp.log(jax.random.uniform(ks[11], (L, SSD_HEADS), f32, 1.0, 16.0)),
        "d_skip": 1.0 + nrm(ks[12], (L, SSD_HEADS), 0.1),
        "ssd_norm_w": 1.0 + nrm(ks[13], (L, SSD_WIDTH), 0.02),
        "pool_w": nrm(ks[14], (L, POOL_GROUPS, POOL_GROUP_DIM, POOL_GROUP_DIM), POOL_GROUP_DIM ** -0.5),
        "pool_scale": 1.0 + nrm(ks[15], (L, POOL_WIDTH), 0.1),
        "w_out": nrm(ks[16], (L, D_MIX, D_MODEL), D_MIX ** -0.5),
        "norm2_w": 1.0 + nrm(ks[17], (L, D_MODEL), 0.02),
        "w_up": nrm(ks[18], (L, D_MODEL, 2 * D_FF), D_MODEL ** -0.5),
        "ffn_conv_w": nrm(ks[19], (L, FFN_CONV_W, 2 * D_FF), FFN_CONV_W ** -0.5),
        "ffn_conv_b": nrm(ks[20], (L, 2 * D_FF), 0.02),
        "w_down": nrm(ks[21], (L, D_FF, D_MODEL), D_FF ** -0.5),
        "final_norm_w": 1.0 + nrm(ks[22], (D_MODEL,), 0.02),
    }


def reference(x_prompt, x_sample, state_ssm, state_conv, state_pool, state_ffn_conv,
              norm1_w, w_in, conv_w, conv_b, dt_bias, a_log, d_skip, ssd_norm_w,
              pool_w, pool_scale, w_out, norm2_w, w_up, ffn_conv_w, ffn_conv_b, w_down,
              final_norm_w):
    dtp = x_prompt.dtype
    xp, xs = x_prompt, x_sample
    ssm_p, ssm_s, conv_p, conv_s, pool_p, pool_s, ffn_p, ffn_s = [], [], [], [], [], [], [], []
    for l in range(DEPTH):
        params = (norm1_w[l], w_in[l], conv_w[l], conv_b[l], dt_bias[l], a_log[l], d_skip[l],
                  ssd_norm_w[l], pool_w[l], pool_scale[l], w_out[l], norm2_w[l], w_up[l],
                  ffn_conv_w[l], ffn_conv_b[l], w_down[l])
        xp, hp, cp, pp, fp = hybrid_layer(
            xp, 0,
            jnp.zeros((BATCH, SSD_HEADS, SSD_HEAD_DIM, SSD_STATE), dtp),
            jnp.zeros((BATCH, SSD_CONV_W - 1, CONV_DIM), dtp),
            jnp.zeros((BATCH, POOL_MAX - 1, POOL_WIDTH), dtp),
            jnp.zeros((BATCH, FFN_CONV_W - 1, 2 * D_FF), dtp),
            *params)
        xs, hs, cs_, ps, fs = hybrid_layer(
            xs, PAST_LEN, state_ssm[l], state_conv[l], state_pool[l], state_ffn_conv[l], *params)
        ssm_p.append(hp); ssm_s.append(hs)
        conv_p.append(cp); conv_s.append(cs_)
        pool_p.append(pp); pool_s.append(ps)
        ffn_p.append(fp); ffn_s.append(fs)
    y_prompt = rmsnorm(xp, final_norm_w)
    y_sample = rmsnorm(xs, final_norm_w)
    return (y_prompt, y_sample,
            jnp.stack(ssm_p), jnp.stack(ssm_s),
            jnp.stack(conv_p), jnp.stack(conv_s),
            jnp.stack(pool_p), jnp.stack(pool_s),
            jnp.stack(ffn_p), jnp.stack(ffn_s))
```

```python
import functools

import jax
import jax.numpy as jnp
from jax import lax
from jax.experimental import pallas as pl
from jax.experimental.pallas import tpu as pltpu

F32 = jnp.float32
BF16 = jnp.bfloat16

D_MODEL = 1024
SSD_WIDTH = 1536
HEAD_DIM = 64
N_HEADS = SSD_WIDTH // HEAD_DIM
N_GROUPS = 4
HEADS_PER_GROUP = N_HEADS // N_GROUPS
D_STATE = 64
BC_WIDTH = N_GROUPS * D_STATE
CONV_DIM = SSD_WIDTH + 2 * BC_WIDTH
CONV_W = 4
CHUNK = 128
POOL_WIDTH = 512
POOL_WINDOWS = (2, 4, 8, 16)
POOL_GROUP_DIM = POOL_WIDTH // len(POOL_WINDOWS)
POOL_MAX = max(POOL_WINDOWS)
D_MIX = SSD_WIDTH + POOL_WIDTH
D_FF = 2816
FFN_CONV_W = 3
NORM_GROUP = SSD_WIDTH // N_GROUPS
EPS = 1e-6

LANES = 128
SUBLANES = 8
DT_PAD = LANES
VMEM_LIMIT = 56 * 1024 * 1024

_Z0, _Z1 = 0, SSD_WIDTH
_X0, _X1 = _Z1, _Z1 + CONV_DIM
_V0, _V1 = _X1, _X1 + POOL_WIDTH
_T0, _T1 = _V1, _V1 + DT_PAD


def _rms(x, w):
    ms = jnp.mean(x * x, axis=-1, keepdims=True)
    return x * lax.rsqrt(ms + EPS) * w


def _silu(x):
    return x * (1.0 / (1.0 + jnp.exp(-x)))


def _softplus(x):
    return jnp.maximum(x, 0.0) + jnp.log1p(jnp.exp(-jnp.abs(x)))


def _const_spec(shape):
    nd = len(shape)
    return pl.BlockSpec(shape, lambda *_: (0,) * nd, pipeline_mode=pl.Buffered(1))


def _inproj_kernel(x_ref, nw_ref, w_ref, z_ref, xbc_ref, v_ref, dt_ref):
    hn = _rms(x_ref[...], nw_ref[...]).astype(BF16)

    def mm(lo, hi):
        return jnp.dot(hn, w_ref[:, lo:hi], preferred_element_type=F32)

    z_ref[...] = mm(_Z0, _Z1)
    xbc_ref[...] = mm(_X0, _X1)
    v_ref[...] = mm(_V0, _V1)
    dt_ref[...] = mm(_T0, _T1)


def _inproj(x2d, norm_w, w_cat, tm):
    m = x2d.shape[0]
    row = lambda i: (i, 0)
    return pl.pallas_call(
        _inproj_kernel,
        grid=(m // tm,),
        in_specs=[pl.BlockSpec((tm, D_MODEL), row),
                  _const_spec((1, D_MODEL)),
                  _const_spec((D_MODEL, _T1))],
        out_specs=[pl.BlockSpec((tm, SSD_WIDTH), row),
                   pl.BlockSpec((tm, CONV_DIM), row),
                   pl.BlockSpec((tm, POOL_WIDTH), row),
                   pl.BlockSpec((tm, DT_PAD), row)],
        out_shape=[jax.ShapeDtypeStruct((m, SSD_WIDTH), F32),
                   jax.ShapeDtypeStruct((m, CONV_DIM), F32),
                   jax.ShapeDtypeStruct((m, POOL_WIDTH), F32),
                   jax.ShapeDtypeStruct((m, DT_PAD), F32)],
        compiler_params=pltpu.CompilerParams(dimension_semantics=("arbitrary",),
                                             vmem_limit_bytes=VMEM_LIMIT),
        name="inproj",
    )(x2d, norm_w, w_cat)


def _gate_and_norm(y, xs, z, dskip, nw):
    y = (y + dskip * xs) * _silu(z)
    outs = []
    for g in range(N_GROUPS):
        sl = slice(g * NORM_GROUP, (g + 1) * NORM_GROUP)
        outs.append(_rms(y[:, sl], nw[:, sl]))
    return jnp.concatenate(outs, axis=1)


def _pool_project(win_sums, v, cnts, pw_ref, pscale):
    outs = []
    for g in range(len(POOL_WINDOWS)):
        sl = slice(g * POOL_GROUP_DIM, (g + 1) * POOL_GROUP_DIM)
        m = win_sums[g] / cnts[g] - v[:, sl]
        outs.append(jnp.dot(m.astype(BF16), pw_ref[g], preferred_element_type=F32))
    return jnp.concatenate(outs, axis=1) * pscale


def _cumsum_rows(x):
    n = x.shape[0]
    row = lax.broadcasted_iota(jnp.int32, x.shape, 0)
    k = 1
    while k < n:
        x = x + jnp.where(row >= k, pltpu.roll(x, k, axis=0), 0.0)
        k *= 2
    return x


HALO = SUBLANES
PHALO = 2 * SUBLANES


def _ssd_prompt_kernel(xbc_ref, z_ref, dt_ref, v_ref, cw_ref, cb_ref, dtb_ref, alog_ref, dskip_ref,
                       nw_ref, pw_ref, ps_ref, ymix_ref, hfin_ref,
                       ext_ref, extv_ref, state_ref, yacc_ref, *, pos0):
    c = pl.program_id(1)
    nc = pl.num_programs(1)

    @pl.when(c == 0)
    def _():
        ext_ref[0:HALO, :] = jnp.zeros((HALO, CONV_DIM), F32)
        extv_ref[0:PHALO, :] = jnp.zeros((PHALO, POOL_WIDTH), F32)
        state_ref[...] = jnp.zeros(state_ref.shape, F32)

    ext_ref[HALO:HALO + CHUNK, :] = xbc_ref[...]
    conv = cb_ref[...]
    for k in range(CONV_W):
        off = HALO - (CONV_W - 1) + k
        conv = conv + ext_ref[off:off + CHUNK, :] * cw_ref[k:k + 1, :]
    xbc = _silu(conv)
    ext_ref[0:HALO, :] = ext_ref[CHUNK:CHUNK + HALO, :]

    xs = xbc[:, :SSD_WIDTH]
    bm = xbc[:, SSD_WIDTH:SSD_WIDTH + BC_WIDTH]
    cm = xbc[:, SSD_WIDTH + BC_WIDTH:]
    bm_t = bm.T

    dt = _softplus(dt_ref[...] + dtb_ref[...])
    da = dt * (-jnp.exp(alog_ref[...]))
    acum = _cumsum_rows(da)
    acum_t = acum.T
    dt_t = dt.T
    li = lax.broadcasted_iota(jnp.int32, (CHUNK, CHUNK), 0)
    si = lax.broadcasted_iota(jnp.int32, (CHUNK, CHUNK), 1)
    causal = li >= si

    xs_b = xs.astype(BF16)
    for g in range(N_GROUPS):
        gs = slice(g * D_STATE, (g + 1) * D_STATE)
        b_g = bm[:, gs]
        c_g = cm[:, gs]
        cb = lax.dot_general(c_g.astype(BF16), b_g.astype(BF16), (((1,), (1,)), ((), ())),
                             preferred_element_type=F32)
        bt_g = bm_t[gs, :]
        for r in range(HEADS_PER_GROUP):
            h = g * HEADS_PER_GROUP + r
            hs = slice(h * HEAD_DIM, (h + 1) * HEAD_DIM)
            a_l = acum[:, h:h + 1]
            a_s = acum_t[h:h + 1, :]
            dt_s = dt_t[h:h + 1, :]
            decay = jnp.exp(jnp.where(causal, a_l - a_s, -jnp.inf))
            mix = cb * decay * dt_s
            x_h = xs_b[:, hs]
            s_h = state_ref[h]
            y_h = jnp.dot(mix.astype(BF16), x_h, preferred_element_type=F32)
            y_h = y_h + jnp.dot((c_g * jnp.exp(a_l)).astype(BF16), s_h.astype(BF16),
                                preferred_element_type=F32)
            yacc_ref[:, hs] = y_h
            a_last = acum_t[h:h + 1, CHUNK - 1:CHUNK]
            w_s = jnp.exp(a_last - a_s) * dt_s
            state_ref[h] = jnp.exp(a_last) * s_h + jnp.dot(
                (bt_g * w_s).astype(BF16), x_h, preferred_element_type=F32)

    y = _gate_and_norm(yacc_ref[...], xs, z_ref[...], dskip_ref[...], nw_ref[...])
    ymix_ref[:, :SSD_WIDTH] = y.astype(BF16)

    @pl.when(c == nc - 1)
    def _():
        hfin_ref[0] = state_ref[...]

    v = v_ref[...]
    extv_ref[PHALO:PHALO + CHUNK, :] = v
    pos = pos0 + c * CHUNK + lax.broadcasted_iota(jnp.int32, (CHUNK, 1), 0)
    sums, cnts = [], []
    for g, w in enumerate(POOL_WINDOWS):
        sl = slice(g * POOL_GROUP_DIM, (g + 1) * POOL_GROUP_DIM)
        acc = v[:, sl]
        for j in range(1, w):
            acc = acc + extv_ref[PHALO - j:PHALO - j + CHUNK, sl]
        sums.append(acc)
        cnts.append(jnp.minimum(pos + 1, w).astype(F32))
    ypool = _pool_project(sums, v, cnts, pw_ref, ps_ref[...])
    ymix_ref[:, SSD_WIDTH:] = ypool.astype(BF16)
    extv_ref[0:PHALO, :] = extv_ref[CHUNK:CHUNK + PHALO, :]


def _ssd_prompt(xbc, z, dt, v, cw, cb, dtb, alog, dskip, nw, pw, ps, nseq, seq):
    nc = seq // CHUNK
    row = lambda b, c: (b * nc + c, 0)
    return pl.pallas_call(
        functools.partial(_ssd_prompt_kernel, pos0=0),
        grid=(nseq, nc),
        in_specs=[pl.BlockSpec((CHUNK, CONV_DIM), row),
                  pl.BlockSpec((CHUNK, SSD_WIDTH), row),
                  pl.BlockSpec((CHUNK, DT_PAD), row),
                  pl.BlockSpec((CHUNK, POOL_WIDTH), row),
                  _const_spec((CONV_W, CONV_DIM)),
                  _const_spec((1, CONV_DIM)),
                  _const_spec((1, DT_PAD)),
                  _const_spec((1, DT_PAD)),
                  _const_spec((1, SSD_WIDTH)),
                  _const_spec((1, SSD_WIDTH)),
                  _const_spec((len(POOL_WINDOWS), POOL_GROUP_DIM, POOL_GROUP_DIM)),
                  _const_spec((1, POOL_WIDTH))],
        out_specs=[pl.BlockSpec((CHUNK, D_MIX), row),
                   pl.BlockSpec((1, N_HEADS, D_STATE, HEAD_DIM), lambda b, c: (b, 0, 0, 0))],
        out_shape=[jax.ShapeDtypeStruct((nseq * seq, D_MIX), BF16),
                   jax.ShapeDtypeStruct((nseq, N_HEADS, D_STATE, HEAD_DIM), F32)],
        scratch_shapes=[pltpu.VMEM((HALO + CHUNK, CONV_DIM), F32),
                        pltpu.VMEM((PHALO + CHUNK, POOL_WIDTH), F32),
                        pltpu.VMEM((N_HEADS, D_STATE, HEAD_DIM), F32),
                        pltpu.VMEM((CHUNK, SSD_WIDTH), F32)],
        compiler_params=pltpu.CompilerParams(dimension_semantics=("arbitrary", "arbitrary"),
                                             vmem_limit_bytes=VMEM_LIMIT),
        name="ssd_prompt",
    )(xbc, z, dt, v, cw, cb, dtb, alog, dskip, nw, pw, ps)


def _step_pre_kernel(xbc_ref, dt_ref, cs_ref, cw_ref, cb_ref, dtb_ref, alog_ref,
                     xs_ref, xst_ref, dec_ref, bdt_ref, ce_ref):
    conv = cb_ref[...] + xbc_ref[...] * cw_ref[CONV_W - 1:CONV_W, :]
    for k in range(CONV_W - 1):
        conv = conv + cs_ref[:, k * CONV_DIM:(k + 1) * CONV_DIM] * cw_ref[k:k + 1, :]
    xbc = _silu(conv)
    xs = xbc[:, :SSD_WIDTH]
    xs_ref[...] = xs
    for i in range(SSD_WIDTH // LANES):
        sl = slice(i * LANES, (i + 1) * LANES)
        xst_ref[sl, :] = xs[:, sl].T
    dt = _softplus(dt_ref[...] + dtb_ref[...])
    dec = jnp.exp(dt * (-jnp.exp(alog_ref[...])))
    k_i = lax.broadcasted_iota(jnp.int32, (DT_PAD, N_HEADS * LANES), 0)
    c_i = lax.broadcasted_iota(jnp.int32, (DT_PAD, N_HEADS * LANES), 1)
    expand = (c_i // LANES == k_i).astype(F32)
    dt_e = jnp.dot(dt, expand, preferred_element_type=F32, precision=lax.Precision.HIGHEST)
    dec_ref[...] = jnp.dot(dec, expand, preferred_element_type=F32, precision=lax.Precision.HIGHEST)
    for g in range(N_GROUPS):
        b_g = xbc[:, SSD_WIDTH + g * D_STATE:SSD_WIDTH + (g + 1) * D_STATE]
        c_g = xbc[:, SSD_WIDTH + BC_WIDTH + g * D_STATE:SSD_WIDTH + BC_WIDTH + (g + 1) * D_STATE]
        b2 = jnp.concatenate([b_g, b_g], axis=1)
        c2 = jnp.concatenate([c_g, c_g], axis=1)
        for r in range(HEADS_PER_GROUP):
            h = g * HEADS_PER_GROUP + r
            sl = slice(h * LANES, (h + 1) * LANES)
            bdt_ref[:, sl] = b2 * dt_e[:, sl]
            ce_ref[:, sl] = c2


def _step_pre(xbc, dt, cs2d, cw, cb, dtb, alog):
    n = xbc.shape[0]
    wide = N_HEADS * LANES
    full = lambda shape: pl.BlockSpec(shape, lambda i: (0,) * len(shape))
    return pl.pallas_call(
        _step_pre_kernel,
        grid=(1,),
        in_specs=[full((n, CONV_DIM)), full((n, DT_PAD)), full((n, (CONV_W - 1) * CONV_DIM)),
                  full((CONV_W, CONV_DIM)), full((1, CONV_DIM)), full((1, DT_PAD)), full((1, DT_PAD))],
        out_specs=[full((n, SSD_WIDTH)), full((SSD_WIDTH, n)), full((n, wide)), full((n, wide)),
                   full((n, wide))],
        out_shape=[jax.ShapeDtypeStruct((n, SSD_WIDTH), F32),
                   jax.ShapeDtypeStruct((SSD_WIDTH, n), F32),
                   jax.ShapeDtypeStruct((n, wide), F32),
                   jax.ShapeDtypeStruct((n, wide), F32),
                   jax.ShapeDtypeStruct((n, wide), F32)],
        compiler_params=pltpu.CompilerParams(dimension_semantics=("arbitrary",),
                                             vmem_limit_bytes=VMEM_LIMIT),
        name="step_pre",
    )(xbc, dt, cs2d, cw, cb, dtb, alog)


STEP_BB = 8


def _step_state_kernel(h_ref, xt_ref, dec_ref, bdt_ref, ce_ref, hout_ref, yt_ref):
    def head(h, carry):
        rows = pl.ds(pl.multiple_of(h * HEAD_DIM, HEAD_DIM), HEAD_DIM)
        for j in range(STEP_BB):
            dec = dec_ref[j, pl.ds(h, 1), :][:, :D_STATE]
            bdt = bdt_ref[j, pl.ds(h, 1), :][:, :D_STATE]
            ce = ce_ref[j, pl.ds(h, 1), :][:, :D_STATE]
            xcol = xt_ref[0, rows, j:j + 1]
            hn = h_ref[j, h] * dec + xcol * bdt
            hout_ref[j, h] = hn
            yt_ref[0, rows, j:j + 1] = jnp.sum(hn * ce, axis=1, keepdims=True)
        return carry

    lax.fori_loop(0, N_HEADS, head, 0)


def _step_state(h0, xt_blk, dec3, bdt3, ce3):
    n = h0.shape[0]
    nb = n // STEP_BB
    coef = pl.BlockSpec((STEP_BB, N_HEADS, LANES), lambda i: (i, 0, 0))
    col = pl.BlockSpec((1, SSD_WIDTH, STEP_BB), lambda i: (i, 0, 0))
    hblk = pl.BlockSpec((STEP_BB, N_HEADS, HEAD_DIM, D_STATE), lambda i: (i, 0, 0, 0))
    return pl.pallas_call(
        _step_state_kernel,
        grid=(nb,),
        in_specs=[hblk, col, coef, coef, coef],
        out_specs=[hblk, col],
        out_shape=[jax.ShapeDtypeStruct(h0.shape, F32),
                   jax.ShapeDtypeStruct((nb, SSD_WIDTH, STEP_BB), F32)],
        compiler_params=pltpu.CompilerParams(dimension_semantics=("arbitrary",),
                                             vmem_limit_bytes=VMEM_LIMIT),
        name="step_state",
    )(h0, xt_blk, dec3, bdt3, ce3)


def _step_post_kernel(y_ref, xs_ref, z_ref, v_ref, ps_state_ref, dskip_ref, nw_ref, pw_ref, ps_ref,
                      ymix_ref, *, pos0):
    y = _gate_and_norm(y_ref[...], xs_ref[...], z_ref[...], dskip_ref[...], nw_ref[...])
    ymix_ref[:, :SSD_WIDTH] = y.astype(BF16)
    v = v_ref[...]
    nprev = POOL_MAX - 1
    sums, cnts = [], []
    for g, w in enumerate(POOL_WINDOWS):
        acc = v[:, g * POOL_GROUP_DIM:(g + 1) * POOL_GROUP_DIM]
        for j in range(1, w):
            lo = (nprev - j) * POOL_WIDTH + g * POOL_GROUP_DIM
            acc = acc + ps_state_ref[:, lo:lo + POOL_GROUP_DIM]
        sums.append(acc)
        cnts.append(float(min(pos0 + 1, w)))
    ypool = _pool_project(sums, v, cnts, pw_ref, ps_ref[...])
    ymix_ref[:, SSD_WIDTH:] = ypool.astype(BF16)


def _step_post(y, xs, z, v, ps2d, dskip, nw, pw, ps, pos0):
    n = y.shape[0]
    full = lambda shape: pl.BlockSpec(shape, lambda i: (0,) * len(shape))
    return pl.pallas_call(
        functools.partial(_step_post_kernel, pos0=pos0),
        grid=(1,),
        in_specs=[full((n, SSD_WIDTH)), full((n, SSD_WIDTH)), full((n, SSD_WIDTH)), full((n, POOL_WIDTH)),
                  full((n, (POOL_MAX - 1) * POOL_WIDTH)), full((1, SSD_WIDTH)), full((1, SSD_WIDTH)),
                  full((len(POOL_WINDOWS), POOL_GROUP_DIM, POOL_GROUP_DIM)), full((1, POOL_WIDTH))],
        out_specs=full((n, D_MIX)),
        out_shape=jax.ShapeDtypeStruct((n, D_MIX), BF16),
        compiler_params=pltpu.CompilerParams(dimension_semantics=("arbitrary",),
                                             vmem_limit_bytes=VMEM_LIMIT),
        name="step_post",
    )(y, xs, z, v, ps2d, dskip, nw, pw, ps)


def _outffn_kernel(*refs, tm, decode):
    if decode:
        (x_ref, ymix_ref, s0_ref, s1_ref, wout_ref, n2_ref, wup_ref, fcw_ref, fcb_ref, wdn_ref, fn_ref,
         y_ref, u_ref) = refs
    else:
        (x_ref, ymix_ref, wout_ref, n2_ref, wup_ref, fcw_ref, fcb_ref, wdn_ref, fn_ref,
         y_ref, ust_ref, ext_ref) = refs
    x1 = x_ref[...] + jnp.dot(ymix_ref[...], wout_ref[...], preferred_element_type=F32)
    hn = _rms(x1, n2_ref[...]).astype(BF16)
    u = jnp.dot(hn, wup_ref[...], preferred_element_type=F32)
    if decode:
        u_ref[...] = u
        prev2, prev1 = s0_ref[...], s1_ref[...]
    else:
        j = pl.program_id(1)

        @pl.when(j == 0)
        def _():
            ext_ref[0:HALO, :] = jnp.zeros((HALO, 2 * D_FF), F32)

        ext_ref[HALO:HALO + tm, :] = u
        prev2 = ext_ref[HALO - 2:HALO - 2 + tm, :]
        prev1 = ext_ref[HALO - 1:HALO - 1 + tm, :]
    conv = fcb_ref[...] + prev2 * fcw_ref[0:1, :] + prev1 * fcw_ref[1:2, :] + u * fcw_ref[2:3, :]
    act = (_silu(conv[:, :D_FF]) * conv[:, D_FF:]).astype(BF16)
    x2 = x1 + jnp.dot(act, wdn_ref[...], preferred_element_type=F32)
    y_ref[...] = _rms(x2, fn_ref[...])
    if not decode:
        ext_ref[0:HALO, :] = ext_ref[tm:tm + HALO, :]

        @pl.when(j == pl.num_programs(1) - 1)
        def _():
            ust_ref[0] = ext_ref[HALO - (FFN_CONV_W - 1):HALO, :]


def _outffn_prompt(x2d, ymix, wout, n2, wup, fcw, fcb, wdn, fn, nseq, seq, tm):
    nj = seq // tm
    row = lambda b, j: (b * nj + j, 0)
    return pl.pallas_call(
        functools.partial(_outffn_kernel, tm=tm, decode=False),
        grid=(nseq, nj),
        in_specs=[pl.BlockSpec((tm, D_MODEL), row),
                  pl.BlockSpec((tm, D_MIX), row),
                  _const_spec((D_MIX, D_MODEL)),
                  _const_spec((1, D_MODEL)),
                  _const_spec((D_MODEL, 2 * D_FF)),
                  _const_spec((FFN_CONV_W, 2 * D_FF)),
                  _const_spec((1, 2 * D_FF)),
                  _const_spec((D_FF, D_MODEL)),
                  _const_spec((1, D_MODEL))],
        out_specs=[pl.BlockSpec((tm, D_MODEL), row),
                   pl.BlockSpec((1, FFN_CONV_W - 1, 2 * D_FF), lambda b, j: (b, 0, 0))],
        out_shape=[jax.ShapeDtypeStruct((nseq * seq, D_MODEL), F32),
                   jax.ShapeDtypeStruct((nseq, FFN_CONV_W - 1, 2 * D_FF), F32)],
        scratch_shapes=[pltpu.VMEM((HALO + tm, 2 * D_FF), F32)],
        compiler_params=pltpu.CompilerParams(dimension_semantics=("arbitrary", "arbitrary"),
                                             vmem_limit_bytes=VMEM_LIMIT),
        name="outffn_prompt",
    )(x2d, ymix, wout, n2, wup, fcw, fcb, wdn, fn)


def _outffn_decode(x2d, ymix, fs2d, wout, n2, wup, fcw, fcb, wdn, fn):
    n = x2d.shape[0]
    full = lambda shape: pl.BlockSpec(shape, lambda i: (0,) * len(shape))
    return pl.pallas_call(
        functools.partial(_outffn_kernel, tm=n, decode=True),
        grid=(1,),
        in_specs=[full((n, D_MODEL)), full((n, D_MIX)),
                  pl.BlockSpec((n, 2 * D_FF), lambda i: (0, 0)),
                  pl.BlockSpec((n, 2 * D_FF), lambda i: (0, 1)),
                  _const_spec((D_MIX, D_MODEL)),
                  _const_spec((1, D_MODEL)),
                  _const_spec((D_MODEL, 2 * D_FF)),
                  _const_spec((FFN_CONV_W, 2 * D_FF)),
                  _const_spec((1, 2 * D_FF)),
                  _const_spec((D_FF, D_MODEL)),
                  _const_spec((1, D_MODEL))],
        out_specs=[full((n, D_MODEL)), full((n, 2 * D_FF))],
        out_shape=[jax.ShapeDtypeStruct((n, D_MODEL), F32),
                   jax.ShapeDtypeStruct((n, 2 * D_FF), F32)],
        compiler_params=pltpu.CompilerParams(dimension_semantics=("arbitrary",),
                                             vmem_limit_bytes=VMEM_LIMIT),
        name="outffn_decode",
    )(x2d, ymix, fs2d, fs2d, wout, n2, wup, fcw, fcb, wdn, fn)


def kernel(x_prompt, x_sample, state_ssm, state_conv, state_pool, state_ffn_conv, norm1_w, w_in, conv_w, conv_b,
           dt_bias, a_log, d_skip, ssd_norm_w, pool_w, pool_scale, w_out, norm2_w, w_up, ffn_conv_w, ffn_conv_b,
           w_down, final_norm_w):
    assert w_in.shape[0] == 1, "single-layer model"
    nseq, seq, _ = x_prompt.shape
    nsmp, one, _ = x_sample.shape
    assert one == 1 and seq % CHUNK == 0 and nsmp % STEP_BB == 0
    past_len = 16384

    wi = w_in[0]
    w_dt = jnp.pad(wi[:, SSD_WIDTH + CONV_DIM:SSD_WIDTH + CONV_DIM + N_HEADS], ((0, 0), (0, DT_PAD - N_HEADS)))
    w_cat = jnp.concatenate([wi[:, :SSD_WIDTH], wi[:, SSD_WIDTH:SSD_WIDTH + CONV_DIM],
                             wi[:, SSD_WIDTH + CONV_DIM + N_HEADS:], w_dt], axis=1).astype(BF16)
    n1 = norm1_w[0][None]
    cw, cb = conv_w[0], conv_b[0][None]
    dtb = jnp.pad(dt_bias[0], (0, DT_PAD - N_HEADS))[None]
    alog = jnp.pad(a_log[0], (0, DT_PAD - N_HEADS))[None]
    dskip = jnp.repeat(d_skip[0], HEAD_DIM)[None]
    nw = ssd_norm_w[0][None]
    pw = pool_w[0].astype(BF16)
    ps = pool_scale[0][None]
    wout = w_out[0].astype(BF16)
    n2 = norm2_w[0][None]
    wup = w_up[0].astype(BF16)
    fcw, fcb = ffn_conv_w[0], ffn_conv_b[0][None]
    wdn = w_down[0].astype(BF16)
    fn = final_norm_w[None]

    xp = x_prompt.reshape(nseq * seq, D_MODEL)
    z_p, xbc_p, v_p, dt_p = _inproj(xp, n1, w_cat, tm=512)
    ymix_p, hfin = _ssd_prompt(xbc_p, z_p, dt_p, v_p, cw, cb, dtb, alog, dskip, nw, pw, ps, nseq, seq)
    y_p, ffn_p = _outffn_prompt(xp, ymix_p, wout, n2, wup, fcw, fcb, wdn, fn, nseq, seq, tm=256)

    xs_in = x_sample.reshape(nsmp, D_MODEL)
    z_s, xbc_s, v_s, dt_s = _inproj(xs_in, n1, w_cat, tm=nsmp)
    cs2d = state_conv[0].reshape(nsmp, (CONV_W - 1) * CONV_DIM)
    xs_s, xst, dec_e, bdt_e, c_e = _step_pre(xbc_s, dt_s, cs2d, cw, cb, dtb, alog)
    nb = nsmp // STEP_BB
    xt_blk = xst.reshape(SSD_WIDTH, nb, STEP_BB).transpose(1, 0, 2)
    to3 = lambda a: a.reshape(nsmp, N_HEADS, LANES)
    h_new, yt_blk = _step_state(state_ssm[0], xt_blk, to3(dec_e), to3(bdt_e), to3(c_e))
    y0_s = yt_blk.transpose(0, 2, 1).reshape(nsmp, SSD_WIDTH)
    ps2d = state_pool[0].reshape(nsmp, (POOL_MAX - 1) * POOL_WIDTH)
    ymix_s = _step_post(y0_s, xs_s, z_s, v_s, ps2d, dskip, nw, pw, ps, past_len)
    fs2d = state_ffn_conv[0].reshape(nsmp, (FFN_CONV_W - 1) * 2 * D_FF)
    y_s, u_s = _outffn_decode(xs_in, ymix_s, fs2d, wout, n2, wup, fcw, fcb, wdn, fn)

    def roll_state(state, new_row):
        return jnp.concatenate([state[0][:, 1:], new_row[:, None]], axis=1)[None]

    return (y_p.reshape(nseq, seq, D_MODEL),
            y_s.reshape(nsmp, 1, D_MODEL),
            jnp.swapaxes(hfin, 2, 3)[None],
            h_new[None],
            xbc_p.reshape(nseq, seq, CONV_DIM)[:, seq - (CONV_W - 1):][None],
            roll_state(state_conv, xbc_s),
            v_p.reshape(nseq, seq, POOL_WIDTH)[:, seq - (POOL_MAX - 1):][None],
            roll_state(state_pool, v_s),
            ffn_p[None],
            roll_state(state_ffn_conv, u_s))
```

```python
import functools

import jax
import jax.numpy as jnp
from jax import lax
from jax.experimental import pallas as pl
from jax.experimental.pallas import tpu as pltpu

F32 = jnp.float32
BF16 = jnp.bfloat16

D_MODEL = 1024
SSD_WIDTH = 1536
HEAD_DIM = 64
N_HEADS = SSD_WIDTH // HEAD_DIM
N_GROUPS = 4
HEADS_PER_GROUP = N_HEADS // N_GROUPS
D_STATE = 64
BC_WIDTH = N_GROUPS * D_STATE
CONV_DIM = SSD_WIDTH + 2 * BC_WIDTH
CONV_W = 4
CHUNK = 128
POOL_WIDTH = 512
POOL_WINDOWS = (2, 4, 8, 16)
POOL_GROUP_DIM = POOL_WIDTH // len(POOL_WINDOWS)
POOL_MAX = max(POOL_WINDOWS)
D_MIX = SSD_WIDTH + POOL_WIDTH
D_FF = 2816
FFN_CONV_W = 3
NORM_GROUP = SSD_WIDTH // N_GROUPS
EPS = 1e-6
PAST_LEN = 16384

LANES = 128
SUBLANES = 8
DT_PAD = LANES
VMEM_LIMIT = 56 * 1024 * 1024

_Z0, _Z1 = 0, SSD_WIDTH
_X0, _X1 = _Z1, _Z1 + CONV_DIM
_V0, _V1 = _X1, _X1 + POOL_WIDTH
_T0, _T1 = _V1, _V1 + DT_PAD


def _rms(x, w):
    ms = jnp.mean(x * x, axis=-1, keepdims=True)
    return x * lax.rsqrt(ms + EPS) * w


def _silu(x):
    return x * (1.0 / (1.0 + jnp.exp(-x)))


def _softplus(x):
    return jnp.maximum(x, 0.0) + jnp.log1p(jnp.exp(-jnp.abs(x)))


def _const_spec(shape):
    nd = len(shape)
    return pl.BlockSpec(shape, lambda *_: (0,) * nd, pipeline_mode=pl.Buffered(1))


def _inproj_kernel(x_ref, nw_ref, w_ref, z_ref, xbc_ref, v_ref, dt_ref):
    hn = _rms(x_ref[...], nw_ref[...]).astype(BF16)

    def mm(lo, hi):
        return jnp.dot(hn, w_ref[:, lo:hi], preferred_element_type=F32)

    z_ref[...] = mm(_Z0, _Z1)
    xbc_ref[...] = mm(_X0, _X1)
    v_ref[...] = mm(_V0, _V1)
    dt_ref[...] = mm(_T0, _T1)


def _inproj(x2d, norm_w, w_cat, tm):
    m = x2d.shape[0]
    row = lambda i: (i, 0)
    return pl.pallas_call(
        _inproj_kernel,
        grid=(m // tm,),
        in_specs=[pl.BlockSpec((tm, D_MODEL), row),
                  _const_spec((1, D_MODEL)),
                  _const_spec((D_MODEL, _T1))],
        out_specs=[pl.BlockSpec((tm, SSD_WIDTH), row),
                   pl.BlockSpec((tm, CONV_DIM), row),
                   pl.BlockSpec((tm, POOL_WIDTH), row),
                   pl.BlockSpec((tm, DT_PAD), row)],
        out_shape=[jax.ShapeDtypeStruct((m, SSD_WIDTH), F32),
                   jax.ShapeDtypeStruct((m, CONV_DIM), F32),
                   jax.ShapeDtypeStruct((m, POOL_WIDTH), F32),
                   jax.ShapeDtypeStruct((m, DT_PAD), F32)],
        compiler_params=pltpu.CompilerParams(dimension_semantics=("arbitrary",),
                                             vmem_limit_bytes=VMEM_LIMIT),
        name="inproj",
    )(x2d, norm_w, w_cat)


def _gate_and_norm(y, xs, z, dskip, nw):
    y = (y + dskip * xs) * _silu(z)
    outs = []
    for g in range(N_GROUPS):
        sl = slice(g * NORM_GROUP, (g + 1) * NORM_GROUP)
        outs.append(_rms(y[:, sl], nw[:, sl]))
    return jnp.concatenate(outs, axis=1)


def _pool_project(win_sums, v, cnts, pw_ref, pscale):
    outs = []
    for g in range(len(POOL_WINDOWS)):
        sl = slice(g * POOL_GROUP_DIM, (g + 1) * POOL_GROUP_DIM)
        m = win_sums[g] / cnts[g] - v[:, sl]
        outs.append(jnp.dot(m.astype(BF16), pw_ref[g], preferred_element_type=F32))
    return jnp.concatenate(outs, axis=1) * pscale


def _cumsum_rows(x):
    n = x.shape[0]
    row = lax.broadcasted_iota(jnp.int32, x.shape, 0)
    k = 1
    while k < n:
        x = x + jnp.where(row >= k, pltpu.roll(x, k, axis=0), 0.0)
        k *= 2
    return x


HALO = SUBLANES
PHALO = 2 * SUBLANES


def _ssd_prompt_kernel(xbc_ref, z_ref, dt_ref, v_ref, cw_ref, cb_ref, dtb_ref, alog_ref, dskip_ref,
                       nw_ref, pw_ref, ps_ref, ymix_ref, hfin_ref,
                       ext_ref, extv_ref, state_ref, yacc_ref, *, pos0):
    c = pl.program_id(1)
    nc = pl.num_programs(1)

    @pl.when(c == 0)
    def _():
        ext_ref[0:HALO, :] = jnp.zeros((HALO, CONV_DIM), F32)
        extv_ref[0:PHALO, :] = jnp.zeros((PHALO, POOL_WIDTH), F32)
        state_ref[...] = jnp.zeros(state_ref.shape, F32)

    ext_ref[HALO:HALO + CHUNK, :] = xbc_ref[...]
    conv = cb_ref[...]
    for k in range(CONV_W):
        off = HALO - (CONV_W - 1) + k
        conv = conv + ext_ref[off:off + CHUNK, :] * cw_ref[k:k + 1, :]
    xbc = _silu(conv)
    ext_ref[0:HALO, :] = ext_ref[CHUNK:CHUNK + HALO, :]

    xs = xbc[:, :SSD_WIDTH]
    bm = xbc[:, SSD_WIDTH:SSD_WIDTH + BC_WIDTH]
    cm = xbc[:, SSD_WIDTH + BC_WIDTH:]
    bm_t = bm.T

    dt = _softplus(dt_ref[...] + dtb_ref[...])
    da = dt * (-jnp.exp(alog_ref[...]))
    acum = _cumsum_rows(da)
    acum_t = acum.T
    dt_t = dt.T
    li = lax.broadcasted_iota(jnp.int32, (CHUNK, CHUNK), 0)
    si = lax.broadcasted_iota(jnp.int32, (CHUNK, CHUNK), 1)
    causal = li >= si

    xs_b = xs.astype(BF16)
    for g in range(N_GROUPS):
        gs = slice(g * D_STATE, (g + 1) * D_STATE)
        b_g = bm[:, gs]
        c_g = cm[:, gs]
        cb = lax.dot_general(c_g.astype(BF16), b_g.astype(BF16), (((1,), (1,)), ((), ())),
                             preferred_element_type=F32)
        bt_g = bm_t[gs, :]
        for r in range(HEADS_PER_GROUP):
            h = g * HEADS_PER_GROUP + r
            hs = slice(h * HEAD_DIM, (h + 1) * HEAD_DIM)
            a_l = acum[:, h:h + 1]
            a_s = acum_t[h:h + 1, :]
            dt_s = dt_t[h:h + 1, :]
            decay = jnp.exp(jnp.where(causal, a_l - a_s, -jnp.inf))
            mix = cb * decay * dt_s
            x_h = xs_b[:, hs]
            s_h = state_ref[h]
            y_h = jnp.dot(mix.astype(BF16), x_h, preferred_element_type=F32)
            y_h = y_h + jnp.dot((c_g * jnp.exp(a_l)).astype(BF16), s_h.astype(BF16),
                                preferred_element_type=F32)
            yacc_ref[:, hs] = y_h
            a_last = acum_t[h:h + 1, CHUNK - 1:CHUNK]
            w_s = jnp.exp(a_last - a_s) * dt_s
            state_ref[h] = jnp.exp(a_last) * s_h + jnp.dot(
                (bt_g * w_s).astype(BF16), x_h, preferred_element_type=F32)

    y = _gate_and_norm(yacc_ref[...], xs, z_ref[...], dskip_ref[...], nw_ref[...])
    ymix_ref[:, :SSD_WIDTH] = y.astype(BF16)

    @pl.when(c == nc - 1)
    def _():
        hfin_ref[0] = state_ref[...]

    v = v_ref[...]
    extv_ref[PHALO:PHALO + CHUNK, :] = v
    pos = pos0 + c * CHUNK + lax.broadcasted_iota(jnp.int32, (CHUNK, 1), 0)
    sums, cnts = [], []
    for g, w in enumerate(POOL_WINDOWS):
        sl = slice(g * POOL_GROUP_DIM, (g + 1) * POOL_GROUP_DIM)
        acc = v[:, sl]
        for j in range(1, w):
            acc = acc + extv_ref[PHALO - j:PHALO - j + CHUNK, sl]
        sums.append(acc)
        cnts.append(jnp.minimum(pos + 1, w).astype(F32))
    ypool = _pool_project(sums, v, cnts, pw_ref, ps_ref[...])
    ymix_ref[:, SSD_WIDTH:] = ypool.astype(BF16)
    extv_ref[0:PHALO, :] = extv_ref[CHUNK:CHUNK + PHALO, :]


def _ssd_prompt(xbc, z, dt, v, cw, cb, dtb, alog, dskip, nw, pw, ps, nseq, seq):
    nc = seq // CHUNK
    row = lambda b, c: (b * nc + c, 0)
    return pl.pallas_call(
        functools.partial(_ssd_prompt_kernel, pos0=0),
        grid=(nseq, nc),
        in_specs=[pl.BlockSpec((CHUNK, CONV_DIM), row),
                  pl.BlockSpec((CHUNK, SSD_WIDTH), row),
                  pl.BlockSpec((CHUNK, DT_PAD), row),
                  pl.BlockSpec((CHUNK, POOL_WIDTH), row),
                  _const_spec((CONV_W, CONV_DIM)),
                  _const_spec((1, CONV_DIM)),
                  _const_spec((1, DT_PAD)),
                  _const_spec((1, DT_PAD)),
                  _const_spec((1, SSD_WIDTH)),
                  _const_spec((1, SSD_WIDTH)),
                  _const_spec((len(POOL_WINDOWS), POOL_GROUP_DIM, POOL_GROUP_DIM)),
                  _const_spec((1, POOL_WIDTH))],
        out_specs=[pl.BlockSpec((CHUNK, D_MIX), row),
                   pl.BlockSpec((1, N_HEADS, D_STATE, HEAD_DIM), lambda b, c: (b, 0, 0, 0))],
        out_shape=[jax.ShapeDtypeStruct((nseq * seq, D_MIX), BF16),
                   jax.ShapeDtypeStruct((nseq, N_HEADS, D_STATE, HEAD_DIM), F32)],
        scratch_shapes=[pltpu.VMEM((HALO + CHUNK, CONV_DIM), F32),
                        pltpu.VMEM((PHALO + CHUNK, POOL_WIDTH), F32),
                        pltpu.VMEM((N_HEADS, D_STATE, HEAD_DIM), F32),
                        pltpu.VMEM((CHUNK, SSD_WIDTH), F32)],
        compiler_params=pltpu.CompilerParams(dimension_semantics=("arbitrary", "arbitrary"),
                                             vmem_limit_bytes=VMEM_LIMIT),
        name="ssd_prompt",
    )(xbc, z, dt, v, cw, cb, dtb, alog, dskip, nw, pw, ps)


def _step_mix_kernel(xbc_ref, z_ref, dt_ref, v_ref, cs_ref, ps_ref, h_ref,
                     cw_ref, cb_ref, dtb_ref, alog_ref, dskip_ref, nw_ref, pw_ref, psc_ref,
                     hout_ref, ymix_ref, cso_ref, pso_ref,
                     xs_ref, xdt_ref, bt_ref, ct_ref, dect_ref, yt_ref, *, pos0):
    h = pl.program_id(0)
    nprev = POOL_MAX - 1

    @pl.when(h == 0)
    def _prepare():
        xnew = xbc_ref[...]
        conv = cb_ref[...] + xnew * cw_ref[CONV_W - 1:CONV_W, :]
        for k in range(CONV_W - 1):
            conv = conv + cs_ref[k] * cw_ref[k:k + 1, :]
        xbc = _silu(conv)
        xs = xbc[:, :SSD_WIDTH]
        xs_ref[...] = xs
        dt = _softplus(dt_ref[...] + dtb_ref[...])
        dect_ref[...] = jnp.exp(dt * (-jnp.exp(alog_ref[...]))).T
        dt_t = dt.T
        for hh in range(N_HEADS):
            blk = xs[:, (hh // 2) * LANES:(hh // 2 + 1) * LANES].T
            half = blk[(hh % 2) * HEAD_DIM:(hh % 2 + 1) * HEAD_DIM, :]
            xdt_ref[hh * HEAD_DIM:(hh + 1) * HEAD_DIM, :] = half * dt_t[hh:hh + 1, :]
        for i in range(BC_WIDTH // LANES):
            sl = slice(i * LANES, (i + 1) * LANES)
            bt_ref[sl, :] = xbc[:, SSD_WIDTH + i * LANES:SSD_WIDTH + (i + 1) * LANES].T
            ct_ref[sl, :] = xbc[:, SSD_WIDTH + BC_WIDTH + i * LANES:SSD_WIDTH + BC_WIDTH + (i + 1) * LANES].T
        for k in range(CONV_W - 2):
            cso_ref[k] = cs_ref[k + 1]
        cso_ref[CONV_W - 2] = xnew
        for k in range(nprev - 1):
            pso_ref[k] = ps_ref[k + 1]
        pso_ref[nprev - 1] = v_ref[...]

    g = lax.div(h, HEADS_PER_GROUP)
    hrow = pl.multiple_of(h * HEAD_DIM, HEAD_DIM)
    grow = pl.multiple_of(g * D_STATE, D_STATE)
    b_t = bt_ref[pl.ds(grow, D_STATE), :]
    c_t = ct_ref[pl.ds(grow, D_STATE), :]
    dec = dect_ref[pl.ds(h, 1), :]

    def per_p(p, carry):
        xrow = xdt_ref[pl.ds(hrow + p, 1), :]
        hn = h_ref[0, p] * dec + xrow * b_t
        hout_ref[0, p] = hn
        yt_ref[pl.ds(hrow + p, 1), :] = jnp.sum(hn * c_t, axis=0, keepdims=True)
        return carry

    lax.fori_loop(0, HEAD_DIM, per_p, 0, unroll=4)

    @pl.when(h == pl.num_programs(0) - 1)
    def _finish():
        y0 = jnp.concatenate([yt_ref[i * LANES:(i + 1) * LANES, :].T for i in range(SSD_WIDTH // LANES)], axis=1)
        y = _gate_and_norm(y0, xs_ref[...], z_ref[...], dskip_ref[...], nw_ref[...])
        ymix_ref[:, :SSD_WIDTH] = y.astype(BF16)
        v = v_ref[...]
        sums, cnts = [], []
        for gi, w in enumerate(POOL_WINDOWS):
            sl = slice(gi * POOL_GROUP_DIM, (gi + 1) * POOL_GROUP_DIM)
            acc = v[:, sl]
            for j in range(1, w):
                acc = acc + ps_ref[nprev - j][:, sl]
            sums.append(acc)
            cnts.append(float(min(pos0 + 1, w)))
        ypool = _pool_project(sums, v, cnts, pw_ref, psc_ref[...])
        ymix_ref[:, SSD_WIDTH:] = ypool.astype(BF16)


def _step_mix(xbc, z, dt, v, cs_t, ps_t, h_t, cw, cb, dtb, alog, dskip, nw, pw, psc, pos0):
    n = xbc.shape[0]
    nprev = POOL_MAX - 1
    full = lambda shape: pl.BlockSpec(shape, lambda i: (0,) * len(shape))
    hblk = pl.BlockSpec((1, HEAD_DIM, D_STATE, n), lambda i: (i, 0, 0, 0))
    return pl.pallas_call(
        functools.partial(_step_mix_kernel, pos0=pos0),
        grid=(N_HEADS,),
        in_specs=[full((n, CONV_DIM)), full((n, SSD_WIDTH)), full((n, DT_PAD)), full((n, POOL_WIDTH)),
                  full((CONV_W - 1, n, CONV_DIM)), full((nprev, n, POOL_WIDTH)), hblk,
                  full((CONV_W, CONV_DIM)), full((1, CONV_DIM)), full((1, DT_PAD)), full((1, DT_PAD)),
                  full((1, SSD_WIDTH)), full((1, SSD_WIDTH)),
                  full((len(POOL_WINDOWS), POOL_GROUP_DIM, POOL_GROUP_DIM)), full((1, POOL_WIDTH))],
        out_specs=[hblk, full((n, D_MIX)), full((CONV_W - 1, n, CONV_DIM)), full((nprev, n, POOL_WIDTH))],
        out_shape=[jax.ShapeDtypeStruct(h_t.shape, F32),
                   jax.ShapeDtypeStruct((n, D_MIX), BF16),
                   jax.ShapeDtypeStruct((CONV_W - 1, n, CONV_DIM), F32),
                   jax.ShapeDtypeStruct((nprev, n, POOL_WIDTH), F32)],
        scratch_shapes=[pltpu.VMEM((n, SSD_WIDTH), F32),
                        pltpu.VMEM((SSD_WIDTH, n), F32),
                        pltpu.VMEM((BC_WIDTH, n), F32),
                        pltpu.VMEM((BC_WIDTH, n), F32),
                        pltpu.VMEM((DT_PAD, n), F32),
                        pltpu.VMEM((SSD_WIDTH, n), F32)],
        compiler_params=pltpu.CompilerParams(dimension_semantics=("arbitrary",),
                                             vmem_limit_bytes=VMEM_LIMIT),
        name="step_mix",
    )(xbc, z, dt, v, cs_t, ps_t, h_t, cw, cb, dtb, alog, dskip, nw, pw, psc)


FFN_CHUNK = 256


def _delay_rows(u, prev8, s):
    r = pltpu.roll(u, s, axis=0)
    c = pltpu.roll(prev8, s, axis=0)
    row = lax.broadcasted_iota(jnp.int32, prev8.shape, 0)
    top = jnp.where(row < s, c, r[0:SUBLANES])
    return jnp.concatenate([top, r[SUBLANES:]], axis=0)


def _outffn_kernel(*refs, tm, decode):
    if decode:
        (x_ref, ymix_ref, st_ref, wout_ref, n2_ref, wup_ref, fcw_ref, fcb_ref, wdn_ref, fn_ref,
         y_ref, sto_ref) = refs
    else:
        (x_ref, ymix_ref, wout_ref, n2_ref, wup_ref, fcw_ref, fcb_ref, wdn_ref, fn_ref,
         y_ref, ust_ref, carry_ref) = refs
        j = pl.program_id(1)

        @pl.when(j == 0)
        def _():
            carry_ref[...] = jnp.zeros(carry_ref.shape, F32)

    x1 = x_ref[...] + jnp.dot(ymix_ref[...], wout_ref[...], preferred_element_type=F32)
    hn = _rms(x1, n2_ref[...]).astype(BF16)
    acc = x1
    nchunk = D_FF // FFN_CHUNK

    def col_slices(k):
        return [slice(base + k * FFN_CHUNK, base + (k + 1) * FFN_CHUNK) for base in (0, D_FF)]

    def up_proj(k):
        return [jnp.dot(hn, wup_ref[:, cs], preferred_element_type=F32) for cs in col_slices(k)]

    u_next = up_proj(0)
    for k in range(nchunk):
        u_cur = u_next
        if k + 1 < nchunk:
            u_next = up_proj(k + 1)
        halves = []
        for u, cs in zip(u_cur, col_slices(k)):
            if decode:
                prev2, prev1 = st_ref[:, 0, cs], st_ref[:, 1, cs]
                sto_ref[:, 0, cs] = prev1
                sto_ref[:, 1, cs] = u
            else:
                prev8 = carry_ref[:, cs]
                prev1 = _delay_rows(u, prev8, 1)
                prev2 = _delay_rows(u, prev8, 2)
                carry_ref[:, cs] = u[tm - SUBLANES:tm]
            halves.append(fcb_ref[:, cs] + prev2 * fcw_ref[0:1, cs] + prev1 * fcw_ref[1:2, cs]
                          + u * fcw_ref[2:3, cs])
        act = (_silu(halves[0]) * halves[1]).astype(BF16)
        acc = acc + jnp.dot(act, wdn_ref[k * FFN_CHUNK:(k + 1) * FFN_CHUNK, :], preferred_element_type=F32)
    y_ref[...] = _rms(acc, fn_ref[...])
    if not decode:
        @pl.when(j == pl.num_programs(1) - 1)
        def _():
            ust_ref[0] = carry_ref[SUBLANES - (FFN_CONV_W - 1):SUBLANES, :]


def _outffn_prompt(x2d, ymix, wout, n2, wup, fcw, fcb, wdn, fn, nseq, seq, tm):
    nj = seq // tm
    row = lambda b, j: (b * nj + j, 0)
    return pl.pallas_call(
        functools.partial(_outffn_kernel, tm=tm, decode=False),
        grid=(nseq, nj),
        in_specs=[pl.BlockSpec((tm, D_MODEL), row),
                  pl.BlockSpec((tm, D_MIX), row),
                  _const_spec((D_MIX, D_MODEL)),
                  _const_spec((1, D_MODEL)),
                  _const_spec((D_MODEL, 2 * D_FF)),
                  _const_spec((FFN_CONV_W, 2 * D_FF)),
                  _const_spec((1, 2 * D_FF)),
                  _const_spec((D_FF, D_MODEL)),
                  _const_spec((1, D_MODEL))],
        out_specs=[pl.BlockSpec((tm, D_MODEL), row),
                   pl.BlockSpec((1, FFN_CONV_W - 1, 2 * D_FF), lambda b, j: (b, 0, 0))],
        out_shape=[jax.ShapeDtypeStruct((nseq * seq, D_MODEL), F32),
                   jax.ShapeDtypeStruct((nseq, FFN_CONV_W - 1, 2 * D_FF), F32)],
        scratch_shapes=[pltpu.VMEM((SUBLANES, 2 * D_FF), F32)],
        compiler_params=pltpu.CompilerParams(dimension_semantics=("arbitrary", "arbitrary"),
                                             vmem_limit_bytes=VMEM_LIMIT),
        name="outffn_prompt",
    )(x2d, ymix, wout, n2, wup, fcw, fcb, wdn, fn)


def _outffn_decode(x2d, ymix, ffn_state, wout, n2, wup, fcw, fcb, wdn, fn):
    n = x2d.shape[0]
    full = lambda shape: pl.BlockSpec(shape, lambda i: (0,) * len(shape))
    return pl.pallas_call(
        functools.partial(_outffn_kernel, tm=n, decode=True),
        grid=(1,),
        in_specs=[full((n, D_MODEL)), full((n, D_MIX)), full(ffn_state.shape),
                  _const_spec((D_MIX, D_MODEL)),
                  _const_spec((1, D_MODEL)),
                  _const_spec((D_MODEL, 2 * D_FF)),
                  _const_spec((FFN_CONV_W, 2 * D_FF)),
                  _const_spec((1, 2 * D_FF)),
                  _const_spec((D_FF, D_MODEL)),
                  _const_spec((1, D_MODEL))],
        out_specs=[full((n, D_MODEL)), full(ffn_state.shape)],
        out_shape=[jax.ShapeDtypeStruct((n, D_MODEL), F32),
                   jax.ShapeDtypeStruct(ffn_state.shape, F32)],
        compiler_params=pltpu.CompilerParams(dimension_semantics=("arbitrary",),
                                             vmem_limit_bytes=VMEM_LIMIT),
        name="outffn_decode",
    )(x2d, ymix, ffn_state, wout, n2, wup, fcw, fcb, wdn, fn)


def kernel(x_prompt, x_sample, state_ssm, state_conv, state_pool, state_ffn_conv, norm1_w, w_in, conv_w, conv_b,
           dt_bias, a_log, d_skip, ssd_norm_w, pool_w, pool_scale, w_out, norm2_w, w_up, ffn_conv_w, ffn_conv_b,
           w_down, final_norm_w):
    assert w_in.shape[0] == 1, "single-layer model"
    nseq, seq, _ = x_prompt.shape
    nsmp, one, _ = x_sample.shape
    assert one == 1 and seq % CHUNK == 0 and nsmp == LANES

    wi = w_in[0]
    w_dt = jnp.pad(wi[:, SSD_WIDTH + CONV_DIM:SSD_WIDTH + CONV_DIM + N_HEADS], ((0, 0), (0, DT_PAD - N_HEADS)))
    w_cat = jnp.concatenate([wi[:, :SSD_WIDTH], wi[:, SSD_WIDTH:SSD_WIDTH + CONV_DIM],
                             wi[:, SSD_WIDTH + CONV_DIM + N_HEADS:], w_dt], axis=1).astype(BF16)
    n1 = norm1_w[0][None]
    cw, cb = conv_w[0], conv_b[0][None]
    dtb = jnp.pad(dt_bias[0], (0, DT_PAD - N_HEADS))[None]
    alog = jnp.pad(a_log[0], (0, DT_PAD - N_HEADS))[None]
    dskip = jnp.repeat(d_skip[0], HEAD_DIM)[None]
    nw = ssd_norm_w[0][None]
    pw = pool_w[0].astype(BF16)
    ps = pool_scale[0][None]
    wout = w_out[0].astype(BF16)
    n2 = norm2_w[0][None]
    wup = w_up[0].astype(BF16)
    fcw, fcb = ffn_conv_w[0], ffn_conv_b[0][None]
    wdn = w_down[0].astype(BF16)
    fn = final_norm_w[None]

    xp = x_prompt.reshape(nseq * seq, D_MODEL)
    z_p, xbc_p, v_p, dt_p = _inproj(xp, n1, w_cat, tm=512)
    ymix_p, hfin = _ssd_prompt(xbc_p, z_p, dt_p, v_p, cw, cb, dtb, alog, dskip, nw, pw, ps, nseq, seq)
    y_p, ffn_p = _outffn_prompt(xp, ymix_p, wout, n2, wup, fcw, fcb, wdn, fn, nseq, seq, tm=512)

    xs_in = x_sample.reshape(nsmp, D_MODEL)
    z_s, xbc_s, v_s, dt_s = _inproj(xs_in, n1, w_cat, tm=nsmp)
    h_t = jnp.transpose(state_ssm[0], (1, 2, 3, 0))
    cs_t = jnp.transpose(state_conv[0], (1, 0, 2))
    ps_t = jnp.transpose(state_pool[0], (1, 0, 2))
    hnew_t, ymix_s, cs_new, ps_new = _step_mix(xbc_s, z_s, dt_s, v_s, cs_t, ps_t, h_t, cw, cb, dtb, alog,
                                               dskip, nw, pw, ps, PAST_LEN)
    y_s, ffn_new = _outffn_decode(xs_in, ymix_s, state_ffn_conv[0], wout, n2, wup, fcw, fcb, wdn, fn)

    return (y_p.reshape(nseq, seq, D_MODEL),
            y_s.reshape(nsmp, 1, D_MODEL),
            jnp.swapaxes(hfin, 2, 3)[None],
            jnp.transpose(hnew_t, (3, 0, 1, 2))[None],
            xbc_p.reshape(nseq, seq, CONV_DIM)[:, seq - (CONV_W - 1):][None],
            jnp.transpose(cs_new, (1, 0, 2))[None],
            v_p.reshape(nseq, seq, POOL_WIDTH)[:, seq - (POOL_MAX - 1):][None],
            jnp.transpose(ps_new, (1, 0, 2))[None],
            ffn_p[None],
            ffn_new[None])
```

```python
import functools

import jax
import jax.numpy as jnp
from jax import lax
from jax.experimental import pallas as pl
from jax.experimental.pallas import tpu as pltpu

F32 = jnp.float32
BF16 = jnp.bfloat16

D_MODEL = 1024
SSD_WIDTH = 1536
HEAD_DIM = 64
N_HEADS = SSD_WIDTH // HEAD_DIM
N_GROUPS = 4
HEADS_PER_GROUP = N_HEADS // N_GROUPS
D_STATE = 64
BC_WIDTH = N_GROUPS * D_STATE
CONV_DIM = SSD_WIDTH + 2 * BC_WIDTH
CONV_W = 4
CHUNK = 128
POOL_WIDTH = 512
POOL_WINDOWS = (2, 4, 8, 16)
POOL_GROUP_DIM = POOL_WIDTH // len(POOL_WINDOWS)
POOL_MAX = max(POOL_WINDOWS)
D_MIX = SSD_WIDTH + POOL_WIDTH
D_FF = 2816
FFN_CONV_W = 3
NORM_GROUP = SSD_WIDTH // N_GROUPS
EPS = 1e-6
PAST_LEN = 16384

LANES = 128
SUBLANES = 8
DT_PAD = LANES
VMEM_LIMIT = 56 * 1024 * 1024
HEAD_PAIRS = N_HEADS // 2

_Z0, _Z1 = 0, SSD_WIDTH
_X0, _X1 = _Z1, _Z1 + CONV_DIM
_V0, _V1 = _X1, _X1 + POOL_WIDTH
_T0, _T1 = _V1, _V1 + DT_PAD


def _rms(x, w):
    ms = jnp.mean(x * x, axis=-1, keepdims=True)
    return x * lax.rsqrt(ms + EPS) * w


def _silu(x):
    return x * (1.0 / (1.0 + jnp.exp(-x)))


def _softplus(x):
    return jnp.maximum(x, 0.0) + jnp.log1p(jnp.exp(-jnp.abs(x)))


def _const_spec(shape):
    nd = len(shape)
    return pl.BlockSpec(shape, lambda *_: (0,) * nd, pipeline_mode=pl.Buffered(1))


def _delay_rows(u, prev8, s):
    if s == SUBLANES:
        return jnp.concatenate([prev8, u[:u.shape[0] - SUBLANES]], axis=0)
    r = pltpu.roll(u, s, axis=0)
    c = pltpu.roll(prev8, s, axis=0)
    row = lax.broadcasted_iota(jnp.int32, prev8.shape, 0)
    top = jnp.where(row < s, c, r[0:SUBLANES])
    return jnp.concatenate([top, r[SUBLANES:]], axis=0)


def _gate_and_norm(y, xs, gate, dskip, nw):
    y = (y + dskip * xs) * gate
    outs = []
    for g in range(N_GROUPS):
        sl = slice(g * NORM_GROUP, (g + 1) * NORM_GROUP)
        outs.append(_rms(y[:, sl], nw[:, sl]))
    return jnp.concatenate(outs, axis=1)


def _pool_project(win_sums, v, cnts, pw_ref, pscale):
    outs = []
    for g in range(len(POOL_WINDOWS)):
        sl = slice(g * POOL_GROUP_DIM, (g + 1) * POOL_GROUP_DIM)
        m = win_sums[g] / cnts[g] - v[:, sl]
        outs.append(jnp.dot(m.astype(BF16), pw_ref[g], preferred_element_type=F32))
    return jnp.concatenate(outs, axis=1) * pscale


def _cumsum_rows(x):
    n = x.shape[0]
    row = lax.broadcasted_iota(jnp.int32, x.shape, 0)
    k = 1
    while k < n:
        x = x + jnp.where(row >= k, pltpu.roll(x, k, axis=0), 0.0)
        k *= 2
    return x


def _inproj_kernel(*refs, tm, prompt, pos0):
    if prompt:
        (x_ref, nw_ref, w_ref, cw_ref, cb_ref, dtb_ref, pw_ref, psc_ref,
         xs_ref, bc_ref, dt_ref, gate_ref, yp_ref, ctail_ref, vtail_ref,
         cc_ref, vc_ref, s2c_ref, s4c_ref, s8c_ref) = refs
    else:
        x_ref, nw_ref, w_ref, z_ref, xbc_ref, v_ref, dt_ref = refs
    hn = _rms(x_ref[...], nw_ref[...]).astype(BF16)

    def mm(lo, hi):
        return jnp.dot(hn, w_ref[:, lo:hi], preferred_element_type=F32)

    if not prompt:
        z_ref[...] = mm(_Z0, _Z1)
        xbc_ref[...] = mm(_X0, _X1)
        v_ref[...] = mm(_V0, _V1)
        dt_ref[...] = mm(_T0, _T1)
        return

    j = pl.program_id(1)
    carries = (cc_ref, vc_ref, s2c_ref, s4c_ref, s8c_ref)

    @pl.when(j == 0)
    def _():
        for r in carries:
            r[...] = jnp.zeros(r.shape, F32)

    last = j == pl.num_programs(1) - 1

    def conv_stage(lo, hi):
        def fn(xbc):
            prev8 = cc_ref[:, lo:hi]
            conv = cb_ref[:, lo:hi] + xbc * cw_ref[CONV_W - 1:CONV_W, lo:hi]
            for k in range(CONV_W - 1):
                conv = conv + _delay_rows(xbc, prev8, CONV_W - 1 - k) * cw_ref[k:k + 1, lo:hi]
            cc_ref[:, lo:hi] = xbc[tm - SUBLANES:tm]
            act = _silu(conv)
            if hi <= SSD_WIDTH:
                xs_ref[:, lo:hi] = act.astype(BF16)
            else:
                bc_ref[:, lo - SSD_WIDTH:hi - SSD_WIDTH] = act
        return fn

    def pool_stage(v):
        s2 = v + _delay_rows(v, vc_ref[...], 1)
        s4 = s2 + _delay_rows(s2, s2c_ref[...], 2)
        s8 = s4 + _delay_rows(s4, s4c_ref[...], 4)
        s16 = s8 + _delay_rows(s8, s8c_ref[...], 8)
        for r, val in zip(carries[1:], (v, s2, s4, s8)):
            r[...] = val[tm - SUBLANES:tm]
        pos = pos0 + j * tm + lax.broadcasted_iota(jnp.int32, (tm, 1), 0)
        sums, cnts = [], []
        for g, (w, s) in enumerate(zip(POOL_WINDOWS, (s2, s4, s8, s16))):
            sums.append(s[:, g * POOL_GROUP_DIM:(g + 1) * POOL_GROUP_DIM])
            cnts.append(jnp.minimum(pos + 1, w).astype(F32))
        yp_ref[...] = _pool_project(sums, v, cnts, pw_ref, psc_ref[...]).astype(BF16)

        @pl.when(last)
        def _():
            vtail_ref[0] = v[tm - 2 * SUBLANES:tm]

    def gate_stage(lo, hi):
        def fn(z):
            gate_ref[:, lo:hi] = _silu(z)
        return fn

    def dt_stage(dt_raw):
        dt_ref[...] = _softplus(dt_raw + dtb_ref[...])

    stages = [((_Z0, _Z1), gate_stage(0, SSD_WIDTH)),
              ((_T0, _T1), dt_stage),
              ((_X0, _X0 + SSD_WIDTH), conv_stage(0, SSD_WIDTH)),
              ((_X0 + SSD_WIDTH, _X1), conv_stage(SSD_WIDTH, CONV_DIM)),
              ((_V0, _V1), pool_stage)]
    for cols, fn in stages:
        fn(mm(*cols))

    @pl.when(last)
    def _():
        ctail_ref[0] = cc_ref[...]


def _inproj(x2d, norm_w, w_cat, tm):
    m = x2d.shape[0]
    row = lambda i: (i, 0)
    return pl.pallas_call(
        functools.partial(_inproj_kernel, tm=tm, prompt=False, pos0=0),
        grid=(m // tm,),
        in_specs=[pl.BlockSpec((tm, D_MODEL), row),
                  _const_spec((1, D_MODEL)),
                  _const_spec((D_MODEL, _T1))],
        out_specs=[pl.BlockSpec((tm, SSD_WIDTH), row),
                   pl.BlockSpec((tm, CONV_DIM), row),
                   pl.BlockSpec((tm, POOL_WIDTH), row),
                   pl.BlockSpec((tm, DT_PAD), row)],
        out_shape=[jax.ShapeDtypeStruct((m, SSD_WIDTH), F32),
                   jax.ShapeDtypeStruct((m, CONV_DIM), F32),
                   jax.ShapeDtypeStruct((m, POOL_WIDTH), F32),
                   jax.ShapeDtypeStruct((m, DT_PAD), F32)],
        compiler_params=pltpu.CompilerParams(dimension_semantics=("arbitrary",),
                                             vmem_limit_bytes=VMEM_LIMIT),
        name="inproj",
    )(x2d, norm_w, w_cat)


def _inproj_prompt(x2d, norm_w, w_cat, cw, cb, dtb, pw, psc, nseq, seq, tm):
    nj = seq // tm
    m = nseq * seq
    row = lambda b, j: (b * nj + j, 0)
    per_seq = lambda b, j: (b, 0, 0)
    return pl.pallas_call(
        functools.partial(_inproj_kernel, tm=tm, prompt=True, pos0=0),
        grid=(nseq, nj),
        in_specs=[pl.BlockSpec((tm, D_MODEL), row),
                  _const_spec((1, D_MODEL)),
                  _const_spec((D_MODEL, _T1)),
                  _const_spec((CONV_W, CONV_DIM)),
                  _const_spec((1, CONV_DIM)),
                  _const_spec((1, DT_PAD)),
                  _const_spec((len(POOL_WINDOWS), POOL_GROUP_DIM, POOL_GROUP_DIM)),
                  _const_spec((1, POOL_WIDTH))],
        out_specs=[pl.BlockSpec((tm, SSD_WIDTH), row),
                   pl.BlockSpec((tm, 2 * BC_WIDTH), row),
                   pl.BlockSpec((tm, DT_PAD), row),
                   pl.BlockSpec((tm, SSD_WIDTH), row),
                   pl.BlockSpec((tm, POOL_WIDTH), row),
                   pl.BlockSpec((1, SUBLANES, CONV_DIM), per_seq),
                   pl.BlockSpec((1, 2 * SUBLANES, POOL_WIDTH), per_seq)],
        out_shape=[jax.ShapeDtypeStruct((m, SSD_WIDTH), BF16),
                   jax.ShapeDtypeStruct((m, 2 * BC_WIDTH), F32),
                   jax.ShapeDtypeStruct((m, DT_PAD), F32),
                   jax.ShapeDtypeStruct((m, SSD_WIDTH), F32),
                   jax.ShapeDtypeStruct((m, POOL_WIDTH), BF16),
                   jax.ShapeDtypeStruct((nseq, SUBLANES, CONV_DIM), F32),
                   jax.ShapeDtypeStruct((nseq, 2 * SUBLANES, POOL_WIDTH), F32)],
        scratch_shapes=[pltpu.VMEM((SUBLANES, CONV_DIM), F32),
                        pltpu.VMEM((SUBLANES, POOL_WIDTH), F32),
                        pltpu.VMEM((SUBLANES, POOL_WIDTH), F32),
                        pltpu.VMEM((SUBLANES, POOL_WIDTH), F32),
                        pltpu.VMEM((SUBLANES, POOL_WIDTH), F32)],
        compiler_params=pltpu.CompilerParams(dimension_semantics=("arbitrary", "arbitrary"),
                                             vmem_limit_bytes=VMEM_LIMIT),
        name="inproj_prompt",
    )(x2d, norm_w, w_cat, cw, cb, dtb, pw, psc)


def _ssd_prompt_kernel(xs_ref, bc_ref, dt_ref, gate_ref, alog_ref, dskip_ref, nw_ref,
                       y_ref, hfin_ref, state_ref, yacc_ref):
    c = pl.program_id(1)

    @pl.when(c == 0)
    def _():
        state_ref[...] = jnp.zeros(state_ref.shape, F32)

    xs = xs_ref[...]
    bm = bc_ref[:, :BC_WIDTH]
    cm = bc_ref[:, BC_WIDTH:]
    bm_t = bm.T

    dt = dt_ref[...]
    acum = _cumsum_rows(dt * (-jnp.exp(alog_ref[...])))
    acum_t = acum.T
    as_t = acum_t - jnp.log(dt).T
    li =lax.broadcasted_iota(jnp.int32, (CHUNK, CHUNK), 0)
    si = lax.broadcasted_iota(jnp.int32, (CHUNK, CHUNK), 1)
    causal = li >= si
    first_half = si < HEAD_DIM
    first_half_n = lax.broadcasted_iota(jnp.int32, (D_STATE, LANES), 1) < HEAD_DIM

    for g in range(N_GROUPS):
        gs = slice(g * D_STATE, (g + 1) * D_STATE)
        c_g = cm[:, gs]
        cb = lax.dot_general(c_g.astype(BF16), bm[:, gs].astype(BF16), (((1,), (1,)), ((), ())),
                             preferred_element_type=F32)
        bt_g = bm_t[gs, :]
        for q in range(HEADS_PER_GROUP // 2):
            pair = (g * HEADS_PER_GROUP) // 2 + q
            ps = slice(pair * LANES, (pair + 1) * LANES)
            x_pair = xs[:, ps]
            s_pair = state_ref[pair]
            rhs = jnp.concatenate([x_pair, s_pair.astype(BF16), jnp.zeros((D_STATE, LANES), BF16)], axis=0)
            res, lhs_s, cds = [], [], []
            for h in (2 * pair, 2 * pair + 1):
                a_l = jnp.broadcast_to(acum[:, h:h + 1], (CHUNK, CHUNK))
                a_s = as_t[h:h + 1, :]
                mix = cb * jnp.exp(jnp.where(causal, a_l - a_s, -jnp.inf))
                cdec = c_g * jnp.exp(a_l[:, :D_STATE])
                lhs = jnp.concatenate([mix.astype(BF16), cdec.astype(BF16),
                                       jnp.zeros((CHUNK, D_STATE), BF16)], axis=1)
                res.append(jnp.dot(lhs, rhs, preferred_element_type=F32))
                a_last = acum_t[h:h + 1, CHUNK - 1:CHUNK]
                lhs_s.append(bt_g * jnp.exp(a_last - a_s))
                cds.append(jnp.exp(a_last))
            yacc_ref[:, ps] = jnp.where(first_half, res[0], res[1])
            upd = jnp.dot(jnp.concatenate(lhs_s, axis=0).astype(BF16), x_pair, preferred_element_type=F32)
            state_ref[pair] = jnp.where(first_half_n, cds[0] * s_pair + upd[:D_STATE],
                                        cds[1] * s_pair + upd[D_STATE:])

    y = _gate_and_norm(yacc_ref[...], xs.astype(F32), gate_ref[...], dskip_ref[...], nw_ref[...])
    y_ref[...] = y.astype(BF16)

    @pl.when(c == pl.num_programs(1) - 1)
    def _():
        for pair in range(HEAD_PAIRS):
            t = jnp.concatenate([state_ref[pair], jnp.zeros((LANES - D_STATE, LANES), F32)], axis=0).T
            hfin_ref[0, pair * LANES:(pair + 1) * LANES, :] = t[:, :D_STATE]


def _ssd_prompt(xs, bc, dt, gate, alog, dskip, nw, nseq, seq):
    nc = seq // CHUNK
    row = lambda b, c: (b * nc + c, 0)
    return pl.pallas_call(
        _ssd_prompt_kernel,
        grid=(nseq, nc),
        in_specs=[pl.BlockSpec((CHUNK, SSD_WIDTH), row),
                  pl.BlockSpec((CHUNK, 2 * BC_WIDTH), row),
                  pl.BlockSpec((CHUNK, DT_PAD), row),
                  pl.BlockSpec((CHUNK, SSD_WIDTH), row),
                  _const_spec((1, DT_PAD)),
                  _const_spec((1, SSD_WIDTH)),
                  _const_spec((1, SSD_WIDTH))],
        out_specs=[pl.BlockSpec((CHUNK, SSD_WIDTH), row),
                   pl.BlockSpec((1, SSD_WIDTH, D_STATE), lambda b, c: (b, 0, 0))],
        out_shape=[jax.ShapeDtypeStruct((nseq * seq, SSD_WIDTH), BF16),
                   jax.ShapeDtypeStruct((nseq, SSD_WIDTH, D_STATE), F32)],
        scratch_shapes=[pltpu.VMEM((HEAD_PAIRS, D_STATE, LANES), F32),
                        pltpu.VMEM((CHUNK, SSD_WIDTH), F32)],
        compiler_params=pltpu.CompilerParams(dimension_semantics=("arbitrary", "arbitrary"),
                                             vmem_limit_bytes=VMEM_LIMIT),
        name="ssd_prompt",
    )(xs, bc, dt, gate, alog, dskip, nw)


def _step_mix_kernel(xbc_ref, z_ref, dt_ref, v_ref, cs_ref, ps_ref, h_ref,
                     cw_ref, cb_ref, dtb_ref, alog_ref, dskip_ref, nw_ref, pw_ref, psc_ref,
                     hout_ref, y_ref, yp_ref, cso_ref, pso_ref,
                     xs_ref, xdt_ref, bt_ref, ct_ref, dect_ref, yt_ref, *, pos0):
    h = pl.program_id(0)
    nprev = POOL_MAX - 1

    @pl.when(h == 0)
    def _prepare():
        xnew = xbc_ref[...]
        conv = cb_ref[...] + xnew * cw_ref[CONV_W - 1:CONV_W, :]
        for k in range(CONV_W - 1):
            conv = conv + cs_ref[k] * cw_ref[k:k + 1, :]
        xbc = _silu(conv)
        xs = xbc[:, :SSD_WIDTH]
        xs_ref[...] = xs
        dt = _softplus(dt_ref[...] + dtb_ref[...])
        dect_ref[...] = jnp.exp(dt * (-jnp.exp(alog_ref[...]))).T
        dt_t = dt.T
        for hh in range(N_HEADS):
            blk = xs[:, (hh // 2) * LANES:(hh // 2 + 1) * LANES].T
            half = blk[(hh % 2) * HEAD_DIM:(hh % 2 + 1) * HEAD_DIM, :]
            xdt_ref[hh * HEAD_DIM:(hh + 1) * HEAD_DIM, :] = half * dt_t[hh:hh + 1, :]
        for i in range(BC_WIDTH // LANES):
            sl = slice(i * LANES, (i + 1) * LANES)
            bt_ref[sl, :] = xbc[:, SSD_WIDTH + i * LANES:SSD_WIDTH + (i + 1) * LANES].T
            ct_ref[sl, :] = xbc[:, SSD_WIDTH + BC_WIDTH + i * LANES:SSD_WIDTH + BC_WIDTH + (i + 1) * LANES].T
        for k in range(CONV_W - 2):
            cso_ref[k] = cs_ref[k + 1]
        cso_ref[CONV_W - 2] = xnew
        for k in range(nprev - 1):
            pso_ref[k] = ps_ref[k + 1]
        pso_ref[nprev - 1] = v_ref[...]

    g = lax.div(h, HEADS_PER_GROUP)
    hrow = pl.multiple_of(h * HEAD_DIM, HEAD_DIM)
    grow = pl.multiple_of(g * D_STATE, D_STATE)
    b_t = bt_ref[pl.ds(grow, D_STATE), :]
    c_t = ct_ref[pl.ds(grow, D_STATE), :]
    dec = dect_ref[pl.ds(h, 1), :]

    def per_p(p, carry):
        xrow = xdt_ref[pl.ds(hrow + p, 1), :]
        hn = h_ref[0, p] * dec + xrow * b_t
        hout_ref[0, p] = hn
        yt_ref[pl.ds(hrow + p, 1), :] = jnp.sum(hn * c_t, axis=0, keepdims=True)
        return carry

    lax.fori_loop(0, HEAD_DIM, per_p, 0, unroll=4)

    @pl.when(h == pl.num_programs(0) - 1)
    def _finish():
        y0 = jnp.concatenate([yt_ref[i * LANES:(i + 1) * LANES, :].T for i in range(SSD_WIDTH // LANES)], axis=1)
        y = _gate_and_norm(y0, xs_ref[...], _silu(z_ref[...]), dskip_ref[...], nw_ref[...])
        y_ref[...] = y.astype(BF16)
        v = v_ref[...]
        sums, cnts = [], []
        for gi, w in enumerate(POOL_WINDOWS):
            sl = slice(gi * POOL_GROUP_DIM, (gi + 1) * POOL_GROUP_DIM)
            acc = v[:, sl]
            for j in range(1, w):
                acc = acc + ps_ref[nprev - j][:, sl]
            sums.append(acc)
            cnts.append(float(min(pos0 + 1, w)))
        yp_ref[...] = _pool_project(sums, v, cnts, pw_ref, psc_ref[...]).astype(BF16)


def _step_mix(xbc, z, dt, v, cs_t, ps_t, h_t, cw, cb, dtb, alog, dskip, nw, pw, psc, pos0):
    n = xbc.shape[0]
    nprev = POOL_MAX - 1
    full = lambda shape: pl.BlockSpec(shape, lambda i: (0,) * len(shape))
    hblk = pl.BlockSpec((1, HEAD_DIM, D_STATE, n), lambda i: (i, 0, 0, 0))
    return pl.pallas_call(
        functools.partial(_step_mix_kernel, pos0=pos0),
        grid=(N_HEADS,),
        in_specs=[full((n, CONV_DIM)), full((n, SSD_WIDTH)), full((n, DT_PAD)), full((n, POOL_WIDTH)),
                  full((CONV_W - 1, n, CONV_DIM)), full((nprev, n, POOL_WIDTH)), hblk,
                  full((CONV_W, CONV_DIM)), full((1, CONV_DIM)), full((1, DT_PAD)), full((1, DT_PAD)),
                  full((1, SSD_WIDTH)), full((1, SSD_WIDTH)),
                  full((len(POOL_WINDOWS), POOL_GROUP_DIM, POOL_GROUP_DIM)), full((1, POOL_WIDTH))],
        out_specs=[hblk, full((n, SSD_WIDTH)), full((n, POOL_WIDTH)),
                   full((CONV_W - 1, n, CONV_DIM)), full((nprev, n, POOL_WIDTH))],
        out_shape=[jax.ShapeDtypeStruct(h_t.shape, F32),
                   jax.ShapeDtypeStruct((n, SSD_WIDTH), BF16),
                   jax.ShapeDtypeStruct((n, POOL_WIDTH), BF16),
                   jax.ShapeDtypeStruct((CONV_W - 1, n, CONV_DIM), F32),
                   jax.ShapeDtypeStruct((nprev, n, POOL_WIDTH), F32)],
        scratch_shapes=[pltpu.VMEM((n, SSD_WIDTH), F32),
                        pltpu.VMEM((SSD_WIDTH, n), F32),
                        pltpu.VMEM((BC_WIDTH, n), F32),
                        pltpu.VMEM((BC_WIDTH, n), F32),
                        pltpu.VMEM((DT_PAD, n), F32),
                        pltpu.VMEM((SSD_WIDTH, n), F32)],
        compiler_params=pltpu.CompilerParams(dimension_semantics=("arbitrary",),
                                             vmem_limit_bytes=VMEM_LIMIT),
        name="step_mix",
    )(xbc, z, dt, v, cs_t, ps_t, h_t, cw, cb, dtb, alog, dskip, nw, pw, psc)


FFN_CHUNK = 256


def _outffn_kernel(*refs, tm, decode):
    if decode:
        (x_ref, ys_ref, yp_ref, st_ref, wout_ref, n2_ref, wup_ref, fcw_ref, fcb_ref, wdn_ref, fn_ref,
         y_ref, sto_ref) = refs
    else:
        (x_ref, ys_ref, yp_ref, wout_ref, n2_ref, wup_ref, fcw_ref, fcb_ref, wdn_ref, fn_ref,
         y_ref, ust_ref, carry_ref) = refs
        j = pl.program_id(1)

        @pl.when(j == 0)
        def _():
            carry_ref[...] = jnp.zeros(carry_ref.shape, F32)

    ymix = jnp.concatenate([ys_ref[...], yp_ref[...]], axis=1)
    x1 = x_ref[...] + jnp.dot(ymix, wout_ref[...], preferred_element_type=F32)
    hn = _rms(x1, n2_ref[...]).astype(BF16)
    acc = x1
    nchunk = D_FF // FFN_CHUNK

    def col_slices(k):
        return [slice(base + k * FFN_CHUNK, base + (k + 1) * FFN_CHUNK) for base in (0, D_FF)]

    def up_proj(k):
        return [jnp.dot(hn, wup_ref[:, cs], preferred_element_type=F32) for cs in col_slices(k)]

    u_next = up_proj(0)
    for k in range(nchunk):
        u_cur = u_next
        if k + 1 < nchunk:
            u_next = up_proj(k + 1)
        halves = []
        for u, cs in zip(u_cur, col_slices(k)):
            if decode:
                prev2, prev1 = st_ref[:, 0, cs], st_ref[:, 1, cs]
                sto_ref[:, 0, cs] = prev1
                sto_ref[:, 1, cs] = u
            else:
                prev8 = carry_ref[:, cs]
                prev1 = _delay_rows(u, prev8, 1)
                prev2 = _delay_rows(u, prev8, 2)
                carry_ref[:, cs] = u[tm - SUBLANES:tm]
            halves.append(fcb_ref[:, cs] + prev2 * fcw_ref[0:1, cs] + prev1 * fcw_ref[1:2, cs]
                          + u * fcw_ref[2:3, cs])
        act = (_silu(halves[0]) * halves[1]).astype(BF16)
        acc = acc + jnp.dot(act, wdn_ref[k * FFN_CHUNK:(k + 1) * FFN_CHUNK, :], preferred_element_type=F32)
    y_ref[...] = _rms(acc, fn_ref[...])
    if not decode:
        @pl.when(j == pl.num_programs(1) - 1)
        def _():
            ust_ref[0] = carry_ref[SUBLANES - (FFN_CONV_W - 1):SUBLANES, :]


def _outffn_prompt(x2d, ys, yp, wout, n2, wup, fcw, fcb, wdn, fn, nseq, seq, tm):
    nj = seq // tm
    row = lambda b, j: (b * nj + j, 0)
    return pl.pallas_call(
        functools.partial(_outffn_kernel, tm=tm, decode=False),
        grid=(nseq, nj),
        in_specs=[pl.BlockSpec((tm, D_MODEL), row),
                  pl.BlockSpec((tm, SSD_WIDTH), row),
                  pl.BlockSpec((tm, POOL_WIDTH), row),
                  _const_spec((D_MIX, D_MODEL)),
                  _const_spec((1, D_MODEL)),
                  _const_spec((D_MODEL, 2 * D_FF)),
                  _const_spec((FFN_CONV_W, 2 * D_FF)),
                  _const_spec((1, 2 * D_FF)),
                  _const_spec((D_FF, D_MODEL)),
                  _const_spec((1, D_MODEL))],
        out_specs=[pl.BlockSpec((tm, D_MODEL), row),
                   pl.BlockSpec((1, FFN_CONV_W - 1, 2 * D_FF), lambda b, j: (b, 0, 0))],
        out_shape=[jax.ShapeDtypeStruct((nseq * seq, D_MODEL), F32),
                   jax.ShapeDtypeStruct((nseq, FFN_CONV_W - 1, 2 * D_FF), F32)],
        scratch_shapes=[pltpu.VMEM((SUBLANES, 2 * D_FF), F32)],
        compiler_params=pltpu.CompilerParams(dimension_semantics=("arbitrary", "arbitrary"),
                                             vmem_limit_bytes=VMEM_LIMIT),
        name="outffn_prompt",
    )(x2d, ys, yp, wout, n2, wup, fcw, fcb, wdn, fn)


def _outffn_decode(x2d, ys, yp, ffn_state, wout, n2, wup, fcw, fcb, wdn, fn):
    n = x2d.shape[0]
    full = lambda shape: pl.BlockSpec(shape, lambda i: (0,) * len(shape))
    return pl.pallas_call(
        functools.partial(_outffn_kernel, tm=n, decode=True),
        grid=(1,),
        in_specs=[full((n, D_MODEL)), full((n, SSD_WIDTH)), full((n, POOL_WIDTH)), full(ffn_state.shape),
                  _const_spec((D_MIX, D_MODEL)),
                  _const_spec((1, D_MODEL)),
                  _const_spec((D_MODEL, 2 * D_FF)),
                  _const_spec((FFN_CONV_W, 2 * D_FF)),
                  _const_spec((1, 2 * D_FF)),
                  _const_spec((D_FF, D_MODEL)),
                  _const_spec((1, D_MODEL))],
        out_specs=[full((n, D_MODEL)), full(ffn_state.shape)],
        out_shape=[jax.ShapeDtypeStruct((n, D_MODEL), F32),
                   jax.ShapeDtypeStruct(ffn_state.shape, F32)],
        compiler_params=pltpu.CompilerParams(dimension_semantics=("arbitrary",),
                                             vmem_limit_bytes=VMEM_LIMIT),
        name="outffn_decode",
    )(x2d, ys, yp, ffn_state, wout, n2, wup, fcw, fcb, wdn, fn)


def kernel(x_prompt, x_sample, state_ssm, state_conv, state_pool, state_ffn_conv, norm1_w, w_in, conv_w, conv_b,
           dt_bias, a_log, d_skip, ssd_norm_w, pool_w, pool_scale, w_out, norm2_w, w_up, ffn_conv_w, ffn_conv_b,
           w_down, final_norm_w):
    assert w_in.shape[0] == 1, "single-layer model"
    nseq, seq, _ = x_prompt.shape
    nsmp, one, _ = x_sample.shape
    assert one == 1 and seq % CHUNK == 0 and nsmp == LANES

    wi = w_in[0]
    w_dt = jnp.pad(wi[:, SSD_WIDTH + CONV_DIM:SSD_WIDTH + CONV_DIM + N_HEADS], ((0, 0), (0, DT_PAD - N_HEADS)))
    w_cat = jnp.concatenate([wi[:, :SSD_WIDTH], wi[:, SSD_WIDTH:SSD_WIDTH + CONV_DIM],
                             wi[:, SSD_WIDTH + CONV_DIM + N_HEADS:], w_dt], axis=1).astype(BF16)
    n1 = norm1_w[0][None]
    cw, cb = conv_w[0], conv_b[0][None]
    dtb = jnp.pad(dt_bias[0], (0, DT_PAD - N_HEADS))[None]
    alog = jnp.pad(a_log[0], (0, DT_PAD - N_HEADS))[None]
    dskip = jnp.repeat(d_skip[0], HEAD_DIM)[None]
    nw = ssd_norm_w[0][None]
    pw = pool_w[0].astype(BF16)
    psc = pool_scale[0][None]
    wout = w_out[0].astype(BF16)
    n2 = norm2_w[0][None]
    wup = w_up[0].astype(BF16)
    fcw, fcb = ffn_conv_w[0], ffn_conv_b[0][None]
    wdn = w_down[0].astype(BF16)
    fn = final_norm_w[None]

    xp = x_prompt.reshape(nseq * seq, D_MODEL)
    xs_p, bc_p, dt_p, gate_p, yp_p, ctail, vtail = _inproj_prompt(xp, n1, w_cat, cw, cb, dtb, pw, psc,
                                                                  nseq, seq, tm=512)
    ys_p, hfin = _ssd_prompt(xs_p, bc_p, dt_p, gate_p, alog, dskip, nw, nseq, seq)
    y_p, ffn_p = _outffn_prompt(xp, ys_p, yp_p, wout, n2, wup, fcw, fcb, wdn, fn, nseq, seq, tm=256)

    xs_in = x_sample.reshape(nsmp, D_MODEL)
    z_s, xbc_s, v_s, dt_s = _inproj(xs_in, n1, w_cat, tm=nsmp)
    h_t = jnp.transpose(state_ssm[0], (1, 2, 3, 0))
    cs_t = jnp.transpose(state_conv[0], (1, 0, 2))
    ps_t = jnp.transpose(state_pool[0], (1, 0, 2))
    hnew_t, ys_s, yp_s, cs_new, ps_new = _step_mix(xbc_s, z_s, dt_s, v_s, cs_t, ps_t, h_t, cw, cb, dtb, alog,
                                                   dskip, nw, pw, psc, PAST_LEN)
    y_s, ffn_new = _outffn_decode(xs_in, ys_s, yp_s, state_ffn_conv[0], wout, n2, wup, fcw, fcb, wdn, fn)

    return (y_p.reshape(nseq, seq, D_MODEL),
            y_s.reshape(nsmp, 1, D_MODEL),
            hfin.reshape(nseq, N_HEADS, HEAD_DIM, D_STATE)[None],
            jnp.transpose(hnew_t, (3, 0, 1, 2))[None],
            ctail[:, SUBLANES - (CONV_W - 1):][None],
            jnp.transpose(cs_new, (1, 0, 2))[None],
            vtail[:, 2 * SUBLANES - (POOL_MAX - 1):][None],
            jnp.transpose(ps_new, (1, 0, 2))[None],
            ffn_p[None],
            ffn_new[None])
```

```python
import functools

import jax
import jax.numpy as jnp
from jax import lax
from jax.experimental import pallas as pl
from jax.experimental.pallas import tpu as pltpu

F32 = jnp.float32
BF16 = jnp.bfloat16

D_MODEL = 1024
SSD_WIDTH = 1536
HEAD_DIM = 64
N_HEADS = SSD_WIDTH // HEAD_DIM
N_GROUPS = 4
HEADS_PER_GROUP = N_HEADS // N_GROUPS
D_STATE = 64
BC_WIDTH = N_GROUPS * D_STATE
CONV_DIM = SSD_WIDTH + 2 * BC_WIDTH
CONV_W = 4
CHUNK = 128
POOL_WIDTH = 512
POOL_WINDOWS = (2, 4, 8, 16)
POOL_GROUP_DIM = POOL_WIDTH // len(POOL_WINDOWS)
POOL_MAX = max(POOL_WINDOWS)
D_MIX = SSD_WIDTH + POOL_WIDTH
D_FF = 2816
FFN_CONV_W = 3
NORM_GROUP = SSD_WIDTH // N_GROUPS
EPS = 1e-6
LOG2_E = 1.4426950408889634
PAST_LEN = 16384

LANES = 128
SUBLANES = 8
DT_PAD = LANES
VMEM_LIMIT = 56 * 1024 * 1024
HEAD_PAIRS = N_HEADS // 2

_Z0, _Z1 = 0, SSD_WIDTH
_X0, _X1 = _Z1, _Z1 + CONV_DIM
_V0, _V1 = _X1, _X1 + POOL_WIDTH
_T0, _T1 = _V1, _V1 + DT_PAD


def _rms(x, w):
    ms = jnp.mean(x * x, axis=-1, keepdims=True)
    return x * lax.rsqrt(ms + EPS) * w


def _silu(x):
    h = 0.5 * x
    return h + h * jnp.tanh(h)


def _softplus(x):
    return jnp.maximum(x, 0.0) + jnp.log1p(jnp.exp(-jnp.abs(x)))


def _const_spec(shape):
    nd = len(shape)
    return pl.BlockSpec(shape, lambda *_: (0,) * nd, pipeline_mode=pl.Buffered(1))


def _delay_rows(u, prev8, s):
    if s == SUBLANES:
        return jnp.concatenate([prev8, u[:u.shape[0] - SUBLANES]], axis=0)
    r = pltpu.roll(u, s, axis=0)
    c = pltpu.roll(prev8, s, axis=0)
    row = lax.broadcasted_iota(jnp.int32, prev8.shape, 0)
    top = jnp.where(row < s, c, r[0:SUBLANES])
    return jnp.concatenate([top, r[SUBLANES:]], axis=0)


def _gate_and_norm(y, xs, gate, dskip, nw):
    y = (y + dskip * xs) * gate
    outs = []
    for g in range(N_GROUPS):
        sl = slice(g * NORM_GROUP, (g + 1) * NORM_GROUP)
        outs.append(_rms(y[:, sl], nw[:, sl]))
    return jnp.concatenate(outs, axis=1)


def _pool_project(win_sums, v, cnts, pw_ref, pscale):
    outs = []
    for g in range(len(POOL_WINDOWS)):
        sl = slice(g * POOL_GROUP_DIM, (g + 1) * POOL_GROUP_DIM)
        m = win_sums[g] * (1.0 / cnts[g]) - v[:, sl]
        outs.append(jnp.dot(m.astype(BF16), pw_ref[g], preferred_element_type=F32))
    return jnp.concatenate(outs, axis=1) * pscale


def _cumsum_rows(x):
    n = x.shape[0]
    row = lax.broadcasted_iota(jnp.int32, x.shape, 0)
    k = 1
    while k < n:
        x = x + jnp.where(row >= k, pltpu.roll(x, k, axis=0), 0.0)
        k *= 2
    return x


def _inproj_kernel(*refs, tm, prompt, pos0):
    if prompt:
        (x_ref, nw_ref, w_ref, cw_ref, cb_ref, dtb_ref, pw_ref, psc_ref,
         xs_ref, bc_ref, dt_ref, gate_ref, yp_ref, ctail_ref, vtail_ref,
         cc_ref, vc_ref, s2c_ref, s4c_ref, s8c_ref) = refs
    else:
        x_ref, nw_ref, w_ref, z_ref, xbc_ref, v_ref, dt_ref = refs
    hn = _rms(x_ref[...], nw_ref[...]).astype(BF16)

    def mm(lo, hi):
        return jnp.dot(hn, w_ref[:, lo:hi], preferred_element_type=F32)

    if not prompt:
        z_ref[...] = mm(_Z0, _Z1)
        xbc_ref[...] = mm(_X0, _X1)
        v_ref[...] = mm(_V0, _V1)
        dt_ref[...] = mm(_T0, _T1)
        return

    j = pl.program_id(1)
    carries = (cc_ref, vc_ref, s2c_ref, s4c_ref, s8c_ref)

    @pl.when(j == 0)
    def _():
        for r in carries:
            r[...] = jnp.zeros(r.shape, F32)

    last = j == pl.num_programs(1) - 1

    def conv_stage(lo, hi):
        def fn(xbc):
            prev8 = cc_ref[:, lo:hi]
            conv = cb_ref[:, lo:hi] + xbc * cw_ref[CONV_W - 1:CONV_W, lo:hi]
            for k in range(CONV_W - 1):
                conv = conv + _delay_rows(xbc, prev8, CONV_W - 1 - k) * cw_ref[k:k + 1, lo:hi]
            cc_ref[:, lo:hi] = xbc[tm - SUBLANES:tm]
            act = _silu(conv)
            if hi <= SSD_WIDTH:
                xs_ref[:, lo:hi] = act.astype(BF16)
            else:
                bc_ref[:, lo - SSD_WIDTH:hi - SSD_WIDTH] = act
        return fn

    def pool_stage(v):
        s2 = v + _delay_rows(v, vc_ref[...], 1)
        s4 = s2 + _delay_rows(s2, s2c_ref[...], 2)
        s8 = s4 + _delay_rows(s4, s4c_ref[...], 4)
        s16 = s8 + _delay_rows(s8, s8c_ref[...], 8)
        for r, val in zip(carries[1:], (v, s2, s4, s8)):
            r[...] = val[tm - SUBLANES:tm]
        pos = pos0 + j * tm + lax.broadcasted_iota(jnp.int32, (tm, 1), 0)
        sums, cnts = [], []
        for g, (w, s) in enumerate(zip(POOL_WINDOWS, (s2, s4, s8, s16))):
            sums.append(s[:, g * POOL_GROUP_DIM:(g + 1) * POOL_GROUP_DIM])
            cnts.append(jnp.minimum(pos + 1, w).astype(F32))
        yp_ref[...] = _pool_project(sums, v, cnts, pw_ref, psc_ref[...]).astype(BF16)

        @pl.when(last)
        def _():
            vtail_ref[0] = v[tm - 2 * SUBLANES:tm]

    def gate_stage(lo, hi):
        def fn(z):
            gate_ref[:, lo:hi] = _silu(z)
        return fn

    def dt_stage(dt_raw):
        dt_ref[...] = _softplus(dt_raw + dtb_ref[...])

    stages = [((_Z0, _Z1), gate_stage(0, SSD_WIDTH)),
              ((_T0, _T1), dt_stage),
              ((_X0, _X0 + SSD_WIDTH), conv_stage(0, SSD_WIDTH)),
              ((_X0 + SSD_WIDTH, _X1), conv_stage(SSD_WIDTH, CONV_DIM)),
              ((_V0, _V1), pool_stage)]
    for cols, fn in stages:
        fn(mm(*cols))

    @pl.when(last)
    def _():
        ctail_ref[0] = cc_ref[...]


def _inproj(x2d, norm_w, w_cat, tm):
    m = x2d.shape[0]
    row = lambda i: (i, 0)
    return pl.pallas_call(
        functools.partial(_inproj_kernel, tm=tm, prompt=False, pos0=0),
        grid=(m // tm,),
        in_specs=[pl.BlockSpec((tm, D_MODEL), row),
                  _const_spec((1, D_MODEL)),
                  _const_spec((D_MODEL, _T1))],
        out_specs=[pl.BlockSpec((tm, SSD_WIDTH), row),
                   pl.BlockSpec((tm, CONV_DIM), row),
                   pl.BlockSpec((tm, POOL_WIDTH), row),
                   pl.BlockSpec((tm, DT_PAD), row)],
        out_shape=[jax.ShapeDtypeStruct((m, SSD_WIDTH), F32),
                   jax.ShapeDtypeStruct((m, CONV_DIM), F32),
                   jax.ShapeDtypeStruct((m, POOL_WIDTH), F32),
                   jax.ShapeDtypeStruct((m, DT_PAD), F32)],
        compiler_params=pltpu.CompilerParams(dimension_semantics=("arbitrary",),
                                             vmem_limit_bytes=VMEM_LIMIT),
        name="inproj",
    )(x2d, norm_w, w_cat)


def _inproj_prompt(x2d, norm_w, w_cat, cw, cb, dtb, pw, psc, nseq, seq, tm):
    nj = seq // tm
    m = nseq * seq
    row = lambda b, j: (b * nj + j, 0)
    per_seq = lambda b, j: (b, 0, 0)
    return pl.pallas_call(
        functools.partial(_inproj_kernel, tm=tm, prompt=True, pos0=0),
        grid=(nseq, nj),
        in_specs=[pl.BlockSpec((tm, D_MODEL), row),
                  _const_spec((1, D_MODEL)),
                  _const_spec((D_MODEL, _T1)),
                  _const_spec((CONV_W, CONV_DIM)),
                  _const_spec((1, CONV_DIM)),
                  _const_spec((1, DT_PAD)),
                  _const_spec((len(POOL_WINDOWS), POOL_GROUP_DIM, POOL_GROUP_DIM)),
                  _const_spec((1, POOL_WIDTH))],
        out_specs=[pl.BlockSpec((tm, SSD_WIDTH), row),
                   pl.BlockSpec((tm, 2 * BC_WIDTH), row),
                   pl.BlockSpec((tm, DT_PAD), row),
                   pl.BlockSpec((tm, SSD_WIDTH), row),
                   pl.BlockSpec((tm, POOL_WIDTH), row),
                   pl.BlockSpec((1, SUBLANES, CONV_DIM), per_seq),
                   pl.BlockSpec((1, 2 * SUBLANES, POOL_WIDTH), per_seq)],
        out_shape=[jax.ShapeDtypeStruct((m, SSD_WIDTH), BF16),
                   jax.ShapeDtypeStruct((m, 2 * BC_WIDTH), F32),
                   jax.ShapeDtypeStruct((m, DT_PAD), F32),
                   jax.ShapeDtypeStruct((m, SSD_WIDTH), F32),
                   jax.ShapeDtypeStruct((m, POOL_WIDTH), BF16),
                   jax.ShapeDtypeStruct((nseq, SUBLANES, CONV_DIM), F32),
                   jax.ShapeDtypeStruct((nseq, 2 * SUBLANES, POOL_WIDTH), F32)],
        scratch_shapes=[pltpu.VMEM((SUBLANES, CONV_DIM), F32),
                        pltpu.VMEM((SUBLANES, POOL_WIDTH), F32),
                        pltpu.VMEM((SUBLANES, POOL_WIDTH), F32),
                        pltpu.VMEM((SUBLANES, POOL_WIDTH), F32),
                        pltpu.VMEM((SUBLANES, POOL_WIDTH), F32)],
        compiler_params=pltpu.CompilerParams(dimension_semantics=("arbitrary", "arbitrary"),
                                             vmem_limit_bytes=VMEM_LIMIT),
        name="inproj_prompt",
    )(x2d, norm_w, w_cat, cw, cb, dtb, pw, psc)


SSD_SEQS = 1


def _ssd_chunk(q, xs_ref, bc_ref, dt_ref, gate_ref, neg_a, dskip, nw, y_ref, state_ref):
    xs = xs_ref[q]
    bm = bc_ref[q, :, :BC_WIDTH]
    cm = bc_ref[q, :, BC_WIDTH:]
    bm_t = bm.T

    dt = dt_ref[q]
    acum = _cumsum_rows(dt * neg_a) * LOG2_E
    acum_t = acum.T
    as_t = acum_t - jnp.log2(dt).T
    li = lax.broadcasted_iota(jnp.int32, (CHUNK, CHUNK), 0)
    si = lax.broadcasted_iota(jnp.int32, (CHUNK, CHUNK), 1)
    causal = li >= si
    first_half = si < HEAD_DIM
    first_half_n = lax.broadcasted_iota(jnp.int32, (D_STATE, LANES), 1) < HEAD_DIM
    first_half_row = lax.broadcasted_iota(jnp.int32, (1, LANES), 1) < HEAD_DIM
    zeros_n = jnp.zeros((D_STATE, LANES), BF16)
    updates = []

    for g in range(N_GROUPS):
        gs = slice(g * D_STATE, (g + 1) * D_STATE)
        c_g = cm[:, gs]
        c2 = jnp.concatenate([c_g, c_g], axis=1)
        cb = lax.dot_general(c_g.astype(BF16), bm[:, gs].astype(BF16), (((1,), (1,)), ((), ())),
                             preferred_element_type=F32)
        bt_g = bm_t[gs, :]
        y_pairs = []
        for k in range(HEADS_PER_GROUP // 2):
            pair = (g * HEADS_PER_GROUP) // 2 + k
            x_pair = xs[:, pair * LANES:(pair + 1) * LANES]
            s_pair = state_ref[q, pair]
            s_b = s_pair.astype(BF16)
            heads = (2 * pair, 2 * pair + 1)
            a_l = [jnp.broadcast_to(acum[:, h:h + 1], (CHUNK, CHUNK)) for h in heads]
            cdec = (c2 * jnp.exp2(jnp.where(first_half, a_l[0], a_l[1]))).astype(BF16)
            rhs = [jnp.concatenate([x_pair, s_b, zeros_n], axis=0),
                   jnp.concatenate([x_pair, zeros_n, s_b], axis=0)]
            res, lhs_s, cds = [], [], []
            for i, h in enumerate(heads):
                a_s = as_t[h:h + 1, :]
                mix = cb * jnp.exp2(jnp.where(causal, a_l[i] - a_s, -jnp.inf))
                lhs = jnp.concatenate([mix.astype(BF16), cdec], axis=1)
                res.append(jnp.dot(lhs, rhs[i], preferred_element_type=F32))
                a_last = acum[CHUNK - 1:CHUNK, h:h + 1]
                lhs_s.append(bt_g * jnp.exp2(a_last - a_s))
                cds.append(jnp.exp2(a_last))
            y_pairs.append(jnp.where(first_half, res[0], res[1]))
            updates.append((pair, jnp.concatenate(lhs_s, axis=0).astype(BF16), x_pair,
                            jnp.where(first_half_row, cds[0], cds[1]) * s_pair))

        ns = slice(g * NORM_GROUP, (g + 1) * NORM_GROUP)
        y_g = jnp.concatenate(y_pairs, axis=1)
        y_g = (y_g + dskip[:, ns] * xs[:, ns].astype(F32)) * gate_ref[q, :, ns]
        y_ref[q, :, ns] = _rms(y_g, nw[:, ns]).astype(BF16)

    for pair, lhs_s, x_pair, decayed in updates:
        upd = jnp.dot(lhs_s, x_pair, preferred_element_type=F32)
        state_ref[q, pair] = decayed + jnp.where(first_half_n, upd[:D_STATE], upd[D_STATE:])


def _ssd_prompt_kernel(xs_ref, bc_ref, dt_ref, gate_ref, alog_ref, dskip_ref, nw_ref,
                       y_ref, hfin_ref, state_ref):
    c = pl.program_id(1)

    @pl.when(c == 0)
    def _():
        state_ref[...] = jnp.zeros(state_ref.shape, F32)

    neg_a = -jnp.exp(alog_ref[...])
    for q in range(SSD_SEQS):
        _ssd_chunk(q, xs_ref, bc_ref, dt_ref, gate_ref, neg_a, dskip_ref[...], nw_ref[...],
                   y_ref, state_ref)

    @pl.when(c == pl.num_programs(1) - 1)
    def _():
        for q in range(SSD_SEQS):
            for pair in range(HEAD_PAIRS):
                t = jnp.concatenate([state_ref[q, pair], jnp.zeros((LANES - D_STATE, LANES), F32)], axis=0).T
                hfin_ref[q, pair * LANES:(pair + 1) * LANES, :] = t[:, :D_STATE]


def _ssd_prompt(xs, bc, dt, gate, alog, dskip, nw, nseq, seq):
    nc = seq // CHUNK
    blk = lambda width: pl.BlockSpec((SSD_SEQS, CHUNK, width), lambda b, c: (b, c, 0))
    return pl.pallas_call(
        _ssd_prompt_kernel,
        grid=(nseq // SSD_SEQS, nc),
        in_specs=[blk(SSD_WIDTH), blk(2 * BC_WIDTH), blk(DT_PAD), blk(SSD_WIDTH),
                  _const_spec((1, DT_PAD)),
                  _const_spec((1, SSD_WIDTH)),
                  _const_spec((1, SSD_WIDTH))],
        out_specs=[blk(SSD_WIDTH),
                   pl.BlockSpec((SSD_SEQS, SSD_WIDTH, D_STATE), lambda b, c: (b, 0, 0))],
        out_shape=[jax.ShapeDtypeStruct((nseq, seq, SSD_WIDTH), BF16),
                   jax.ShapeDtypeStruct((nseq, SSD_WIDTH, D_STATE), F32)],
        scratch_shapes=[pltpu.VMEM((SSD_SEQS, HEAD_PAIRS, D_STATE, LANES), F32)],
        compiler_params=pltpu.CompilerParams(dimension_semantics=("arbitrary", "arbitrary"),
                                             vmem_limit_bytes=VMEM_LIMIT),
        name="ssd_prompt",
    )(xs, bc, dt, gate, alog, dskip, nw)


def _step_mix_kernel(xbc_ref, z_ref, dt_ref, v_ref, cs_ref, ps_ref, h_ref,
                     cw_ref, cb_ref, dtb_ref, alog_ref, dskip_ref, nw_ref, pw_ref, psc_ref,
                     hout_ref, y_ref, yp_ref, cso_ref, pso_ref,
                     xs_ref, xdt_ref, bt_ref, ct_ref, dect_ref, yt_ref, *, pos0):
    h = pl.program_id(0)
    nprev = POOL_MAX - 1

    @pl.when(h == 0)
    def _prepare():
        xnew = xbc_ref[...]
        conv = cb_ref[...] + xnew * cw_ref[CONV_W - 1:CONV_W, :]
        for k in range(CONV_W - 1):
            conv = conv + cs_ref[k] * cw_ref[k:k + 1, :]
        xbc = _silu(conv)
        xs = xbc[:, :SSD_WIDTH]
        xs_ref[...] = xs
        dt = _softplus(dt_ref[...] + dtb_ref[...])
        dect_ref[...] = jnp.exp(dt * (-jnp.exp(alog_ref[...]))).T
        dt_t = dt.T
        for hh in range(N_HEADS):
            blk = xs[:, (hh // 2) * LANES:(hh // 2 + 1) * LANES].T
            half = blk[(hh % 2) * HEAD_DIM:(hh % 2 + 1) * HEAD_DIM, :]
            xdt_ref[hh * HEAD_DIM:(hh + 1) * HEAD_DIM, :] = half * dt_t[hh:hh + 1, :]
        for i in range(BC_WIDTH // LANES):
            sl = slice(i * LANES, (i + 1) * LANES)
            bt_ref[sl, :] = xbc[:, SSD_WIDTH + i * LANES:SSD_WIDTH + (i + 1) * LANES].T
            ct_ref[sl, :] = xbc[:, SSD_WIDTH + BC_WIDTH + i * LANES:SSD_WIDTH + BC_WIDTH + (i + 1) * LANES].T
        for k in range(CONV_W - 2):
            cso_ref[k] = cs_ref[k + 1]
        cso_ref[CONV_W - 2] = xnew
        for k in range(nprev - 1):
            pso_ref[k] = ps_ref[k + 1]
        pso_ref[nprev - 1] = v_ref[...]

    g = lax.div(h, HEADS_PER_GROUP)
    hrow = pl.multiple_of(h * HEAD_DIM, HEAD_DIM)
    grow = pl.multiple_of(g * D_STATE, D_STATE)
    b_t = bt_ref[pl.ds(grow, D_STATE), :]
    c_t = ct_ref[pl.ds(grow, D_STATE), :]
    dec = dect_ref[pl.ds(h, 1), :]

    def per_p(p, carry):
        xrow = xdt_ref[pl.ds(hrow + p, 1), :]
        hn = h_ref[0, p] * dec + xrow * b_t
        hout_ref[0, p] = hn
        yt_ref[pl.ds(hrow + p, 1), :] = jnp.sum(hn * c_t, axis=0, keepdims=True)
        return carry

    lax.fori_loop(0, HEAD_DIM, per_p, 0, unroll=4)

    @pl.when(h == pl.num_programs(0) - 1)
    def _finish():
        y0 = jnp.concatenate([yt_ref[i * LANES:(i + 1) * LANES, :].T for i in range(SSD_WIDTH // LANES)], axis=1)
        y = _gate_and_norm(y0, xs_ref[...], _silu(z_ref[...]), dskip_ref[...], nw_ref[...])
        y_ref[...] = y.astype(BF16)
        v = v_ref[...]
        sums, cnts = [], []
        for gi, w in enumerate(POOL_WINDOWS):
            sl = slice(gi * POOL_GROUP_DIM, (gi + 1) * POOL_GROUP_DIM)
            acc = v[:, sl]
            for j in range(1, w):
                acc = acc + ps_ref[nprev - j][:, sl]
            sums.append(acc)
            cnts.append(float(min(pos0 + 1, w)))
        yp_ref[...] = _pool_project(sums, v, cnts, pw_ref, psc_ref[...]).astype(BF16)


def _step_mix(xbc, z, dt, v, cs_t, ps_t, h_t, cw, cb, dtb, alog, dskip, nw, pw, psc, pos0):
    n = xbc.shape[0]
    nprev = POOL_MAX - 1
    full = lambda shape: pl.BlockSpec(shape, lambda i: (0,) * len(shape))
    hblk = pl.BlockSpec((1, HEAD_DIM, D_STATE, n), lambda i: (i, 0, 0, 0))
    return pl.pallas_call(
        functools.partial(_step_mix_kernel, pos0=pos0),
        grid=(N_HEADS,),
        in_specs=[full((n, CONV_DIM)), full((n, SSD_WIDTH)), full((n, DT_PAD)), full((n, POOL_WIDTH)),
                  full((CONV_W - 1, n, CONV_DIM)), full((nprev, n, POOL_WIDTH)), hblk,
                  full((CONV_W, CONV_DIM)), full((1, CONV_DIM)), full((1, DT_PAD)), full((1, DT_PAD)),
                  full((1, SSD_WIDTH)), full((1, SSD_WIDTH)),
                  full((len(POOL_WINDOWS), POOL_GROUP_DIM, POOL_GROUP_DIM)), full((1, POOL_WIDTH))],
        out_specs=[hblk, full((n, SSD_WIDTH)), full((n, POOL_WIDTH)),
                   full((CONV_W - 1, n, CONV_DIM)), full((nprev, n, POOL_WIDTH))],
        out_shape=[jax.ShapeDtypeStruct(h_t.shape, F32),
                   jax.ShapeDtypeStruct((n, SSD_WIDTH), BF16),
                   jax.ShapeDtypeStruct((n, POOL_WIDTH), BF16),
                   jax.ShapeDtypeStruct((CONV_W - 1, n, CONV_DIM), F32),
                   jax.ShapeDtypeStruct((nprev, n, POOL_WIDTH), F32)],
        scratch_shapes=[pltpu.VMEM((n, SSD_WIDTH), F32),
                        pltpu.VMEM((SSD_WIDTH, n), F32),
                        pltpu.VMEM((BC_WIDTH, n), F32),
                        pltpu.VMEM((BC_WIDTH, n), F32),
                        pltpu.VMEM((DT_PAD, n), F32),
                        pltpu.VMEM((SSD_WIDTH, n), F32)],
        compiler_params=pltpu.CompilerParams(dimension_semantics=("arbitrary",),
                                             vmem_limit_bytes=VMEM_LIMIT),
        name="step_mix",
    )(xbc, z, dt, v, cs_t, ps_t, h_t, cw, cb, dtb, alog, dskip, nw, pw, psc)


FFN_CHUNK = 256


def _outffn_kernel(*refs, tm, decode):
    if decode:
        (x_ref, ys_ref, yp_ref, st_ref, wout_ref, n2_ref, wup_ref, fcw_ref, fcb_ref, wdn_ref, fn_ref,
         y_ref, sto_ref) = refs
    else:
        (x_ref, ys_ref, yp_ref, wout_ref, n2_ref, wup_ref, fcw_ref, fcb_ref, wdn_ref, fn_ref,
         y_ref, ust_ref, carry_ref) = refs
        j = pl.program_id(1)

        @pl.when(j == 0)
        def _():
            carry_ref[...] = jnp.zeros(carry_ref.shape, F32)

    ymix = jnp.concatenate([ys_ref[...], yp_ref[...]], axis=1)
    x1 = x_ref[...] + jnp.dot(ymix, wout_ref[...], preferred_element_type=F32)
    hn = _rms(x1, n2_ref[...]).astype(BF16)
    acc = x1
    nchunk = D_FF // FFN_CHUNK

    def col_slices(k):
        return [slice(base + k * FFN_CHUNK, base + (k + 1) * FFN_CHUNK) for base in (0, D_FF)]

    def up_proj(k):
        return [jnp.dot(hn, wup_ref[:, cs], preferred_element_type=F32) for cs in col_slices(k)]

    u_next = up_proj(0)
    for k in range(nchunk):
        u_cur = u_next
        if k + 1 < nchunk:
            u_next = up_proj(k + 1)
        halves = []
        for u, cs in zip(u_cur, col_slices(k)):
            if decode:
                prev2, prev1 = st_ref[:, 0, cs], st_ref[:, 1, cs]
                sto_ref[:, 0, cs] = prev1
                sto_ref[:, 1, cs] = u
            else:
                prev8 = carry_ref[:, cs]
                prev1 = _delay_rows(u, prev8, 1)
                prev2 = _delay_rows(u, prev8, 2)
                carry_ref[:, cs] = u[tm - SUBLANES:tm]
            halves.append(fcb_ref[:, cs] + prev2 * fcw_ref[0:1, cs] + prev1 * fcw_ref[1:2, cs]
                          + u * fcw_ref[2:3, cs])
        act = (_silu(halves[0]) * halves[1]).astype(BF16)
        acc = acc + jnp.dot(act, wdn_ref[k * FFN_CHUNK:(k + 1) * FFN_CHUNK, :], preferred_element_type=F32)
    y_ref[...] = _rms(acc, fn_ref[...])
    if not decode:
        @pl.when(j == pl.num_programs(1) - 1)
        def _():
            ust_ref[0] = carry_ref[SUBLANES - (FFN_CONV_W - 1):SUBLANES, :]


def _outffn_prompt(x2d, ys, yp, wout, n2, wup, fcw, fcb, wdn, fn, nseq, seq, tm):
    nj = seq // tm
    row = lambda b, j: (b * nj + j, 0)
    return pl.pallas_call(
        functools.partial(_outffn_kernel, tm=tm, decode=False),
        grid=(nseq, nj),
        in_specs=[pl.BlockSpec((tm, D_MODEL), row),
                  pl.BlockSpec((tm, SSD_WIDTH), row),
                  pl.BlockSpec((tm, POOL_WIDTH), row),
                  _const_spec((D_MIX, D_MODEL)),
                  _const_spec((1, D_MODEL)),
                  _const_spec((D_MODEL, 2 * D_FF)),
                  _const_spec((FFN_CONV_W, 2 * D_FF)),
                  _const_spec((1, 2 * D_FF)),
                  _const_spec((D_FF, D_MODEL)),
                  _const_spec((1, D_MODEL))],
        out_specs=[pl.BlockSpec((tm, D_MODEL), row),
                   pl.BlockSpec((1, FFN_CONV_W - 1, 2 * D_FF), lambda b, j: (b, 0, 0))],
        out_shape=[jax.ShapeDtypeStruct((nseq * seq, D_MODEL), F32),
                   jax.ShapeDtypeStruct((nseq, FFN_CONV_W - 1, 2 * D_FF), F32)],
        scratch_shapes=[pltpu.VMEM((SUBLANES, 2 * D_FF), F32)],
        compiler_params=pltpu.CompilerParams(dimension_semantics=("arbitrary", "arbitrary"),
                                             vmem_limit_bytes=VMEM_LIMIT),
        name="outffn_prompt",
    )(x2d, ys, yp, wout, n2, wup, fcw, fcb, wdn, fn)


def _outffn_decode(x2d, ys, yp, ffn_state, wout, n2, wup, fcw, fcb, wdn, fn):
    n = x2d.shape[0]
    full = lambda shape: pl.BlockSpec(shape, lambda i: (0,) * len(shape))
    return pl.pallas_call(
        functools.partial(_outffn_kernel, tm=n, decode=True),
        grid=(1,),
        in_specs=[full((n, D_MODEL)), full((n, SSD_WIDTH)), full((n, POOL_WIDTH)), full(ffn_state.shape),
                  _const_spec((D_MIX, D_MODEL)),
                  _const_spec((1, D_MODEL)),
                  _const_spec((D_MODEL, 2 * D_FF)),
                  _const_spec((FFN_CONV_W, 2 * D_FF)),
                  _const_spec((1, 2 * D_FF)),
                  _const_spec((D_FF, D_MODEL)),
                  _const_spec((1, D_MODEL))],
        out_specs=[full((n, D_MODEL)), full(ffn_state.shape)],
        out_shape=[jax.ShapeDtypeStruct((n, D_MODEL), F32),
                   jax.ShapeDtypeStruct(ffn_state.shape, F32)],
        compiler_params=pltpu.CompilerParams(dimension_semantics=("arbitrary",),
                                             vmem_limit_bytes=VMEM_LIMIT),
        name="outffn_decode",
    )(x2d, ys, yp, ffn_state, wout, n2, wup, fcw, fcb, wdn, fn)


def kernel(x_prompt, x_sample, state_ssm, state_conv, state_pool, state_ffn_conv, norm1_w, w_in, conv_w, conv_b,
           dt_bias, a_log, d_skip, ssd_norm_w, pool_w, pool_scale, w_out, norm2_w, w_up, ffn_conv_w, ffn_conv_b,
           w_down, final_norm_w):
    assert w_in.shape[0] == 1, "single-layer model"
    nseq, seq, _ = x_prompt.shape
    nsmp, one, _ = x_sample.shape
    assert one == 1 and seq % CHUNK == 0 and nsmp == LANES

    wi = w_in[0]
    w_dt = jnp.pad(wi[:, SSD_WIDTH + CONV_DIM:SSD_WIDTH + CONV_DIM + N_HEADS], ((0, 0), (0, DT_PAD - N_HEADS)))
    w_cat = jnp.concatenate([wi[:, :SSD_WIDTH], wi[:, SSD_WIDTH:SSD_WIDTH + CONV_DIM],
                             wi[:, SSD_WIDTH + CONV_DIM + N_HEADS:], w_dt], axis=1).astype(BF16)
    n1 = norm1_w[0][None]
    cw, cb = conv_w[0], conv_b[0][None]
    dtb = jnp.pad(dt_bias[0], (0, DT_PAD - N_HEADS))[None]
    alog = jnp.pad(a_log[0], (0, DT_PAD - N_HEADS))[None]
    dskip = jnp.repeat(d_skip[0], HEAD_DIM)[None]
    nw = ssd_norm_w[0][None]
    pw = pool_w[0].astype(BF16)
    psc = pool_scale[0][None]
    wout = w_out[0].astype(BF16)
    n2 = norm2_w[0][None]
    wup = w_up[0].astype(BF16)
    fcw, fcb = ffn_conv_w[0], ffn_conv_b[0][None]
    wdn = w_down[0].astype(BF16)
    fn = final_norm_w[None]

    xp = x_prompt.reshape(nseq * seq, D_MODEL)
    xs_p, bc_p, dt_p, gate_p, yp_p, ctail, vtail = _inproj_prompt(xp, n1, w_cat, cw, cb, dtb, pw, psc,
                                                                  nseq, seq, tm=512)
    per_seq = lambda a: a.reshape(nseq, seq, a.shape[-1])
    ys_p, hfin = _ssd_prompt(per_seq(xs_p), per_seq(bc_p), per_seq(dt_p), per_seq(gate_p), alog, dskip, nw,
                             nseq, seq)
    ys_p = ys_p.reshape(nseq * seq, SSD_WIDTH)
    y_p, ffn_p = _outffn_prompt(xp, ys_p, yp_p, wout, n2, wup, fcw, fcb, wdn, fn, nseq, seq, tm=256)

    xs_in = x_sample.reshape(nsmp, D_MODEL)
    z_s, xbc_s, v_s, dt_s = _inproj(xs_in, n1, w_cat, tm=nsmp)
    h_t = jnp.transpose(state_ssm[0], (1, 2, 3, 0))
    cs_t = jnp.transpose(state_conv[0], (1, 0, 2))
    ps_t = jnp.transpose(state_pool[0], (1, 0, 2))
    hnew_t, ys_s, yp_s, cs_new, ps_new = _step_mix(xbc_s, z_s, dt_s, v_s, cs_t, ps_t, h_t, cw, cb, dtb, alog,
                                                   dskip, nw, pw, psc, PAST_LEN)
    y_s, ffn_new = _outffn_decode(xs_in, ys_s, yp_s, state_ffn_conv[0], wout, n2, wup, fcw, fcb, wdn, fn)

    return (y_p.reshape(nseq, seq, D_MODEL),
            y_s.reshape(nsmp, 1, D_MODEL),
            hfin.reshape(nseq, N_HEADS, HEAD_DIM, D_STATE)[None],
            jnp.transpose(hnew_t, (3, 0, 1, 2))[None],
            ctail[:, SUBLANES - (CONV_W - 1):][None],
            jnp.transpose(cs_new, (1, 0, 2))[None],
            vtail[:, 2 * SUBLANES - (POOL_MAX - 1):][None],
            jnp.transpose(ps_new, (1, 0, 2))[None],
            ffn_p[None],
            ffn_new[None])
```

```python
import functools

import jax
import jax.numpy as jnp
from jax import lax
from jax.experimental import pallas as pl
from jax.experimental.pallas import tpu as pltpu

F32 = jnp.float32
BF16 = jnp.bfloat16

D_MODEL = 1024
SSD_WIDTH = 1536
HEAD_DIM = 64
N_HEADS = SSD_WIDTH // HEAD_DIM
N_GROUPS = 4
HEADS_PER_GROUP = N_HEADS // N_GROUPS
D_STATE = 64
BC_WIDTH = N_GROUPS * D_STATE
CONV_DIM = SSD_WIDTH + 2 * BC_WIDTH
CONV_W = 4
CHUNK = 128
POOL_WIDTH = 512
POOL_WINDOWS = (2, 4, 8, 16)
POOL_GROUP_DIM = POOL_WIDTH // len(POOL_WINDOWS)
POOL_MAX = max(POOL_WINDOWS)
D_MIX = SSD_WIDTH + POOL_WIDTH
D_FF = 2816
FFN_CONV_W = 3
NORM_GROUP = SSD_WIDTH // N_GROUPS
EPS = 1e-6
LOG2_E = 1.4426950408889634
PAST_LEN = 16384

LANES = 128
SUBLANES = 8
DT_PAD = LANES
VMEM_LIMIT = 56 * 1024 * 1024
HEAD_PAIRS = N_HEADS // 2

_Z0, _Z1 = 0, SSD_WIDTH
_X0, _X1 = _Z1, _Z1 + CONV_DIM
_V0, _V1 = _X1, _X1 + POOL_WIDTH
_T0, _T1 = _V1, _V1 + DT_PAD


def _rms(x, w):
    ms = jnp.mean(x * x, axis=-1, keepdims=True)
    return x * lax.rsqrt(ms + EPS) * w


def _silu_half(h):
    return h + h * jnp.tanh(h)


def _silu(x):
    return _silu_half(0.5 * x)


def _softplus(x):
    return jnp.maximum(x, 0.0) + jnp.log1p(jnp.exp(-jnp.abs(x)))


def _const_spec(shape):
    nd = len(shape)
    return pl.BlockSpec(shape, lambda *_: (0,) * nd, pipeline_mode=pl.Buffered(1))


def _delay_rows(u, prev8, s):
    if s == SUBLANES:
        return jnp.concatenate([prev8, u[:u.shape[0] - SUBLANES]], axis=0)
    r = pltpu.roll(u, s, axis=0)
    c = pltpu.roll(prev8, s, axis=0)
    row = lax.broadcasted_iota(jnp.int32, prev8.shape, 0)
    top = jnp.where(row < s, c, r[0:SUBLANES])
    return jnp.concatenate([top, r[SUBLANES:]], axis=0)


def _gate_and_norm(y, xs, gate, dskip, nw):
    y = (y + dskip * xs) * gate
    outs = []
    for g in range(N_GROUPS):
        sl = slice(g * NORM_GROUP, (g + 1) * NORM_GROUP)
        outs.append(_rms(y[:, sl], nw[:, sl]))
    return jnp.concatenate(outs, axis=1)


def _pool_project(win_sums, v, cnts, pw_ref, pscale):
    outs = []
    for g in range(len(POOL_WINDOWS)):
        sl = slice(g * POOL_GROUP_DIM, (g + 1) * POOL_GROUP_DIM)
        m = win_sums[g] * (1.0 / cnts[g]) - v[:, sl]
        outs.append(jnp.dot(m.astype(BF16), pw_ref[g], preferred_element_type=F32))
    return jnp.concatenate(outs, axis=1) * pscale


def _cumsum_rows(x):
    n = x.shape[0]
    row = lax.broadcasted_iota(jnp.int32, x.shape, 0)
    k = 1
    while k < n:
        x = x + jnp.where(row >= k, pltpu.roll(x, k, axis=0), 0.0)
        k *= 2
    return x


def _inproj_kernel(*refs, tm, prompt, pos0):
    if prompt:
        (x_ref, nw_ref, w_ref, cw_ref, cb_ref, dtb_ref, alog_ref, pw_ref, psc_ref,
         xs_ref, bc_ref, bmt_ref, acum_ref, ast_ref, gate_ref, yp_ref, ctail_ref, vtail_ref,
         cc_ref, vc_ref, s2c_ref, s4c_ref, s8c_ref) = refs
    else:
        x_ref, nw_ref, w_ref, z_ref, xbc_ref, v_ref, dt_ref = refs
    hn = _rms(x_ref[...], nw_ref[...]).astype(BF16)

    def mm(lo, hi):
        return jnp.dot(hn, w_ref[:, lo:hi], preferred_element_type=F32)

    if not prompt:
        z_ref[...] = mm(_Z0, _Z1)
        xbc_ref[...] = mm(_X0, _X1)
        v_ref[...] = mm(_V0, _V1)
        dt_ref[...] = mm(_T0, _T1)
        return

    j = pl.program_id(1)
    carries = (cc_ref, vc_ref, s2c_ref, s4c_ref, s8c_ref)

    @pl.when(j == 0)
    def _():
        for r in carries:
            r[...] = jnp.zeros(r.shape, F32)

    last = j == pl.num_programs(1) - 1

    def conv_stage(lo, hi):
        def fn(xbc):
            prev8 = cc_ref[:, lo:hi]
            conv = cb_ref[:, lo:hi] + xbc * cw_ref[CONV_W - 1:CONV_W, lo:hi]
            for k in range(CONV_W - 1):
                conv = conv + _delay_rows(xbc, prev8, CONV_W - 1 - k) * cw_ref[k:k + 1, lo:hi]
            cc_ref[:, lo:hi] = xbc[tm - SUBLANES:tm]
            act = _silu_half(conv)
            if hi <= SSD_WIDTH:
                xs_ref[:, lo:hi] = act.astype(BF16)
            else:
                bc_ref[:, lo - SSD_WIDTH:hi - SSD_WIDTH] = act
                for c in range(tm // CHUNK):
                    for i in range(BC_WIDTH // LANES):
                        blk = act[c * CHUNK:(c + 1) * CHUNK, i * LANES:(i + 1) * LANES]
                        bmt_ref[c * BC_WIDTH + i * LANES:c * BC_WIDTH + (i + 1) * LANES, :] = blk.T
        return fn

    def pool_stage(v):
        s2 = v + _delay_rows(v, vc_ref[...], 1)
        s4 = s2 + _delay_rows(s2, s2c_ref[...], 2)
        s8 = s4 + _delay_rows(s4, s4c_ref[...], 4)
        s16 = s8 + _delay_rows(s8, s8c_ref[...], 8)
        for r, val in zip(carries[1:], (v, s2, s4, s8)):
            r[...] = val[tm - SUBLANES:tm]
        pos = pos0 + j * tm + lax.broadcasted_iota(jnp.int32, (tm, 1), 0)
        sums, cnts = [], []
        for g, (w, s) in enumerate(zip(POOL_WINDOWS, (s2, s4, s8, s16))):
            sums.append(s[:, g * POOL_GROUP_DIM:(g + 1) * POOL_GROUP_DIM])
            cnts.append(jnp.minimum(pos + 1, w).astype(F32))
        yp_ref[...] = _pool_project(sums, v, cnts, pw_ref, psc_ref[...]).astype(BF16)

        @pl.when(last)
        def _():
            vtail_ref[0] = v[tm - 2 * SUBLANES:tm]

    def gate_stage(lo, hi):
        def fn(half_z):
            gate_ref[:, lo:hi] = _silu_half(half_z)
        return fn

    def dt_stage(dt_raw):
        dt = _softplus(dt_raw + dtb_ref[...])
        neg_a = -jnp.exp(alog_ref[...])
        for c in range(tm // CHUNK):
            rows = slice(c * CHUNK, (c + 1) * CHUNK)
            acum = _cumsum_rows(dt[rows] * neg_a) * LOG2_E
            acum_ref[rows, :] = acum
            ast_ref[rows, :] = acum.T - jnp.log2(dt[rows]).T

    stages = [((_Z0, _Z1), gate_stage(0, SSD_WIDTH)),
              ((_T0, _T1), dt_stage),
              ((_X0, _X0 + SSD_WIDTH), conv_stage(0, SSD_WIDTH)),
              ((_X0 + SSD_WIDTH, _X1), conv_stage(SSD_WIDTH, CONV_DIM)),
              ((_V0, _V1), pool_stage)]
    for cols, fn in stages:
        fn(mm(*cols))

    @pl.when(last)
    def _():
        ctail_ref[0] = cc_ref[...]


def _inproj(x2d, norm_w, w_cat, tm):
    m = x2d.shape[0]
    row = lambda i: (i, 0)
    return pl.pallas_call(
        functools.partial(_inproj_kernel, tm=tm, prompt=False, pos0=0),
        grid=(m // tm,),
        in_specs=[pl.BlockSpec((tm, D_MODEL), row),
                  _const_spec((1, D_MODEL)),
                  _const_spec((D_MODEL, _T1))],
        out_specs=[pl.BlockSpec((tm, SSD_WIDTH), row),
                   pl.BlockSpec((tm, CONV_DIM), row),
                   pl.BlockSpec((tm, POOL_WIDTH), row),
                   pl.BlockSpec((tm, DT_PAD), row)],
        out_shape=[jax.ShapeDtypeStruct((m, SSD_WIDTH), F32),
                   jax.ShapeDtypeStruct((m, CONV_DIM), F32),
                   jax.ShapeDtypeStruct((m, POOL_WIDTH), F32),
                   jax.ShapeDtypeStruct((m, DT_PAD), F32)],
        compiler_params=pltpu.CompilerParams(dimension_semantics=("arbitrary",),
                                             vmem_limit_bytes=VMEM_LIMIT),
        name="inproj",
    )(x2d, norm_w, w_cat)


def _inproj_prompt(x2d, norm_w, w_cat, cw, cb, dtb, alog, pw, psc, nseq, seq, tm):
    nj = seq // tm
    m = nseq * seq
    row = lambda b, j: (b * nj + j, 0)
    per_seq = lambda b, j: (b, 0, 0)
    return pl.pallas_call(
        functools.partial(_inproj_kernel, tm=tm, prompt=True, pos0=0),
        grid=(nseq, nj),
        in_specs=[pl.BlockSpec((tm, D_MODEL), row),
                  _const_spec((1, D_MODEL)),
                  _const_spec((D_MODEL, _T1)),
                  _const_spec((CONV_W, CONV_DIM)),
                  _const_spec((1, CONV_DIM)),
                  _const_spec((1, DT_PAD)),
                  _const_spec((1, DT_PAD)),
                  _const_spec((len(POOL_WINDOWS), POOL_GROUP_DIM, POOL_GROUP_DIM)),
                  _const_spec((1, POOL_WIDTH))],
        out_specs=[pl.BlockSpec((tm, SSD_WIDTH), row),
                   pl.BlockSpec((tm, 2 * BC_WIDTH), row),
                   pl.BlockSpec((2 * tm, LANES), row),
                   pl.BlockSpec((tm, DT_PAD), row),
                   pl.BlockSpec((tm, DT_PAD), row),
                   pl.BlockSpec((tm, SSD_WIDTH), row),
                   pl.BlockSpec((tm, POOL_WIDTH), row),
                   pl.BlockSpec((1, SUBLANES, CONV_DIM), per_seq),
                   pl.BlockSpec((1, 2 * SUBLANES, POOL_WIDTH), per_seq)],
        out_shape=[jax.ShapeDtypeStruct((m, SSD_WIDTH), BF16),
                   jax.ShapeDtypeStruct((m, 2 * BC_WIDTH), F32),
                   jax.ShapeDtypeStruct((2 * m, LANES), F32),
                   jax.ShapeDtypeStruct((m, DT_PAD), F32),
                   jax.ShapeDtypeStruct((m, DT_PAD), F32),
                   jax.ShapeDtypeStruct((m, SSD_WIDTH), F32),
                   jax.ShapeDtypeStruct((m, POOL_WIDTH), BF16),
                   jax.ShapeDtypeStruct((nseq, SUBLANES, CONV_DIM), F32),
                   jax.ShapeDtypeStruct((nseq, 2 * SUBLANES, POOL_WIDTH), F32)],
        scratch_shapes=[pltpu.VMEM((SUBLANES, CONV_DIM), F32),
                        pltpu.VMEM((SUBLANES, POOL_WIDTH), F32),
                        pltpu.VMEM((SUBLANES, POOL_WIDTH), F32),
                        pltpu.VMEM((SUBLANES, POOL_WIDTH), F32),
                        pltpu.VMEM((SUBLANES, POOL_WIDTH), F32)],
        compiler_params=pltpu.CompilerParams(dimension_semantics=("arbitrary", "arbitrary"),
                                             vmem_limit_bytes=VMEM_LIMIT),
        name="inproj_prompt",
    )(x2d, norm_w, w_cat, cw, cb, dtb, alog, pw, psc)


SSD_SEQS = 1


def _ssd_chunk(q, xs_ref, bc_ref, bmt_ref, acum_ref, ast_ref, gate_ref, dskip, nw, y_ref, state_ref):
    xs = xs_ref[q]
    bm = bc_ref[q, :, :BC_WIDTH]
    cm = bc_ref[q, :, BC_WIDTH:]
    bm_t = bmt_ref[q]
    acum = acum_ref[q]
    as_t = ast_ref[q]
    li = lax.broadcasted_iota(jnp.int32, (CHUNK, CHUNK), 0)
    si = lax.broadcasted_iota(jnp.int32, (CHUNK, CHUNK), 1)
    causal = li >= si
    first_half = si < HEAD_DIM
    first_half_n = lax.broadcasted_iota(jnp.int32, (D_STATE, LANES), 1) < HEAD_DIM
    first_half_row = lax.broadcasted_iota(jnp.int32, (1, LANES), 1) < HEAD_DIM
    zeros_n = jnp.zeros((D_STATE, LANES), BF16)
    updates = []

    for g in range(N_GROUPS):
        gs = slice(g * D_STATE, (g + 1) * D_STATE)
        c_g = cm[:, gs]
        c2 = jnp.concatenate([c_g, c_g], axis=1)
        cb = lax.dot_general(c_g.astype(BF16), bm[:, gs].astype(BF16), (((1,), (1,)), ((), ())),
                             preferred_element_type=F32)
        bt_g = bm_t[gs, :]
        y_pairs = []
        for k in range(HEADS_PER_GROUP // 2):
            pair = (g * HEADS_PER_GROUP) // 2 + k
            x_pair = xs[:, pair * LANES:(pair + 1) * LANES]
            s_pair = state_ref[q, pair]
            s_b = s_pair.astype(BF16)
            heads = (2 * pair, 2 * pair + 1)
            a_l = [jnp.broadcast_to(acum[:, h:h + 1], (CHUNK, CHUNK)) for h in heads]
            cdec = (c2 * jnp.exp2(jnp.where(first_half, a_l[0], a_l[1]))).astype(BF16)
            rhs = [jnp.concatenate([x_pair, s_b, zeros_n], axis=0),
                   jnp.concatenate([x_pair, zeros_n, s_b], axis=0)]
            res, lhs_s, cds = [], [], []
            for i, h in enumerate(heads):
                a_s = as_t[h:h + 1, :]
                mix = cb * jnp.exp2(jnp.where(causal, a_l[i] - a_s, -jnp.inf))
                lhs = jnp.concatenate([mix.astype(BF16), cdec], axis=1)
                res.append(jnp.dot(lhs, rhs[i], preferred_element_type=F32))
                a_last = acum[CHUNK - 1:CHUNK, h:h + 1]
                lhs_s.append(bt_g * jnp.exp2(a_last - a_s))
                cds.append(jnp.exp2(a_last))
            y_pairs.append(jnp.where(first_half, res[0], res[1]))
            updates.append((pair, jnp.concatenate(lhs_s, axis=0).astype(BF16), x_pair,
                            jnp.where(first_half_row, cds[0], cds[1]) * s_pair))

        ns = slice(g * NORM_GROUP, (g + 1) * NORM_GROUP)
        y_g = jnp.concatenate(y_pairs, axis=1)
        y_g = (y_g + dskip[:, ns] * xs[:, ns].astype(F32)) * gate_ref[q, :, ns]
        y_ref[q, :, ns] = _rms(y_g, nw[:, ns]).astype(BF16)

    for pair, lhs_s, x_pair, decayed in updates:
        upd = jnp.dot(lhs_s, x_pair, preferred_element_type=F32)
        state_ref[q, pair] = decayed + jnp.where(first_half_n, upd[:D_STATE], upd[D_STATE:])


def _ssd_prompt_kernel(xs_ref, bc_ref, bmt_ref, acum_ref, ast_ref, gate_ref, dskip_ref, nw_ref,
                       y_ref, hfin_ref, state_ref):
    c = pl.program_id(1)

    @pl.when(c == 0)
    def _():
        state_ref[...] = jnp.zeros(state_ref.shape, F32)

    for q in range(SSD_SEQS):
        _ssd_chunk(q, xs_ref, bc_ref, bmt_ref, acum_ref, ast_ref, gate_ref, dskip_ref[...], nw_ref[...],
                   y_ref, state_ref)

    @pl.when(c == pl.num_programs(1) - 1)
    def _():
        for q in range(SSD_SEQS):
            for pair in range(HEAD_PAIRS):
                t = jnp.concatenate([state_ref[q, pair], jnp.zeros((LANES - D_STATE, LANES), F32)], axis=0).T
                hfin_ref[q, pair * LANES:(pair + 1) * LANES, :] = t[:, :D_STATE]


def _ssd_prompt(xs, bc, bmt, acum, ast, gate, dskip, nw, nseq, seq):
    nc = seq // CHUNK
    blk = lambda width, rows=CHUNK: pl.BlockSpec((SSD_SEQS, rows, width), lambda b, c: (b, c, 0))
    return pl.pallas_call(
        _ssd_prompt_kernel,
        grid=(nseq // SSD_SEQS, nc),
        in_specs=[blk(SSD_WIDTH), blk(2 * BC_WIDTH), blk(LANES, BC_WIDTH), blk(DT_PAD), blk(DT_PAD),
                  blk(SSD_WIDTH),
                  _const_spec((1, SSD_WIDTH)),
                  _const_spec((1, SSD_WIDTH))],
        out_specs=[blk(SSD_WIDTH),
                   pl.BlockSpec((SSD_SEQS, SSD_WIDTH, D_STATE), lambda b, c: (b, 0, 0))],
        out_shape=[jax.ShapeDtypeStruct((nseq, seq, SSD_WIDTH), BF16),
                   jax.ShapeDtypeStruct((nseq, SSD_WIDTH, D_STATE), F32)],
        scratch_shapes=[pltpu.VMEM((SSD_SEQS, HEAD_PAIRS, D_STATE, LANES), F32)],
        compiler_params=pltpu.CompilerParams(dimension_semantics=("arbitrary", "arbitrary"),
                                             vmem_limit_bytes=VMEM_LIMIT),
        name="ssd_prompt",
    )(xs, bc, bmt, acum, ast, gate, dskip, nw)


def _step_mix_kernel(xbc_ref, z_ref, dt_ref, v_ref, cs_ref, ps_ref, h_ref,
                     cw_ref, cb_ref, dtb_ref, alog_ref, dskip_ref, nw_ref, pw_ref, psc_ref,
                     hout_ref, y_ref, yp_ref, cso_ref, pso_ref,
                     xs_ref, xdt_ref, bt_ref, ct_ref, dect_ref, yt_ref, *, pos0):
    h = pl.program_id(0)
    nprev = POOL_MAX - 1

    @pl.when(h == 0)
    def _prepare():
        xnew = xbc_ref[...]
        conv = cb_ref[...] + xnew * cw_ref[CONV_W - 1:CONV_W, :]
        for k in range(CONV_W - 1):
            conv = conv + cs_ref[k] * cw_ref[k:k + 1, :]
        xbc = _silu(conv)
        xs = xbc[:, :SSD_WIDTH]
        xs_ref[...] = xs
        dt = _softplus(dt_ref[...] + dtb_ref[...])
        dect_ref[...] = jnp.exp(dt * (-jnp.exp(alog_ref[...]))).T
        dt_t = dt.T
        for hh in range(N_HEADS):
            blk = xs[:, (hh // 2) * LANES:(hh // 2 + 1) * LANES].T
            half = blk[(hh % 2) * HEAD_DIM:(hh % 2 + 1) * HEAD_DIM, :]
            xdt_ref[hh * HEAD_DIM:(hh + 1) * HEAD_DIM, :] = half * dt_t[hh:hh + 1, :]
        for i in range(BC_WIDTH // LANES):
            sl = slice(i * LANES, (i + 1) * LANES)
            bt_ref[sl, :] = xbc[:, SSD_WIDTH + i * LANES:SSD_WIDTH + (i + 1) * LANES].T
            ct_ref[sl, :] = xbc[:, SSD_WIDTH + BC_WIDTH + i * LANES:SSD_WIDTH + BC_WIDTH + (i + 1) * LANES].T
        for k in range(CONV_W - 2):
            cso_ref[k] = cs_ref[k + 1]
        cso_ref[CONV_W - 2] = xnew
        for k in range(nprev - 1):
            pso_ref[k] = ps_ref[k + 1]
        pso_ref[nprev - 1] = v_ref[...]

    g = lax.div(h, HEADS_PER_GROUP)
    hrow = pl.multiple_of(h * HEAD_DIM, HEAD_DIM)
    grow = pl.multiple_of(g * D_STATE, D_STATE)
    b_t = bt_ref[pl.ds(grow, D_STATE), :]
    c_t = ct_ref[pl.ds(grow, D_STATE), :]
    dec = dect_ref[pl.ds(h, 1), :]

    def per_p(p, carry):
        xrow = xdt_ref[pl.ds(hrow + p, 1), :]
        hn = h_ref[0, p] * dec + xrow * b_t
        hout_ref[0, p] = hn
        yt_ref[pl.ds(hrow + p, 1), :] = jnp.sum(hn * c_t, axis=0, keepdims=True)
        return carry

    lax.fori_loop(0, HEAD_DIM, per_p, 0, unroll=4)

    @pl.when(h == pl.num_programs(0) - 1)
    def _finish():
        y0 = jnp.concatenate([yt_ref[i * LANES:(i + 1) * LANES, :].T for i in range(SSD_WIDTH // LANES)], axis=1)
        y = _gate_and_norm(y0, xs_ref[...], _silu_half(z_ref[...]), dskip_ref[...], nw_ref[...])
        y_ref[...] = y.astype(BF16)
        v = v_ref[...]
        sums, cnts = [], []
        for gi, w in enumerate(POOL_WINDOWS):
            sl = slice(gi * POOL_GROUP_DIM, (gi + 1) * POOL_GROUP_DIM)
            acc = v[:, sl]
            for j in range(1, w):
                acc = acc + ps_ref[nprev - j][:, sl]
            sums.append(acc)
            cnts.append(float(min(pos0 + 1, w)))
        yp_ref[...] = _pool_project(sums, v, cnts, pw_ref, psc_ref[...]).astype(BF16)


def _step_mix(xbc, z, dt, v, cs_t, ps_t, h_t, cw, cb, dtb, alog, dskip, nw, pw, psc, pos0):
    n = xbc.shape[0]
    nprev = POOL_MAX - 1
    full = lambda shape: pl.BlockSpec(shape, lambda i: (0,) * len(shape))
    hblk = pl.BlockSpec((1, HEAD_DIM, D_STATE, n), lambda i: (i, 0, 0, 0))
    return pl.pallas_call(
        functools.partial(_step_mix_kernel, pos0=pos0),
        grid=(N_HEADS,),
        in_specs=[full((n, CONV_DIM)), full((n, SSD_WIDTH)), full((n, DT_PAD)), full((n, POOL_WIDTH)),
                  full((CONV_W - 1, n, CONV_DIM)), full((nprev, n, POOL_WIDTH)), hblk,
                  full((CONV_W, CONV_DIM)), full((1, CONV_DIM)), full((1, DT_PAD)), full((1, DT_PAD)),
                  full((1, SSD_WIDTH)), full((1, SSD_WIDTH)),
                  full((len(POOL_WINDOWS), POOL_GROUP_DIM, POOL_GROUP_DIM)), full((1, POOL_WIDTH))],
        out_specs=[hblk, full((n, SSD_WIDTH)), full((n, POOL_WIDTH)),
                   full((CONV_W - 1, n, CONV_DIM)), full((nprev, n, POOL_WIDTH))],
        out_shape=[jax.ShapeDtypeStruct(h_t.shape, F32),
                   jax.ShapeDtypeStruct((n, SSD_WIDTH), BF16),
                   jax.ShapeDtypeStruct((n, POOL_WIDTH), BF16),
                   jax.ShapeDtypeStruct((CONV_W - 1, n, CONV_DIM), F32),
                   jax.ShapeDtypeStruct((nprev, n, POOL_WIDTH), F32)],
        scratch_shapes=[pltpu.VMEM((n, SSD_WIDTH), F32),
                        pltpu.VMEM((SSD_WIDTH, n), F32),
                        pltpu.VMEM((BC_WIDTH, n), F32),
                        pltpu.VMEM((BC_WIDTH, n), F32),
                        pltpu.VMEM((DT_PAD, n), F32),
                        pltpu.VMEM((SSD_WIDTH, n), F32)],
        compiler_params=pltpu.CompilerParams(dimension_semantics=("arbitrary",),
                                             vmem_limit_bytes=VMEM_LIMIT),
        name="step_mix",
    )(xbc, z, dt, v, cs_t, ps_t, h_t, cw, cb, dtb, alog, dskip, nw, pw, psc)


FFN_CHUNK = 256


def _outffn_kernel(*refs, tm, decode):
    if decode:
        (x_ref, ys_ref, yp_ref, st_ref, wout_ref, n2_ref, wup_ref, fcw_ref, fcb_ref, wdn_ref, fn_ref,
         y_ref, sto_ref) = refs
    else:
        (x_ref, ys_ref, yp_ref, wout_ref, n2_ref, wup_ref, fcw_ref, fcb_ref, wdn_ref, fn_ref,
         y_ref, ust_ref, carry_ref) = refs
        j = pl.program_id(1)

        @pl.when(j == 0)
        def _():
            carry_ref[...] = jnp.zeros(carry_ref.shape, F32)

    ymix = jnp.concatenate([ys_ref[...], yp_ref[...]], axis=1)
    x1 = x_ref[...] + jnp.dot(ymix, wout_ref[...], preferred_element_type=F32)
    hn = _rms(x1, n2_ref[...]).astype(BF16)
    acc = x1
    nchunk = D_FF // FFN_CHUNK

    def col_slices(k):
        return [slice(base + k * FFN_CHUNK, base + (k + 1) * FFN_CHUNK) for base in (0, D_FF)]

    def up_proj(k):
        return [jnp.dot(hn, wup_ref[:, cs], preferred_element_type=F32) for cs in col_slices(k)]

    u_next = up_proj(0)
    for k in range(nchunk):
        u_cur = u_next
        if k + 1 < nchunk:
            u_next = up_proj(k + 1)
        halves = []
        for u, cs in zip(u_cur, col_slices(k)):
            if decode:
                prev2, prev1 = st_ref[:, 0, cs], st_ref[:, 1, cs]
                sto_ref[:, 0, cs] = prev1
                sto_ref[:, 1, cs] = u
            else:
                prev8 = carry_ref[:, cs]
                prev1 = _delay_rows(u, prev8, 1)
                prev2 = _delay_rows(u, prev8, 2)
                carry_ref[:, cs] = u[tm - SUBLANES:tm]
            halves.append(fcb_ref[:, cs] + prev2 * fcw_ref[0:1, cs] + prev1 * fcw_ref[1:2, cs]
                          + u * fcw_ref[2:3, cs])
        act = (_silu_half(halves[0]) * halves[1]).astype(BF16)
        acc = acc + jnp.dot(act, wdn_ref[k * FFN_CHUNK:(k + 1) * FFN_CHUNK, :], preferred_element_type=F32)
    y_ref[...] = _rms(acc, fn_ref[...])
    if not decode:
        @pl.when(j == pl.num_programs(1) - 1)
        def _():
            ust_ref[0] = carry_ref[SUBLANES - (FFN_CONV_W - 1):SUBLANES, :]


def _outffn_prompt(x2d, ys, yp, wout, n2, wup, fcw, fcb, wdn, fn, nseq, seq, tm):
    nj = seq // tm
    row = lambda b, j: (b * nj + j, 0)
    return pl.pallas_call(
        functools.partial(_outffn_kernel, tm=tm, decode=False),
        grid=(nseq, nj),
        in_specs=[pl.BlockSpec((tm, D_MODEL), row),
                  pl.BlockSpec((tm, SSD_WIDTH), row),
                  pl.BlockSpec((tm, POOL_WIDTH), row),
                  _const_spec((D_MIX, D_MODEL)),
                  _const_spec((1, D_MODEL)),
                  _const_spec((D_MODEL, 2 * D_FF)),
                  _const_spec((FFN_CONV_W, 2 * D_FF)),
                  _const_spec((1, 2 * D_FF)),
                  _const_spec((D_FF, D_MODEL)),
                  _const_spec((1, D_MODEL))],
        out_specs=[pl.BlockSpec((tm, D_MODEL), row),
                   pl.BlockSpec((1, FFN_CONV_W - 1, 2 * D_FF), lambda b, j: (b, 0, 0))],
        out_shape=[jax.ShapeDtypeStruct((nseq * seq, D_MODEL), F32),
                   jax.ShapeDtypeStruct((nseq, FFN_CONV_W - 1, 2 * D_FF), F32)],
        scratch_shapes=[pltpu.VMEM((SUBLANES, 2 * D_FF), F32)],
        compiler_params=pltpu.CompilerParams(dimension_semantics=("arbitrary", "arbitrary"),
                                             vmem_limit_bytes=VMEM_LIMIT),
        name="outffn_prompt",
    )(x2d, ys, yp, wout, n2, wup, fcw, fcb, wdn, fn)


def _outffn_decode(x2d, ys, yp, ffn_state, wout, n2, wup, fcw, fcb, wdn, fn):
    n = x2d.shape[0]
    full = lambda shape: pl.BlockSpec(shape, lambda i: (0,) * len(shape))
    return pl.pallas_call(
        functools.partial(_outffn_kernel, tm=n, decode=True),
        grid=(1,),
        in_specs=[full((n, D_MODEL)), full((n, SSD_WIDTH)), full((n, POOL_WIDTH)), full(ffn_state.shape),
                  _const_spec((D_MIX, D_MODEL)),
                  _const_spec((1, D_MODEL)),
                  _const_spec((D_MODEL, 2 * D_FF)),
                  _const_spec((FFN_CONV_W, 2 * D_FF)),
                  _const_spec((1, 2 * D_FF)),
                  _const_spec((D_FF, D_MODEL)),
                  _const_spec((1, D_MODEL))],
        out_specs=[full((n, D_MODEL)), full(ffn_state.shape)],
        out_shape=[jax.ShapeDtypeStruct((n, D_MODEL), F32),
                   jax.ShapeDtypeStruct(ffn_state.shape, F32)],
        compiler_params=pltpu.CompilerParams(dimension_semantics=("arbitrary",),
                                             vmem_limit_bytes=VMEM_LIMIT),
        name="outffn_decode",
    )(x2d, ys, yp, ffn_state, wout, n2, wup, fcw, fcb, wdn, fn)


def kernel(x_prompt, x_sample, state_ssm, state_conv, state_pool, state_ffn_conv, norm1_w, w_in, conv_w, conv_b,
           dt_bias, a_log, d_skip, ssd_norm_w, pool_w, pool_scale, w_out, norm2_w, w_up, ffn_conv_w, ffn_conv_b,
           w_down, final_norm_w):
    assert w_in.shape[0] == 1, "single-layer model"
    nseq, seq, _ = x_prompt.shape
    nsmp, one, _ = x_sample.shape
    assert one == 1 and seq % CHUNK == 0 and nsmp == LANES

    wi = w_in[0]
    w_dt = jnp.pad(wi[:, SSD_WIDTH + CONV_DIM:SSD_WIDTH + CONV_DIM + N_HEADS], ((0, 0), (0, DT_PAD - N_HEADS)))
    w_cat = jnp.concatenate([0.5 * wi[:, :SSD_WIDTH], wi[:, SSD_WIDTH:SSD_WIDTH + CONV_DIM],
                             wi[:, SSD_WIDTH + CONV_DIM + N_HEADS:], w_dt], axis=1).astype(BF16)
    n1 = norm1_w[0][None]
    cw, cb = conv_w[0], conv_b[0][None]
    dtb = jnp.pad(dt_bias[0], (0, DT_PAD - N_HEADS))[None]
    alog = jnp.pad(a_log[0], (0, DT_PAD - N_HEADS))[None]
    dskip = jnp.repeat(d_skip[0], HEAD_DIM)[None]
    nw = ssd_norm_w[0][None]
    pw = pool_w[0].astype(BF16)
    psc = pool_scale[0][None]
    wout = w_out[0].astype(BF16)
    n2 = norm2_w[0][None]
    wup = w_up[0].astype(BF16)
    gate_half = jnp.where(jnp.arange(2 * D_FF) < D_FF, 0.5, 1.0).astype(F32)
    fcw, fcb = ffn_conv_w[0] * gate_half, (ffn_conv_b[0] * gate_half)[None]
    wdn = w_down[0].astype(BF16)
    fn = final_norm_w[None]

    xp = x_prompt.reshape(nseq * seq, D_MODEL)
    xs_p, bc_p, bmt_p, acum_p, ast_p, gate_p, yp_p, ctail, vtail = _inproj_prompt(
        xp, n1, w_cat, 0.5 * cw, 0.5 * cb, dtb, alog, pw, psc, nseq, seq, tm=512)
    per_seq = lambda a: a.reshape(nseq, -1, a.shape[-1])
    ys_p, hfin = _ssd_prompt(per_seq(xs_p), per_seq(bc_p), per_seq(bmt_p), per_seq(acum_p), per_seq(ast_p),
                             per_seq(gate_p), dskip, nw, nseq, seq)
    ys_p = ys_p.reshape(nseq * seq, SSD_WIDTH)
    y_p, ffn_p = _outffn_prompt(xp, ys_p, yp_p, wout, n2, wup, fcw, fcb, wdn, fn, nseq, seq, tm=256)

    xs_in = x_sample.reshape(nsmp, D_MODEL)
    z_s, xbc_s, v_s, dt_s = _inproj(xs_in, n1, w_cat, tm=nsmp)
    h_t = jnp.transpose(state_ssm[0], (1, 2, 3, 0))
    cs_t = jnp.transpose(state_conv[0], (1, 0, 2))
    ps_t = jnp.transpose(state_pool[0], (1, 0, 2))
    hnew_t, ys_s, yp_s, cs_new, ps_new = _step_mix(xbc_s, z_s, dt_s, v_s, cs_t, ps_t, h_t, cw, cb, dtb, alog,
                                                   dskip, nw, pw, psc, PAST_LEN)
    y_s, ffn_new = _outffn_decode(xs_in, ys_s, yp_s, state_ffn_conv[0], wout, n2, wup, fcw, fcb, wdn, fn)

    return (y_p.reshape(nseq, seq, D_MODEL),
            y_s.reshape(nsmp, 1, D_MODEL),
            hfin.reshape(nseq, N_HEADS, HEAD_DIM, D_STATE)[None],
            jnp.transpose(hnew_t, (3, 0, 1, 2))[None],
            ctail[:, SUBLANES - (CONV_W - 1):][None],
            jnp.transpose(cs_new, (1, 0, 2))[None],
            vtail[:, 2 * SUBLANES - (POOL_MAX - 1):][None],
            jnp.transpose(ps_new, (1, 0, 2))[None],
            ffn_p[None],
            ffn_new[None])
```

```python
import functools

import jax
import jax.numpy as jnp
from jax import lax
from jax.experimental import pallas as pl
from jax.experimental.pallas import tpu as pltpu

F32 = jnp.float32
BF16 = jnp.bfloat16

D_MODEL = 1024
SSD_WIDTH = 1536
HEAD_DIM = 64
N_HEADS = SSD_WIDTH // HEAD_DIM
N_GROUPS = 4
HEADS_PER_GROUP = N_HEADS // N_GROUPS
D_STATE = 64
BC_WIDTH = N_GROUPS * D_STATE
CONV_DIM = SSD_WIDTH + 2 * BC_WIDTH
CONV_W = 4
CHUNK = 128
POOL_WIDTH = 512
POOL_WINDOWS = (2, 4, 8, 16)
POOL_GROUP_DIM = POOL_WIDTH // len(POOL_WINDOWS)
POOL_MAX = max(POOL_WINDOWS)
D_MIX = SSD_WIDTH + POOL_WIDTH
D_FF = 2816
FFN_CONV_W = 3
NORM_GROUP = SSD_WIDTH // N_GROUPS
EPS = 1e-6
LOG2_E = 1.4426950408889634
PAST_LEN = 16384

LANES = 128
SUBLANES = 8
DT_PAD = LANES
VMEM_LIMIT = 56 * 1024 * 1024
HEAD_PAIRS = N_HEADS // 2

_Z0, _Z1 = 0, SSD_WIDTH
_X0, _X1 = _Z1, _Z1 + CONV_DIM
_V0, _V1 = _X1, _X1 + POOL_WIDTH
_T0, _T1 = _V1, _V1 + DT_PAD


def _rms(x, w):
    ms = jnp.mean(x * x, axis=-1, keepdims=True)
    return x * lax.rsqrt(ms + EPS) * w


def _silu_half(h):
    return h + h * jnp.tanh(h)


def _silu(x):
    return _silu_half(0.5 * x)


def _softplus(x):
    return jnp.maximum(x, 0.0) + jnp.log1p(jnp.exp(-jnp.abs(x)))


def _const_spec(shape):
    nd = len(shape)
    return pl.BlockSpec(shape, lambda *_: (0,) * nd, pipeline_mode=pl.Buffered(1))


def _delay_rows(u, prev8, s):
    if s == SUBLANES:
        return jnp.concatenate([prev8, u[:u.shape[0] - SUBLANES]], axis=0)
    r = pltpu.roll(u, s, axis=0)
    c = pltpu.roll(prev8, s, axis=0)
    row = lax.broadcasted_iota(jnp.int32, prev8.shape, 0)
    top = jnp.where(row < s, c, r[0:SUBLANES])
    return jnp.concatenate([top, r[SUBLANES:]], axis=0)


def _gate_and_norm(y, xs, gate, dskip, nw):
    y = (y + dskip * xs) * gate
    outs = []
    for g in range(N_GROUPS):
        sl = slice(g * NORM_GROUP, (g + 1) * NORM_GROUP)
        outs.append(_rms(y[:, sl], nw[:, sl]))
    return jnp.concatenate(outs, axis=1)


def _pool_project(win_sums, v, cnts, pw_ref, pscale):
    outs = []
    for g in range(len(POOL_WINDOWS)):
        sl = slice(g * POOL_GROUP_DIM, (g + 1) * POOL_GROUP_DIM)
        m = win_sums[g] * (1.0 / cnts[g]) - v[:, sl]
        outs.append(jnp.dot(m.astype(BF16), pw_ref[g], preferred_element_type=F32))
    return jnp.concatenate(outs, axis=1) * pscale


def _cumsum_rows(x):
    n = x.shape[0]
    row = lax.broadcasted_iota(jnp.int32, x.shape, 0)
    k = 1
    while k < n:
        x = x + jnp.where(row >= k, pltpu.roll(x, k, axis=0), 0.0)
        k *= 2
    return x


def _inproj_kernel(*refs, tm, prompt, pos0):
    if prompt:
        (x_ref, nw_ref, w_ref, cw_ref, cb_ref, dtb_ref, alog_ref, pw_ref, psc_ref,
         xs_ref, bc_ref, bmt_ref, acum_ref, ast_ref, gate_ref, yp_ref, ctail_ref, vtail_ref,
         cc_ref, vc_ref, s2c_ref, s4c_ref, s8c_ref) = refs
    else:
        x_ref, nw_ref, w_ref, z_ref, xbc_ref, v_ref, dt_ref = refs
    hn = _rms(x_ref[...], nw_ref[...]).astype(BF16)

    def mm(lo, hi):
        return jnp.dot(hn, w_ref[:, lo:hi], preferred_element_type=F32)

    if not prompt:
        z_ref[...] = mm(_Z0, _Z1)
        xbc_ref[...] = mm(_X0, _X1)
        v_ref[...] = mm(_V0, _V1)
        dt_ref[...] = mm(_T0, _T1)
        return

    j = pl.program_id(1)
    carries = (cc_ref, vc_ref, s2c_ref, s4c_ref, s8c_ref)

    @pl.when(j == 0)
    def _():
        for r in carries:
            r[...] = jnp.zeros(r.shape, F32)

    last = j == pl.num_programs(1) - 1

    def conv_stage(lo, hi):
        def fn(xbc):
            prev8 = cc_ref[:, lo:hi]
            conv = cb_ref[:, lo:hi] + xbc * cw_ref[CONV_W - 1:CONV_W, lo:hi]
            for k in range(CONV_W - 1):
                conv = conv + _delay_rows(xbc, prev8, CONV_W - 1 - k) * cw_ref[k:k + 1, lo:hi]
            cc_ref[:, lo:hi] = xbc[tm - SUBLANES:tm]
            act = _silu_half(conv)
            if hi <= SSD_WIDTH:
                xs_ref[:, lo:hi] = act.astype(BF16)
            else:
                bc_ref[:, lo - SSD_WIDTH:hi - SSD_WIDTH] = act
                for c in range(tm // CHUNK):
                    for i in range(BC_WIDTH // LANES):
                        blk = act[c * CHUNK:(c + 1) * CHUNK, i * LANES:(i + 1) * LANES]
                        bmt_ref[c * BC_WIDTH + i * LANES:c * BC_WIDTH + (i + 1) * LANES, :] = blk.T
        return fn

    def pool_stage(v):
        s2 = v + _delay_rows(v, vc_ref[...], 1)
        s4 = s2 + _delay_rows(s2, s2c_ref[...], 2)
        s8 = s4 + _delay_rows(s4, s4c_ref[...], 4)
        s16 = s8 + _delay_rows(s8, s8c_ref[...], 8)
        for r, val in zip(carries[1:], (v, s2, s4, s8)):
            r[...] = val[tm - SUBLANES:tm]
        pos = pos0 + j * tm + lax.broadcasted_iota(jnp.int32, (tm, 1), 0)
        sums, cnts = [], []
        for g, (w, s) in enumerate(zip(POOL_WINDOWS, (s2, s4, s8, s16))):
            sums.append(s[:, g * POOL_GROUP_DIM:(g + 1) * POOL_GROUP_DIM])
            cnts.append(jnp.minimum(pos + 1, w).astype(F32))
        yp_ref[...] = _pool_project(sums, v, cnts, pw_ref, psc_ref[...]).astype(BF16)

        @pl.when(last)
        def _():
            vtail_ref[0] = v[tm - 2 * SUBLANES:tm]

    def gate_stage(lo, hi):
        def fn(half_z):
            gate_ref[:, lo:hi] = _silu_half(half_z)
        return fn

    def dt_stage(dt_raw):
        dt = _softplus(dt_raw + dtb_ref[...])
        neg_a = -jnp.exp(alog_ref[...])
        for c in range(tm // CHUNK):
            rows = slice(c * CHUNK, (c + 1) * CHUNK)
            acum = _cumsum_rows(dt[rows] * neg_a) * LOG2_E
            acum_ref[rows, :] = acum
            ast_ref[rows, :] = acum.T - jnp.log2(dt[rows]).T

    stages = [((_Z0, _Z1), gate_stage(0, SSD_WIDTH)),
              ((_T0, _T1), dt_stage),
              ((_X0, _X0 + SSD_WIDTH), conv_stage(0, SSD_WIDTH)),
              ((_X0 + SSD_WIDTH, _X1), conv_stage(SSD_WIDTH, CONV_DIM)),
              ((_V0, _V1), pool_stage)]
    for cols, fn in stages:
        fn(mm(*cols))

    @pl.when(last)
    def _():
        ctail_ref[0] = cc_ref[...]


def _inproj(x2d, norm_w, w_cat, tm):
    m = x2d.shape[0]
    row = lambda i: (i, 0)
    return pl.pallas_call(
        functools.partial(_inproj_kernel, tm=tm, prompt=False, pos0=0),
        grid=(m // tm,),
        in_specs=[pl.BlockSpec((tm, D_MODEL), row),
                  _const_spec((1, D_MODEL)),
                  _const_spec((D_MODEL, _T1))],
        out_specs=[pl.BlockSpec((tm, SSD_WIDTH), row),
                   pl.BlockSpec((tm, CONV_DIM), row),
                   pl.BlockSpec((tm, POOL_WIDTH), row),
                   pl.BlockSpec((tm, DT_PAD), row)],
        out_shape=[jax.ShapeDtypeStruct((m, SSD_WIDTH), F32),
                   jax.ShapeDtypeStruct((m, CONV_DIM), F32),
                   jax.ShapeDtypeStruct((m, POOL_WIDTH), F32),
                   jax.ShapeDtypeStruct((m, DT_PAD), F32)],
        compiler_params=pltpu.CompilerParams(dimension_semantics=("arbitrary",),
                                             vmem_limit_bytes=VMEM_LIMIT),
        name="inproj",
    )(x2d, norm_w, w_cat)


def _inproj_prompt(x2d, norm_w, w_cat, cw, cb, dtb, alog, pw, psc, nseq, seq, tm):
    nj = seq // tm
    m = nseq * seq
    row = lambda b, j: (b * nj + j, 0)
    per_seq = lambda b, j: (b, 0, 0)
    return pl.pallas_call(
        functools.partial(_inproj_kernel, tm=tm, prompt=True, pos0=0),
        grid=(nseq, nj),
        in_specs=[pl.BlockSpec((tm, D_MODEL), row),
                  _const_spec((1, D_MODEL)),
                  _const_spec((D_MODEL, _T1)),
                  _const_spec((CONV_W, CONV_DIM)),
                  _const_spec((1, CONV_DIM)),
                  _const_spec((1, DT_PAD)),
                  _const_spec((1, DT_PAD)),
                  _const_spec((len(POOL_WINDOWS), POOL_GROUP_DIM, POOL_GROUP_DIM)),
                  _const_spec((1, POOL_WIDTH))],
        out_specs=[pl.BlockSpec((tm, SSD_WIDTH), row),
                   pl.BlockSpec((tm, 2 * BC_WIDTH), row),
                   pl.BlockSpec((2 * tm, LANES), row),
                   pl.BlockSpec((tm, DT_PAD), row),
                   pl.BlockSpec((tm, DT_PAD), row),
                   pl.BlockSpec((tm, SSD_WIDTH), row),
                   pl.BlockSpec((tm, POOL_WIDTH), row),
                   pl.BlockSpec((1, SUBLANES, CONV_DIM), per_seq),
                   pl.BlockSpec((1, 2 * SUBLANES, POOL_WIDTH), per_seq)],
        out_shape=[jax.ShapeDtypeStruct((m, SSD_WIDTH), BF16),
                   jax.ShapeDtypeStruct((m, 2 * BC_WIDTH), F32),
                   jax.ShapeDtypeStruct((2 * m, LANES), F32),
                   jax.ShapeDtypeStruct((m, DT_PAD), F32),
                   jax.ShapeDtypeStruct((m, DT_PAD), F32),
                   jax.ShapeDtypeStruct((m, SSD_WIDTH), F32),
                   jax.ShapeDtypeStruct((m, POOL_WIDTH), BF16),
                   jax.ShapeDtypeStruct((nseq, SUBLANES, CONV_DIM), F32),
                   jax.ShapeDtypeStruct((nseq, 2 * SUBLANES, POOL_WIDTH), F32)],
        scratch_shapes=[pltpu.VMEM((SUBLANES, CONV_DIM), F32),
                        pltpu.VMEM((SUBLANES, POOL_WIDTH), F32),
                        pltpu.VMEM((SUBLANES, POOL_WIDTH), F32),
                        pltpu.VMEM((SUBLANES, POOL_WIDTH), F32),
                        pltpu.VMEM((SUBLANES, POOL_WIDTH), F32)],
        compiler_params=pltpu.CompilerParams(dimension_semantics=("arbitrary", "arbitrary"),
                                             vmem_limit_bytes=VMEM_LIMIT),
        name="inproj_prompt",
    )(x2d, norm_w, w_cat, cw, cb, dtb, alog, pw, psc)


SSD_SEQS = 1


def _ssd_chunk(q, xs_ref, bc_ref, bmt_ref, acum_ref, ast_ref, gate_ref, dskip, nw, y_ref, state_ref):
    xs = xs_ref[q]
    bm = bc_ref[q, :, :BC_WIDTH]
    cm = bc_ref[q, :, BC_WIDTH:]
    bm_t = bmt_ref[q]
    acum = acum_ref[q]
    as_t = ast_ref[q]
    li = lax.broadcasted_iota(jnp.int32, (CHUNK, CHUNK), 0)
    si = lax.broadcasted_iota(jnp.int32, (CHUNK, CHUNK), 1)
    causal = li >= si
    first_half = si < HEAD_DIM
    first_half_n = lax.broadcasted_iota(jnp.int32, (D_STATE, LANES), 1) < HEAD_DIM
    first_half_row = lax.broadcasted_iota(jnp.int32, (1, LANES), 1) < HEAD_DIM
    zeros_n = jnp.zeros((D_STATE, LANES), BF16)
    updates = []

    for g in range(N_GROUPS):
        gs = slice(g * D_STATE, (g + 1) * D_STATE)
        c_g = cm[:, gs]
        c2 = jnp.concatenate([c_g, c_g], axis=1)
        cb = lax.dot_general(c_g.astype(BF16), bm[:, gs].astype(BF16), (((1,), (1,)), ((), ())),
                             preferred_element_type=F32)
        bt_g = bm_t[gs, :]
        y_pairs = []
        for k in range(HEADS_PER_GROUP // 2):
            pair = (g * HEADS_PER_GROUP) // 2 + k
            x_pair = xs[:, pair * LANES:(pair + 1) * LANES]
            s_pair = state_ref[q, pair]
            s_b = s_pair.astype(BF16)
            heads = (2 * pair, 2 * pair + 1)
            a_l = [jnp.broadcast_to(acum[:, h:h + 1], (CHUNK, CHUNK)) for h in heads]
            cdec = (c2 * jnp.exp2(jnp.where(first_half, a_l[0], a_l[1]))).astype(BF16)
            rhs = [jnp.concatenate([x_pair, s_b, zeros_n], axis=0),
                   jnp.concatenate([x_pair, zeros_n, s_b], axis=0)]
            res, lhs_s, cds = [], [], []
            for i, h in enumerate(heads):
                a_s = as_t[h:h + 1, :]
                mix = cb * jnp.exp2(jnp.where(causal, a_l[i] - a_s, -jnp.inf))
                lhs = jnp.concatenate([mix.astype(BF16), cdec], axis=1)
                res.append(jnp.dot(lhs, rhs[i], preferred_element_type=F32))
                a_last = acum[CHUNK - 1:CHUNK, h:h + 1]
                lhs_s.append(bt_g * jnp.exp2(a_last - a_s))
                cds.append(jnp.exp2(a_last))
            y_pairs.append(jnp.where(first_half, res[0], res[1]))
            updates.append((pair, jnp.concatenate(lhs_s, axis=0).astype(BF16), x_pair,
                            jnp.where(first_half_row, cds[0], cds[1]) * s_pair))

        ns = slice(g * NORM_GROUP, (g + 1) * NORM_GROUP)
        y_g = jnp.concatenate(y_pairs, axis=1)
        y_g = (y_g + dskip[:, ns] * xs[:, ns].astype(F32)) * gate_ref[q, :, ns]
        y_ref[q, :, ns] = _rms(y_g, nw[:, ns]).astype(BF16)

    for pair, lhs_s, x_pair, decayed in updates:
        upd = jnp.dot(lhs_s, x_pair, preferred_element_type=F32)
        state_ref[q, pair] = decayed + jnp.where(first_half_n, upd[:D_STATE], upd[D_STATE:])


def _ssd_prompt_kernel(xs_ref, bc_ref, bmt_ref, acum_ref, ast_ref, gate_ref, dskip_ref, nw_ref,
                       y_ref, hfin_ref, state_ref):
    c = pl.program_id(1)

    @pl.when(c == 0)
    def _():
        state_ref[...] = jnp.zeros(state_ref.shape, F32)

    for q in range(SSD_SEQS):
        _ssd_chunk(q, xs_ref, bc_ref, bmt_ref, acum_ref, ast_ref, gate_ref, dskip_ref[...], nw_ref[...],
                   y_ref, state_ref)

    @pl.when(c == pl.num_programs(1) - 1)
    def _():
        for q in range(SSD_SEQS):
            for pair in range(HEAD_PAIRS):
                t = jnp.concatenate([state_ref[q, pair], jnp.zeros((LANES - D_STATE, LANES), F32)], axis=0).T
                hfin_ref[q, pair * LANES:(pair + 1) * LANES, :] = t[:, :D_STATE]


def _ssd_prompt(xs, bc, bmt, acum, ast, gate, dskip, nw, nseq, seq):
    nc = seq // CHUNK
    blk = lambda width, rows=CHUNK: pl.BlockSpec((SSD_SEQS, rows, width), lambda b, c: (b, c, 0))
    return pl.pallas_call(
        _ssd_prompt_kernel,
        grid=(nseq // SSD_SEQS, nc),
        in_specs=[blk(SSD_WIDTH), blk(2 * BC_WIDTH), blk(LANES, BC_WIDTH), blk(DT_PAD), blk(DT_PAD),
                  blk(SSD_WIDTH),
                  _const_spec((1, SSD_WIDTH)),
                  _const_spec((1, SSD_WIDTH))],
        out_specs=[blk(SSD_WIDTH),
                   pl.BlockSpec((SSD_SEQS, SSD_WIDTH, D_STATE), lambda b, c: (b, 0, 0))],
        out_shape=[jax.ShapeDtypeStruct((nseq, seq, SSD_WIDTH), BF16),
                   jax.ShapeDtypeStruct((nseq, SSD_WIDTH, D_STATE), F32)],
        scratch_shapes=[pltpu.VMEM((SSD_SEQS, HEAD_PAIRS, D_STATE, LANES), F32)],
        compiler_params=pltpu.CompilerParams(dimension_semantics=("arbitrary", "arbitrary"),
                                             vmem_limit_bytes=VMEM_LIMIT),
        name="ssd_prompt",
    )(xs, bc, bmt, acum, ast, gate, dskip, nw)


STEP_HEADS = 2


def _step_mix_kernel(xbc_ref, z_ref, dt_ref, v_ref, cs_ref, ps_ref, h_ref,
                     cw_ref, cb_ref, dtb_ref, alog_ref, dskip_ref, nw_ref, pw_ref, psc_ref,
                     hout_ref, y_ref, yp_ref, cso_ref, pso_ref,
                     xs_ref, xdt_ref, bt_ref, ct_ref, dect_ref, yt_ref, *, pos0):
    h = pl.program_id(0)
    nprev = POOL_MAX - 1

    @pl.when(h == 0)
    def _prepare():
        xnew = xbc_ref[...]
        conv = cb_ref[...] + xnew * cw_ref[CONV_W - 1:CONV_W, :]
        for k in range(CONV_W - 1):
            conv = conv + cs_ref[k] * cw_ref[k:k + 1, :]
        xbc = _silu(conv)
        xs = xbc[:, :SSD_WIDTH]
        xs_ref[...] = xs
        dt = _softplus(dt_ref[...] + dtb_ref[...])
        dect_ref[...] = jnp.exp(dt * (-jnp.exp(alog_ref[...]))).T
        dt_t = dt.T
        for hh in range(N_HEADS):
            blk = xs[:, (hh // 2) * LANES:(hh // 2 + 1) * LANES].T
            half = blk[(hh % 2) * HEAD_DIM:(hh % 2 + 1) * HEAD_DIM, :]
            xdt_ref[hh * HEAD_DIM:(hh + 1) * HEAD_DIM, :] = half * dt_t[hh:hh + 1, :]
        for i in range(BC_WIDTH // LANES):
            sl = slice(i * LANES, (i + 1) * LANES)
            bt_ref[sl, :] = xbc[:, SSD_WIDTH + i * LANES:SSD_WIDTH + (i + 1) * LANES].T
            ct_ref[sl, :] = xbc[:, SSD_WIDTH + BC_WIDTH + i * LANES:SSD_WIDTH + BC_WIDTH + (i + 1) * LANES].T
        for k in range(CONV_W - 2):
            cso_ref[k] = cs_ref[k + 1]
        cso_ref[CONV_W - 2] = xnew
        for k in range(nprev - 1):
            pso_ref[k] = ps_ref[k + 1]
        pso_ref[nprev - 1] = v_ref[...]

    for i in range(STEP_HEADS):
        head = h * STEP_HEADS + i
        g = lax.div(head, HEADS_PER_GROUP)
        hrow = pl.multiple_of(head * HEAD_DIM, HEAD_DIM)
        grow = pl.multiple_of(g * D_STATE, D_STATE)
        b_t = bt_ref[pl.ds(grow, D_STATE), :]
        c_t = ct_ref[pl.ds(grow, D_STATE), :]
        dec = dect_ref[pl.ds(head, 1), :]

        def per_p(p, carry, i=i, hrow=hrow, b_t=b_t, c_t=c_t, dec=dec):
            xrow = xdt_ref[pl.ds(hrow + p, 1), :]
            hn = h_ref[i, p] * dec + xrow * b_t
            hout_ref[i, p] = hn
            yt_ref[pl.ds(hrow + p, 1), :] = jnp.sum(hn * c_t, axis=0, keepdims=True)
            return carry

        lax.fori_loop(0, HEAD_DIM, per_p, 0, unroll=4)

    @pl.when(h == pl.num_programs(0) - 1)
    def _finish():
        y0 = jnp.concatenate([yt_ref[i * LANES:(i + 1) * LANES, :].T for i in range(SSD_WIDTH // LANES)], axis=1)
        y = _gate_and_norm(y0, xs_ref[...], _silu_half(z_ref[...]), dskip_ref[...], nw_ref[...])
        y_ref[...] = y.astype(BF16)
        v = v_ref[...]
        sums, cnts = [], []
        for gi, w in enumerate(POOL_WINDOWS):
            sl = slice(gi * POOL_GROUP_DIM, (gi + 1) * POOL_GROUP_DIM)
            acc = v[:, sl]
            for j in range(1, w):
                acc = acc + ps_ref[nprev - j][:, sl]
            sums.append(acc)
            cnts.append(float(min(pos0 + 1, w)))
        yp_ref[...] = _pool_project(sums, v, cnts, pw_ref, psc_ref[...]).astype(BF16)


def _step_mix(xbc, z, dt, v, cs_t, ps_t, h_t, cw, cb, dtb, alog, dskip, nw, pw, psc, pos0):
    n = xbc.shape[0]
    nprev = POOL_MAX - 1
    full = lambda shape: pl.BlockSpec(shape, lambda i: (0,) * len(shape))
    hblk = pl.BlockSpec((STEP_HEADS, HEAD_DIM, D_STATE, n), lambda i: (i, 0, 0, 0))
    return pl.pallas_call(
        functools.partial(_step_mix_kernel, pos0=pos0),
        grid=(N_HEADS // STEP_HEADS,),
        in_specs=[full((n, CONV_DIM)), full((n, SSD_WIDTH)), full((n, DT_PAD)), full((n, POOL_WIDTH)),
                  _const_spec((CONV_W - 1, n, CONV_DIM)), _const_spec((nprev, n, POOL_WIDTH)), hblk,
                  full((CONV_W, CONV_DIM)), full((1, CONV_DIM)), full((1, DT_PAD)), full((1, DT_PAD)),
                  full((1, SSD_WIDTH)), full((1, SSD_WIDTH)),
                  full((len(POOL_WINDOWS), POOL_GROUP_DIM, POOL_GROUP_DIM)), full((1, POOL_WIDTH))],
        out_specs=[hblk, full((n, SSD_WIDTH)), full((n, POOL_WIDTH)),
                   full((CONV_W - 1, n, CONV_DIM)), full((nprev, n, POOL_WIDTH))],
        out_shape=[jax.ShapeDtypeStruct(h_t.shape, F32),
                   jax.ShapeDtypeStruct((n, SSD_WIDTH), BF16),
                   jax.ShapeDtypeStruct((n, POOL_WIDTH), BF16),
                   jax.ShapeDtypeStruct((CONV_W - 1, n, CONV_DIM), F32),
                   jax.ShapeDtypeStruct((nprev, n, POOL_WIDTH), F32)],
        scratch_shapes=[pltpu.VMEM((n, SSD_WIDTH), F32),
                        pltpu.VMEM((SSD_WIDTH, n), F32),
                        pltpu.VMEM((BC_WIDTH, n), F32),
                        pltpu.VMEM((BC_WIDTH, n), F32),
                        pltpu.VMEM((DT_PAD, n), F32),
                        pltpu.VMEM((SSD_WIDTH, n), F32)],
        compiler_params=pltpu.CompilerParams(dimension_semantics=("arbitrary",),
                                             vmem_limit_bytes=VMEM_LIMIT),
        name="step_mix",
    )(xbc, z, dt, v, cs_t, ps_t, h_t, cw, cb, dtb, alog, dskip, nw, pw, psc)


FFN_CHUNK = 256


def _outffn_kernel(*refs, tm, decode):
    if decode:
        (x_ref, ys_ref, yp_ref, st_ref, wout_ref, n2_ref, wup_ref, fcw_ref, fcb_ref, wdn_ref, fn_ref,
         y_ref, sto_ref) = refs
    else:
        (x_ref, ys_ref, yp_ref, wout_ref, n2_ref, wup_ref, fcw_ref, fcb_ref, wdn_ref, fn_ref,
         y_ref, ust_ref, carry_ref) = refs
        j = pl.program_id(1)

        @pl.when(j == 0)
        def _():
            carry_ref[...] = jnp.zeros(carry_ref.shape, F32)

    ymix = jnp.concatenate([ys_ref[...], yp_ref[...]], axis=1)
    x1 = x_ref[...] + jnp.dot(ymix, wout_ref[...], preferred_element_type=F32)
    hn = _rms(x1, n2_ref[...]).astype(BF16)
    acc = x1
    nchunk = D_FF // FFN_CHUNK

    def col_slices(k):
        return [slice(base + k * FFN_CHUNK, base + (k + 1) * FFN_CHUNK) for base in (0, D_FF)]

    def up_proj(k):
        return [jnp.dot(hn, wup_ref[:, cs], preferred_element_type=F32) for cs in col_slices(k)]

    u_next = up_proj(0)
    for k in range(nchunk):
        u_cur = u_next
        if k + 1 < nchunk:
            u_next = up_proj(k + 1)
        halves = []
        for u, cs in zip(u_cur, col_slices(k)):
            if decode:
                prev2, prev1 = st_ref[:, 0, cs], st_ref[:, 1, cs]
                sto_ref[:, 0, cs] = prev1
                sto_ref[:, 1, cs] = u
            else:
                prev8 = carry_ref[:, cs]
                prev1 = _delay_rows(u, prev8, 1)
                prev2 = _delay_rows(u, prev8, 2)
                carry_ref[:, cs] = u[tm - SUBLANES:tm]
            halves.append(fcb_ref[:, cs] + prev2 * fcw_ref[0:1, cs] + prev1 * fcw_ref[1:2, cs]
                          + u * fcw_ref[2:3, cs])
        act = (_silu_half(halves[0]) * halves[1]).astype(BF16)
        acc = acc + jnp.dot(act, wdn_ref[k * FFN_CHUNK:(k + 1) * FFN_CHUNK, :], preferred_element_type=F32)
    y_ref[...] = _rms(acc, fn_ref[...])
    if not decode:
        @pl.when(j == pl.num_programs(1) - 1)
        def _():
            ust_ref[0] = carry_ref[SUBLANES - (FFN_CONV_W - 1):SUBLANES, :]


def _outffn_prompt(x2d, ys, yp, wout, n2, wup, fcw, fcb, wdn, fn, nseq, seq, tm):
    nj = seq // tm
    row = lambda b, j: (b * nj + j, 0)
    return pl.pallas_call(
        functools.partial(_outffn_kernel, tm=tm, decode=False),
        grid=(nseq, nj),
        in_specs=[pl.BlockSpec((tm, D_MODEL), row),
                  pl.BlockSpec((tm, SSD_WIDTH), row),
                  pl.BlockSpec((tm, POOL_WIDTH), row),
                  _const_spec((D_MIX, D_MODEL)),
                  _const_spec((1, D_MODEL)),
                  _const_spec((D_MODEL, 2 * D_FF)),
                  _const_spec((FFN_CONV_W, 2 * D_FF)),
                  _const_spec((1, 2 * D_FF)),
                  _const_spec((D_FF, D_MODEL)),
                  _const_spec((1, D_MODEL))],
        out_specs=[pl.BlockSpec((tm, D_MODEL), row),
                   pl.BlockSpec((1, FFN_CONV_W - 1, 2 * D_FF), lambda b, j: (b, 0, 0))],
        out_shape=[jax.ShapeDtypeStruct((nseq * seq, D_MODEL), F32),
                   jax.ShapeDtypeStruct((nseq, FFN_CONV_W - 1, 2 * D_FF), F32)],
        scratch_shapes=[pltpu.VMEM((SUBLANES, 2 * D_FF), F32)],
        compiler_params=pltpu.CompilerParams(dimension_semantics=("arbitrary", "arbitrary"),
                                             vmem_limit_bytes=VMEM_LIMIT),
        name="outffn_prompt",
    )(x2d, ys, yp, wout, n2, wup, fcw, fcb, wdn, fn)


def _outffn_decode(x2d, ys, yp, ffn_state, wout, n2, wup, fcw, fcb, wdn, fn):
    n = x2d.shape[0]
    full = lambda shape: pl.BlockSpec(shape, lambda i: (0,) * len(shape))
    return pl.pallas_call(
        functools.partial(_outffn_kernel, tm=n, decode=True),
        grid=(1,),
        in_specs=[full((n, D_MODEL)), full((n, SSD_WIDTH)), full((n, POOL_WIDTH)), full(ffn_state.shape),
                  _const_spec((D_MIX, D_MODEL)),
                  _const_spec((1, D_MODEL)),
                  _const_spec((D_MODEL, 2 * D_FF)),
                  _const_spec((FFN_CONV_W, 2 * D_FF)),
                  _const_spec((1, 2 * D_FF)),
                  _const_spec((D_FF, D_MODEL)),
                  _const_spec((1, D_MODEL))],
        out_specs=[full((n, D_MODEL)), full(ffn_state.shape)],
        out_shape=[jax.ShapeDtypeStruct((n, D_MODEL), F32),
                   jax.ShapeDtypeStruct(ffn_state.shape, F32)],
        compiler_params=pltpu.CompilerParams(dimension_semantics=("arbitrary",),
                                             vmem_limit_bytes=VMEM_LIMIT),
        name="outffn_decode",
    )(x2d, ys, yp, ffn_state, wout, n2, wup, fcw, fcb, wdn, fn)


def kernel(x_prompt, x_sample, state_ssm, state_conv, state_pool, state_ffn_conv, norm1_w, w_in, conv_w, conv_b,
           dt_bias, a_log, d_skip, ssd_norm_w, pool_w, pool_scale, w_out, norm2_w, w_up, ffn_conv_w, ffn_conv_b,
           w_down, final_norm_w):
    assert w_in.shape[0] == 1, "single-layer model"
    nseq, seq, _ = x_prompt.shape
    nsmp, one, _ = x_sample.shape
    assert one == 1 and seq % CHUNK == 0 and nsmp == LANES

    wi = w_in[0]
    w_dt = jnp.pad(wi[:, SSD_WIDTH + CONV_DIM:SSD_WIDTH + CONV_DIM + N_HEADS], ((0, 0), (0, DT_PAD - N_HEADS)))
    w_cat = jnp.concatenate([0.5 * wi[:, :SSD_WIDTH], wi[:, SSD_WIDTH:SSD_WIDTH + CONV_DIM],
                             wi[:, SSD_WIDTH + CONV_DIM + N_HEADS:], w_dt], axis=1).astype(BF16)
    n1 = norm1_w[0][None]
    cw, cb = conv_w[0], conv_b[0][None]
    dtb = jnp.pad(dt_bias[0], (0, DT_PAD - N_HEADS))[None]
    alog = jnp.pad(a_log[0], (0, DT_PAD - N_HEADS))[None]
    dskip = jnp.repeat(d_skip[0], HEAD_DIM)[None]
    nw = ssd_norm_w[0][None]
    pw = pool_w[0].astype(BF16)
    psc = pool_scale[0][None]
    wout = w_out[0].astype(BF16)
    n2 = norm2_w[0][None]
    wup = w_up[0].astype(BF16)
    gate_half = jnp.where(jnp.arange(2 * D_FF) < D_FF, 0.5, 1.0).astype(F32)
    fcw, fcb = ffn_conv_w[0] * gate_half, (ffn_conv_b[0] * gate_half)[None]
    wdn = w_down[0].astype(BF16)
    fn = final_norm_w[None]

    xp = x_prompt.reshape(nseq * seq, D_MODEL)
    xs_p, bc_p, bmt_p, acum_p, ast_p, gate_p, yp_p, ctail, vtail = _inproj_prompt(
        xp, n1, w_cat, 0.5 * cw, 0.5 * cb, dtb, alog, pw, psc, nseq, seq, tm=1024)
    per_seq = lambda a: a.reshape(nseq, -1, a.shape[-1])
    ys_p, hfin = _ssd_prompt(per_seq(xs_p), per_seq(bc_p), per_seq(bmt_p), per_seq(acum_p), per_seq(ast_p),
                             per_seq(gate_p), dskip, nw, nseq, seq)
    ys_p = ys_p.reshape(nseq * seq, SSD_WIDTH)
    y_p, ffn_p = _outffn_prompt(xp, ys_p, yp_p, wout, n2, wup, fcw, fcb, wdn, fn, nseq, seq, tm=256)

    xs_in = x_sample.reshape(nsmp, D_MODEL)
    z_s, xbc_s, v_s, dt_s = _inproj(xs_in, n1, w_cat, tm=nsmp)
    h_t = jnp.transpose(state_ssm[0], (1, 2, 3, 0))
    cs_t = jnp.transpose(state_conv[0], (1, 0, 2))
    ps_t = jnp.transpose(state_pool[0], (1, 0, 2))
    hnew_t, ys_s, yp_s, cs_new, ps_new = _step_mix(xbc_s, z_s, dt_s, v_s, cs_t, ps_t, h_t, cw, cb, dtb, alog,
                                                   dskip, nw, pw, psc, PAST_LEN)
    y_s, ffn_new = _outffn_decode(xs_in, ys_s, yp_s, state_ffn_conv[0], wout, n2, wup, fcw, fcb, wdn, fn)

    return (y_p.reshape(nseq, seq, D_MODEL),
            y_s.reshape(nsmp, 1, D_MODEL),
            hfin.reshape(nseq, N_HEADS, HEAD_DIM, D_STATE)[None],
            jnp.transpose(hnew_t, (3, 0, 1, 2))[None],
            ctail[:, SUBLANES - (CONV_W - 1):][None],
            jnp.transpose(cs_new, (1, 0, 2))[None],
            vtail[:, 2 * SUBLANES - (POOL_MAX - 1):][None],
            jnp.transpose(ps_new, (1, 0, 2))[None],
            ffn_p[None],
            ffn_new[None])
```

```python
import functools

import jax
import jax.numpy as jnp
from jax import lax
from jax.experimental import pallas as pl
from jax.experimental.pallas import tpu as pltpu

F32 = jnp.float32
BF16 = jnp.bfloat16

D_MODEL = 1024
SSD_WIDTH = 1536
HEAD_DIM = 64
N_HEADS = SSD_WIDTH // HEAD_DIM
N_GROUPS = 4
HEADS_PER_GROUP = N_HEADS // N_GROUPS
D_STATE = 64
BC_WIDTH = N_GROUPS * D_STATE
CONV_DIM = SSD_WIDTH + 2 * BC_WIDTH
CONV_W = 4
CHUNK = 128
POOL_WIDTH = 512
POOL_WINDOWS = (2, 4, 8, 16)
POOL_GROUP_DIM = POOL_WIDTH // len(POOL_WINDOWS)
POOL_MAX = max(POOL_WINDOWS)
D_MIX = SSD_WIDTH + POOL_WIDTH
D_FF = 2816
FFN_CONV_W = 3
NORM_GROUP = SSD_WIDTH // N_GROUPS
EPS = 1e-6
LOG2_E = 1.4426950408889634
PAST_LEN = 16384

LANES = 128
SUBLANES = 8
DT_PAD = LANES
VMEM_LIMIT = 56 * 1024 * 1024
HEAD_PAIRS = N_HEADS // 2

_Z0, _Z1 = 0, SSD_WIDTH
_X0, _X1 = _Z1, _Z1 + CONV_DIM
_V0, _V1 = _X1, _X1 + POOL_WIDTH
_T0, _T1 = _V1, _V1 + DT_PAD


def _rms(x, w):
    ms = jnp.mean(x * x, axis=-1, keepdims=True)
    return x * lax.rsqrt(ms + EPS) * w


def _silu_half(h):
    return h + h * jnp.tanh(h)


def _silu(x):
    return _silu_half(0.5 * x)


def _softplus(x):
    return jnp.maximum(x, 0.0) + jnp.log1p(jnp.exp(-jnp.abs(x)))


def _const_spec(shape):
    nd = len(shape)
    return pl.BlockSpec(shape, lambda *_: (0,) * nd, pipeline_mode=pl.Buffered(1))


def _delay_rows(u, prev8, s):
    if s == SUBLANES:
        return jnp.concatenate([prev8, u[:u.shape[0] - SUBLANES]], axis=0)
    r = pltpu.roll(u, s, axis=0)
    c = pltpu.roll(prev8, s, axis=0)
    row = lax.broadcasted_iota(jnp.int32, prev8.shape, 0)
    top = jnp.where(row < s, c, r[0:SUBLANES])
    return jnp.concatenate([top, r[SUBLANES:]], axis=0)


def _gate_and_norm(y, xs, gate, dskip, nw):
    y = (y + dskip * xs) * gate
    outs = []
    for g in range(N_GROUPS):
        sl = slice(g * NORM_GROUP, (g + 1) * NORM_GROUP)
        outs.append(_rms(y[:, sl], nw[:, sl]))
    return jnp.concatenate(outs, axis=1)


def _pool_project(win_sums, v, cnts, pw_ref, pscale):
    outs = []
    for g in range(len(POOL_WINDOWS)):
        sl = slice(g * POOL_GROUP_DIM, (g + 1) * POOL_GROUP_DIM)
        m = win_sums[g] * (1.0 / cnts[g]) - v[:, sl]
        outs.append(jnp.dot(m.astype(BF16), pw_ref[g], preferred_element_type=F32))
    return jnp.concatenate(outs, axis=1) * pscale


def _cumsum_rows(x):
    n = x.shape[0]
    row = lax.broadcasted_iota(jnp.int32, x.shape, 0)
    k = 1
    while k < n:
        x = x + jnp.where(row >= k, pltpu.roll(x, k, axis=0), 0.0)
        k *= 2
    return x


def _inproj_kernel(*refs, tm, prompt, pos0):
    if prompt:
        (x_ref, nw_ref, w_ref, cw_ref, cb_ref, dtb_ref, alog_ref, pw_ref, psc_ref,
         xs_ref, bc_ref, bmt_ref, acum_ref, ast_ref, gate_ref, yp_ref, ctail_ref, vtail_ref,
         cc_ref, vc_ref, s2c_ref, s4c_ref, s8c_ref) = refs
    else:
        x_ref, nw_ref, w_ref, z_ref, xbc_ref, v_ref, dt_ref = refs
    hn = _rms(x_ref[...], nw_ref[...]).astype(BF16)

    def mm(lo, hi):
        return jnp.dot(hn, w_ref[:, lo:hi], preferred_element_type=F32)

    if not prompt:
        z_ref[...] = mm(_Z0, _Z1)
        xbc_ref[...] = mm(_X0, _X1)
        v_ref[...] = mm(_V0, _V1)
        dt_ref[...] = mm(_T0, _T1)
        return

    j = pl.program_id(1)
    carries = (cc_ref, vc_ref, s2c_ref, s4c_ref, s8c_ref)

    @pl.when(j == 0)
    def _():
        for r in carries:
            r[...] = jnp.zeros(r.shape, F32)

    last = j == pl.num_programs(1) - 1

    def conv_stage(lo, hi):
        def fn(xbc):
            prev8 = cc_ref[:, lo:hi]
            conv = cb_ref[:, lo:hi] + xbc * cw_ref[CONV_W - 1:CONV_W, lo:hi]
            for k in range(CONV_W - 1):
                conv = conv + _delay_rows(xbc, prev8, CONV_W - 1 - k) * cw_ref[k:k + 1, lo:hi]
            cc_ref[:, lo:hi] = xbc[tm - SUBLANES:tm]
            act = _silu_half(conv)
            if hi <= SSD_WIDTH:
                xs_ref[:, lo:hi] = act.astype(BF16)
            else:
                bc_ref[:, lo - SSD_WIDTH:hi - SSD_WIDTH] = act
                for c in range(tm // CHUNK):
                    for i in range(BC_WIDTH // LANES):
                        blk = act[c * CHUNK:(c + 1) * CHUNK, i * LANES:(i + 1) * LANES]
                        bmt_ref[c * BC_WIDTH + i * LANES:c * BC_WIDTH + (i + 1) * LANES, :] = blk.T
        return fn

    def pool_stage(v):
        s2 = v + _delay_rows(v, vc_ref[...], 1)
        s4 = s2 + _delay_rows(s2, s2c_ref[...], 2)
        s8 = s4 + _delay_rows(s4, s4c_ref[...], 4)
        s16 = s8 + _delay_rows(s8, s8c_ref[...], 8)
        for r, val in zip(carries[1:], (v, s2, s4, s8)):
            r[...] = val[tm - SUBLANES:tm]
        pos = pos0 + j * tm + lax.broadcasted_iota(jnp.int32, (tm, 1), 0)
        sums, cnts = [], []
        for g, (w, s) in enumerate(zip(POOL_WINDOWS, (s2, s4, s8, s16))):
            sums.append(s[:, g * POOL_GROUP_DIM:(g + 1) * POOL_GROUP_DIM])
            cnts.append(jnp.minimum(pos + 1, w).astype(F32))
        yp_ref[...] = _pool_project(sums, v, cnts, pw_ref, psc_ref[...]).astype(BF16)

        @pl.when(last)
        def _():
            vtail_ref[0] = v[tm - 2 * SUBLANES:tm]

    def gate_stage(lo, hi):
        def fn(z):
            gate_ref[:, lo:hi] = _silu(z)
        return fn

    def dt_stage(dt_raw):
        dt = _softplus(dt_raw + dtb_ref[...])
        neg_a = -jnp.exp(alog_ref[...])
        for c in range(tm // CHUNK):
            rows = slice(c * CHUNK, (c + 1) * CHUNK)
            acum = _cumsum_rows(dt[rows] * neg_a) * LOG2_E
            acum_ref[rows, :] = acum
            ast_ref[rows, :] = acum.T - jnp.log2(dt[rows]).T

    stages = [((_Z0, _Z1), gate_stage(0, SSD_WIDTH)),
              ((_T0, _T1), dt_stage),
              ((_X0, _X0 + SSD_WIDTH), conv_stage(0, SSD_WIDTH)),
              ((_X0 + SSD_WIDTH, _X1), conv_stage(SSD_WIDTH, CONV_DIM)),
              ((_V0, _V1), pool_stage)]
    for cols, fn in stages:
        fn(mm(*cols))

    @pl.when(last)
    def _():
        ctail_ref[0] = cc_ref[...]


def _inproj(x2d, norm_w, w_cat, tm):
    m = x2d.shape[0]
    row = lambda i: (i, 0)
    return pl.pallas_call(
        functools.partial(_inproj_kernel, tm=tm, prompt=False, pos0=0),
        grid=(m // tm,),
        in_specs=[pl.BlockSpec((tm, D_MODEL), row),
                  _const_spec((1, D_MODEL)),
                  _const_spec((D_MODEL, _T1))],
        out_specs=[pl.BlockSpec((tm, SSD_WIDTH), row),
                   pl.BlockSpec((tm, CONV_DIM), row),
                   pl.BlockSpec((tm, POOL_WIDTH), row),
                   pl.BlockSpec((tm, DT_PAD), row)],
        out_shape=[jax.ShapeDtypeStruct((m, SSD_WIDTH), F32),
                   jax.ShapeDtypeStruct((m, CONV_DIM), F32),
                   jax.ShapeDtypeStruct((m, POOL_WIDTH), F32),
                   jax.ShapeDtypeStruct((m, DT_PAD), F32)],
        compiler_params=pltpu.CompilerParams(dimension_semantics=("arbitrary",),
                                             vmem_limit_bytes=VMEM_LIMIT),
        name="inproj",
    )(x2d, norm_w, w_cat)


def _inproj_prompt(x2d, norm_w, w_cat, cw, cb, dtb, alog, pw, psc, nseq, seq, tm):
    nj = seq // tm
    m = nseq * seq
    row = lambda b, j: (b * nj + j, 0)
    per_seq = lambda b, j: (b, 0, 0)
    return pl.pallas_call(
        functools.partial(_inproj_kernel, tm=tm, prompt=True, pos0=0),
        grid=(nseq, nj),
        in_specs=[pl.BlockSpec((tm, D_MODEL), row),
                  _const_spec((1, D_MODEL)),
                  _const_spec((D_MODEL, _T1)),
                  _const_spec((CONV_W, CONV_DIM)),
                  _const_spec((1, CONV_DIM)),
                  _const_spec((1, DT_PAD)),
                  _const_spec((1, DT_PAD)),
                  _const_spec((len(POOL_WINDOWS), POOL_GROUP_DIM, POOL_GROUP_DIM)),
                  _const_spec((1, POOL_WIDTH))],
        out_specs=[pl.BlockSpec((tm, SSD_WIDTH), row),
                   pl.BlockSpec((tm, 2 * BC_WIDTH), row),
                   pl.BlockSpec((2 * tm, LANES), row),
                   pl.BlockSpec((tm, DT_PAD), row),
                   pl.BlockSpec((tm, DT_PAD), row),
                   pl.BlockSpec((tm, SSD_WIDTH), row),
                   pl.BlockSpec((tm, POOL_WIDTH), row),
                   pl.BlockSpec((1, SUBLANES, CONV_DIM), per_seq),
                   pl.BlockSpec((1, 2 * SUBLANES, POOL_WIDTH), per_seq)],
        out_shape=[jax.ShapeDtypeStruct((m, SSD_WIDTH), BF16),
                   jax.ShapeDtypeStruct((m, 2 * BC_WIDTH), F32),
                   jax.ShapeDtypeStruct((2 * m, LANES), F32),
                   jax.ShapeDtypeStruct((m, DT_PAD), F32),
                   jax.ShapeDtypeStruct((m, DT_PAD), F32),
                   jax.ShapeDtypeStruct((m, SSD_WIDTH), F32),
                   jax.ShapeDtypeStruct((m, POOL_WIDTH), BF16),
                   jax.ShapeDtypeStruct((nseq, SUBLANES, CONV_DIM), F32),
                   jax.ShapeDtypeStruct((nseq, 2 * SUBLANES, POOL_WIDTH), F32)],
        scratch_shapes=[pltpu.VMEM((SUBLANES, CONV_DIM), F32),
                        pltpu.VMEM((SUBLANES, POOL_WIDTH), F32),
                        pltpu.VMEM((SUBLANES, POOL_WIDTH), F32),
                        pltpu.VMEM((SUBLANES, POOL_WIDTH), F32),
                        pltpu.VMEM((SUBLANES, POOL_WIDTH), F32)],
        compiler_params=pltpu.CompilerParams(dimension_semantics=("arbitrary", "arbitrary"),
                                             vmem_limit_bytes=VMEM_LIMIT),
        name="inproj_prompt",
    )(x2d, norm_w, w_cat, cw, cb, dtb, alog, pw, psc)


SSD_SEQS = 8


def _ssd_chunk(q, xs_ref, bc_ref, bmt_ref, acum_ref, ast_ref, gate_ref, dskip, nw, y_ref, state_ref):
    xs = xs_ref[q]
    bm = bc_ref[q, :, :BC_WIDTH]
    cm = bc_ref[q, :, BC_WIDTH:]
    bm_t = bmt_ref[q]
    acum = acum_ref[q]
    as_t = ast_ref[q]
    li = lax.broadcasted_iota(jnp.int32, (CHUNK, CHUNK), 0)
    si = lax.broadcasted_iota(jnp.int32, (CHUNK, CHUNK), 1)
    causal = li >= si
    first_half = si < HEAD_DIM
    first_half_n = lax.broadcasted_iota(jnp.int32, (D_STATE, LANES), 1) < HEAD_DIM
    first_half_row = lax.broadcasted_iota(jnp.int32, (1, LANES), 1) < HEAD_DIM
    zeros_n = jnp.zeros((D_STATE, LANES), BF16)
    updates = []

    for g in range(N_GROUPS):
        gs = slice(g * D_STATE, (g + 1) * D_STATE)
        c_g = cm[:, gs]
        c2 = jnp.concatenate([c_g, c_g], axis=1)
        cb = lax.dot_general(c_g.astype(BF16), bm[:, gs].astype(BF16), (((1,), (1,)), ((), ())),
                             preferred_element_type=F32)
        bt_g = bm_t[gs, :]
        y_pairs = []
        for k in range(HEADS_PER_GROUP // 2):
            pair = (g * HEADS_PER_GROUP) // 2 + k
            x_pair = xs[:, pair * LANES:(pair + 1) * LANES]
            s_pair = state_ref[q, pair]
            s_b = s_pair.astype(BF16)
            heads = (2 * pair, 2 * pair + 1)
            a_l = [jnp.broadcast_to(acum[:, h:h + 1], (CHUNK, CHUNK)) for h in heads]
            cdec = (c2 * jnp.exp2(jnp.where(first_half, a_l[0], a_l[1]))).astype(BF16)
            rhs = [jnp.concatenate([x_pair, s_b, zeros_n], axis=0),
                   jnp.concatenate([x_pair, zeros_n, s_b], axis=0)]
            res, lhs_s, cds = [], [], []
            for i, h in enumerate(heads):
                a_s = as_t[h:h + 1, :]
                mix = cb * jnp.exp2(jnp.where(causal, a_l[i] - a_s, -jnp.inf))
                lhs = jnp.concatenate([mix.astype(BF16), cdec], axis=1)
                res.append(jnp.dot(lhs, rhs[i], preferred_element_type=F32))
                a_last = acum[CHUNK - 1:CHUNK, h:h + 1]
                lhs_s.append(bt_g * jnp.exp2(a_last - a_s))
                cds.append(jnp.exp2(a_last))
            y_pairs.append(jnp.where(first_half, res[0], res[1]))
            updates.append((pair, jnp.concatenate(lhs_s, axis=0).astype(BF16), x_pair,
                            jnp.where(first_half_row, cds[0], cds[1]) * s_pair))

        ns = slice(g * NORM_GROUP, (g + 1) * NORM_GROUP)
        y_g = jnp.concatenate(y_pairs, axis=1)
        y_g = (y_g + dskip[:, ns] * xs[:, ns].astype(F32)) * gate_ref[q, :, ns]
        y_ref[q, :, ns] = _rms(y_g, nw[:, ns]).astype(BF16)

    for pair, lhs_s, x_pair, decayed in updates:
        upd = jnp.dot(lhs_s, x_pair, preferred_element_type=F32)
        state_ref[q, pair] = decayed + jnp.where(first_half_n, upd[:D_STATE], upd[D_STATE:])


def _ssd_prompt_kernel(xs_ref, bc_ref, bmt_ref, acum_ref, ast_ref, gate_ref, dskip_ref, nw_ref,
                       y_ref, hfin_ref, state_ref):
    c = pl.program_id(1)

    @pl.when(c == 0)
    def _():
        state_ref[...] = jnp.zeros(state_ref.shape, F32)

    for q in range(SSD_SEQS):
        _ssd_chunk(q, xs_ref, bc_ref, bmt_ref, acum_ref, ast_ref, gate_ref, dskip_ref[...], nw_ref[...],
                   y_ref, state_ref)

    @pl.when(c == pl.num_programs(1) - 1)
    def _():
        for q in range(SSD_SEQS):
            for pair in range(HEAD_PAIRS):
                t = jnp.concatenate([state_ref[q, pair], jnp.zeros((LANES - D_STATE, LANES), F32)], axis=0).T
                hfin_ref[q, pair * LANES:(pair + 1) * LANES, :] = t[:, :D_STATE]


def _ssd_prompt(xs, bc, bmt, acum, ast, gate, dskip, nw, nseq, seq):
    nc = seq // CHUNK
    blk = lambda width, rows=CHUNK: pl.BlockSpec((SSD_SEQS, rows, width), lambda b, c: (b, c, 0))
    return pl.pallas_call(
        _ssd_prompt_kernel,
        grid=(nseq // SSD_SEQS, nc),
        in_specs=[blk(SSD_WIDTH), blk(2 * BC_WIDTH), blk(LANES, BC_WIDTH), blk(DT_PAD), blk(DT_PAD),
                  blk(SSD_WIDTH),
                  _const_spec((1, SSD_WIDTH)),
                  _const_spec((1, SSD_WIDTH))],
        out_specs=[blk(SSD_WIDTH),
                   pl.BlockSpec((SSD_SEQS, SSD_WIDTH, D_STATE), lambda b, c: (b, 0, 0))],
        out_shape=[jax.ShapeDtypeStruct((nseq, seq, SSD_WIDTH), BF16),
                   jax.ShapeDtypeStruct((nseq, SSD_WIDTH, D_STATE), F32)],
        scratch_shapes=[pltpu.VMEM((SSD_SEQS, HEAD_PAIRS, D_STATE, LANES), F32)],
        compiler_params=pltpu.CompilerParams(dimension_semantics=("arbitrary", "arbitrary"),
                                             vmem_limit_bytes=VMEM_LIMIT),
        name="ssd_prompt",
    )(xs, bc, bmt, acum, ast, gate, dskip, nw)


STEP_HEADS = 4


def _step_mix_kernel(xbc_ref, z_ref, dt_ref, v_ref, cs_ref, ps_ref, h_ref,
                     cw_ref, cb_ref, dtb_ref, alog_ref, dskip_ref, nw_ref, pw_ref, psc_ref,
                     hout_ref, y_ref, yp_ref, cso_ref, pso_ref,
                     xs_ref, xdt_ref, bt_ref, ct_ref, dect_ref, yt_ref, *, pos0):
    h = pl.program_id(0)
    nprev = POOL_MAX - 1

    @pl.when(h == 0)
    def _prepare():
        xnew = xbc_ref[...]
        conv = cb_ref[...] + xnew * cw_ref[CONV_W - 1:CONV_W, :]
        for k in range(CONV_W - 1):
            conv = conv + cs_ref[k] * cw_ref[k:k + 1, :]
        xbc = _silu(conv)
        xs = xbc[:, :SSD_WIDTH]
        xs_ref[...] = xs
        dt = _softplus(dt_ref[...] + dtb_ref[...])
        dect_ref[...] = jnp.exp(dt * (-jnp.exp(alog_ref[...]))).T
        dt_t = dt.T
        for hh in range(N_HEADS):
            blk = xs[:, (hh // 2) * LANES:(hh // 2 + 1) * LANES].T
            half = blk[(hh % 2) * HEAD_DIM:(hh % 2 + 1) * HEAD_DIM, :]
            xdt_ref[hh * HEAD_DIM:(hh + 1) * HEAD_DIM, :] = half * dt_t[hh:hh + 1, :]
        for i in range(BC_WIDTH // LANES):
            sl = slice(i * LANES, (i + 1) * LANES)
            bt_ref[sl, :] = xbc[:, SSD_WIDTH + i * LANES:SSD_WIDTH + (i + 1) * LANES].T
            ct_ref[sl, :] = xbc[:, SSD_WIDTH + BC_WIDTH + i * LANES:SSD_WIDTH + BC_WIDTH + (i + 1) * LANES].T
        for k in range(CONV_W - 2):
            cso_ref[k] = cs_ref[k + 1]
        cso_ref[CONV_W - 2] = xnew
        for k in range(nprev - 1):
            pso_ref[k] = ps_ref[k + 1]
        pso_ref[nprev - 1] = v_ref[...]

    for i in range(STEP_HEADS):
        head = h * STEP_HEADS + i
        g = lax.div(head, HEADS_PER_GROUP)
        hrow = pl.multiple_of(head * HEAD_DIM, HEAD_DIM)
        grow = pl.multiple_of(g * D_STATE, D_STATE)
        b_t = bt_ref[pl.ds(grow, D_STATE), :]
        c_t = ct_ref[pl.ds(grow, D_STATE), :]
        dec = dect_ref[pl.ds(head, 1), :]

        def per_p(p, carry, i=i, hrow=hrow, b_t=b_t, c_t=c_t, dec=dec):
            xrow = xdt_ref[pl.ds(hrow + p, 1), :]
            hn = h_ref[i, p] * dec + xrow * b_t
            hout_ref[i, p] = hn
            yt_ref[pl.ds(hrow + p, 1), :] = jnp.sum(hn * c_t, axis=0, keepdims=True)
            return carry

        lax.fori_loop(0, HEAD_DIM, per_p, 0, unroll=4)

    @pl.when(h == pl.num_programs(0) - 1)
    def _finish():
        y0 = jnp.concatenate([yt_ref[i * LANES:(i + 1) * LANES, :].T for i in range(SSD_WIDTH // LANES)], axis=1)
        y = _gate_and_norm(y0, xs_ref[...], _silu(z_ref[...]), dskip_ref[...], nw_ref[...])
        y_ref[...] = y.astype(BF16)
        v = v_ref[...]
        sums, cnts = [], []
        for gi, w in enumerate(POOL_WINDOWS):
            sl = slice(gi * POOL_GROUP_DIM, (gi + 1) * POOL_GROUP_DIM)
            acc = v[:, sl]
            for j in range(1, w):
                acc = acc + ps_ref[nprev - j][:, sl]
            sums.append(acc)
            cnts.append(float(min(pos0 + 1, w)))
        yp_ref[...] = _pool_project(sums, v, cnts, pw_ref, psc_ref[...]).astype(BF16)


def _step_mix(xbc, z, dt, v, cs_t, ps_t, h_t, cw, cb, dtb, alog, dskip, nw, pw, psc, pos0):
    n = xbc.shape[0]
    nprev = POOL_MAX - 1
    full = lambda shape: pl.BlockSpec(shape, lambda i: (0,) * len(shape))
    hblk = pl.BlockSpec((STEP_HEADS, HEAD_DIM, D_STATE, n), lambda i: (i, 0, 0, 0))
    return pl.pallas_call(
        functools.partial(_step_mix_kernel, pos0=pos0),
        grid=(N_HEADS // STEP_HEADS,),
        in_specs=[full((n, CONV_DIM)), full((n, SSD_WIDTH)), full((n, DT_PAD)), full((n, POOL_WIDTH)),
                  _const_spec((CONV_W - 1, n, CONV_DIM)), _const_spec((nprev, n, POOL_WIDTH)), hblk,
                  full((CONV_W, CONV_DIM)), full((1, CONV_DIM)), full((1, DT_PAD)), full((1, DT_PAD)),
                  full((1, SSD_WIDTH)), full((1, SSD_WIDTH)),
                  full((len(POOL_WINDOWS), POOL_GROUP_DIM, POOL_GROUP_DIM)), full((1, POOL_WIDTH))],
        out_specs=[hblk, full((n, SSD_WIDTH)), full((n, POOL_WIDTH)),
                   full((CONV_W - 1, n, CONV_DIM)), full((nprev, n, POOL_WIDTH))],
        out_shape=[jax.ShapeDtypeStruct(h_t.shape, F32),
                   jax.ShapeDtypeStruct((n, SSD_WIDTH), BF16),
                   jax.ShapeDtypeStruct((n, POOL_WIDTH), BF16),
                   jax.ShapeDtypeStruct((CONV_W - 1, n, CONV_DIM), F32),
                   jax.ShapeDtypeStruct((nprev, n, POOL_WIDTH), F32)],
        scratch_shapes=[pltpu.VMEM((n, SSD_WIDTH), F32),
                        pltpu.VMEM((SSD_WIDTH, n), F32),
                        pltpu.VMEM((BC_WIDTH, n), F32),
                        pltpu.VMEM((BC_WIDTH, n), F32),
                        pltpu.VMEM((DT_PAD, n), F32),
                        pltpu.VMEM((SSD_WIDTH, n), F32)],
        compiler_params=pltpu.CompilerParams(dimension_semantics=("arbitrary",),
                                             vmem_limit_bytes=VMEM_LIMIT),
        name="step_mix",
    )(xbc, z, dt, v, cs_t, ps_t, h_t, cw, cb, dtb, alog, dskip, nw, pw, psc)


FFN_CHUNK = 256


def _outffn_kernel(*refs, tm, decode):
    if decode:
        (x_ref, ys_ref, yp_ref, st_ref, wout_ref, n2_ref, wup_ref, fcw_ref, fcb_ref, wdn_ref, fn_ref,
         y_ref, sto_ref) = refs
    else:
        (x_ref, ys_ref, yp_ref, wout_ref, n2_ref, wup_ref, fcw_ref, fcb_ref, wdn_ref, fn_ref,
         y_ref, ust_ref, carry_ref) = refs
        j = pl.program_id(1)

        @pl.when(j == 0)
        def _():
            carry_ref[...] = jnp.zeros(carry_ref.shape, F32)

    ymix = jnp.concatenate([ys_ref[...], yp_ref[...]], axis=1)
    x1 = x_ref[...] + jnp.dot(ymix, wout_ref[...], preferred_element_type=F32)
    hn = _rms(x1, n2_ref[...]).astype(BF16)
    acc = x1
    nchunk = D_FF // FFN_CHUNK

    def col_slices(k):
        return [slice(base + k * FFN_CHUNK, base + (k + 1) * FFN_CHUNK) for base in (0, D_FF)]

    def up_proj(k):
        return [jnp.dot(hn, wup_ref[:, cs], preferred_element_type=F32) for cs in col_slices(k)]

    u_next = up_proj(0)
    for k in range(nchunk):
        u_cur = u_next
        if k + 1 < nchunk:
            u_next = up_proj(k + 1)
        halves = []
        for u, cs in zip(u_cur, col_slices(k)):
            if decode:
                prev2, prev1 = st_ref[:, 0, cs], st_ref[:, 1, cs]
                sto_ref[:, 0, cs] = prev1
                sto_ref[:, 1, cs] = u
            else:
                prev8 = carry_ref[:, cs]
                prev1 = _delay_rows(u, prev8, 1)
                prev2 = _delay_rows(u, prev8, 2)
                carry_ref[:, cs] = u[tm - SUBLANES:tm]
            halves.append(fcb_ref[:, cs] + prev2 * fcw_ref[0:1, cs] + prev1 * fcw_ref[1:2, cs]
                          + u * fcw_ref[2:3, cs])
        act = (_silu_half(halves[0]) * halves[1]).astype(BF16)
        acc = acc + jnp.dot(act, wdn_ref[k * FFN_CHUNK:(k + 1) * FFN_CHUNK, :], preferred_element_type=F32)
    y_ref[...] = _rms(acc, fn_ref[...])
    if not decode:
        @pl.when(j == pl.num_programs(1) - 1)
        def _():
            ust_ref[0] = carry_ref[SUBLANES - (FFN_CONV_W - 1):SUBLANES, :]


def _outffn_prompt(x2d, ys, yp, wout, n2, wup, fcw, fcb, wdn, fn, nseq, seq, tm):
    nj = seq // tm
    row = lambda b, j: (b * nj + j, 0)
    return pl.pallas_call(
        functools.partial(_outffn_kernel, tm=tm, decode=False),
        grid=(nseq, nj),
        in_specs=[pl.BlockSpec((tm, D_MODEL), row),
                  pl.BlockSpec((tm, SSD_WIDTH), row),
                  pl.BlockSpec((tm, POOL_WIDTH), row),
                  _const_spec((D_MIX, D_MODEL)),
                  _const_spec((1, D_MODEL)),
                  _const_spec((D_MODEL, 2 * D_FF)),
                  _const_spec((FFN_CONV_W, 2 * D_FF)),
                  _const_spec((1, 2 * D_FF)),
                  _const_spec((D_FF, D_MODEL)),
                  _const_spec((1, D_MODEL))],
        out_specs=[pl.BlockSpec((tm, D_MODEL), row),
                   pl.BlockSpec((1, FFN_CONV_W - 1, 2 * D_FF), lambda b, j: (b, 0, 0))],
        out_shape=[jax.ShapeDtypeStruct((nseq * seq, D_MODEL), F32),
                   jax.ShapeDtypeStruct((nseq, FFN_CONV_W - 1, 2 * D_FF), F32)],
        scratch_shapes=[pltpu.VMEM((SUBLANES, 2 * D_FF), F32)],
        compiler_params=pltpu.CompilerParams(dimension_semantics=("arbitrary", "arbitrary"),
                                             vmem_limit_bytes=VMEM_LIMIT),
        name="outffn_prompt",
    )(x2d, ys, yp, wout, n2, wup, fcw, fcb, wdn, fn)


def _outffn_decode(x2d, ys, yp, ffn_state, wout, n2, wup, fcw, fcb, wdn, fn):
    n = x2d.shape[0]
    full = lambda shape: pl.BlockSpec(shape, lambda i: (0,) * len(shape))
    return pl.pallas_call(
        functools.partial(_outffn_kernel, tm=n, decode=True),
        grid=(1,),
        in_specs=[full((n, D_MODEL)), full((n, SSD_WIDTH)), full((n, POOL_WIDTH)), full(ffn_state.shape),
                  _const_spec((D_MIX, D_MODEL)),
                  _const_spec((1, D_MODEL)),
                  _const_spec((D_MODEL, 2 * D_FF)),
                  _const_spec((FFN_CONV_W, 2 * D_FF)),
                  _const_spec((1, 2 * D_FF)),
                  _const_spec((D_FF, D_MODEL)),
                  _const_spec((1, D_MODEL))],
        out_specs=[full((n, D_MODEL)), full(ffn_state.shape)],
        out_shape=[jax.ShapeDtypeStruct((n, D_MODEL), F32),
                   jax.ShapeDtypeStruct(ffn_state.shape, F32)],
        compiler_params=pltpu.CompilerParams(dimension_semantics=("arbitrary",),
                                             vmem_limit_bytes=VMEM_LIMIT),
        name="outffn_decode",
    )(x2d, ys, yp, ffn_state, wout, n2, wup, fcw, fcb, wdn, fn)


def kernel(x_prompt, x_sample, state_ssm, state_conv, state_pool, state_ffn_conv, norm1_w, w_in, conv_w, conv_b,
           dt_bias, a_log, d_skip, ssd_norm_w, pool_w, pool_scale, w_out, norm2_w, w_up, ffn_conv_w, ffn_conv_b,
           w_down, final_norm_w):
    assert w_in.shape[0] == 1, "single-layer model"
    nseq, seq, _ = x_prompt.shape
    nsmp, one, _ = x_sample.shape
    assert one == 1 and seq % CHUNK == 0 and nsmp == LANES

    wi = w_in[0]
    w_dt = jnp.pad(wi[:, SSD_WIDTH + CONV_DIM:SSD_WIDTH + CONV_DIM + N_HEADS], ((0, 0), (0, DT_PAD - N_HEADS)))
    w_cat = jnp.concatenate([wi[:, :SSD_WIDTH], wi[:, SSD_WIDTH:SSD_WIDTH + CONV_DIM],
                             wi[:, SSD_WIDTH + CONV_DIM + N_HEADS:], w_dt], axis=1).astype(BF16)
    n1 = norm1_w[0][None]
    cw, cb = conv_w[0], conv_b[0][None]
    dtb = jnp.pad(dt_bias[0], (0, DT_PAD - N_HEADS))[None]
    alog = jnp.pad(a_log[0], (0, DT_PAD - N_HEADS))[None]
    dskip = jnp.repeat(d_skip[0], HEAD_DIM)[None]
    nw = ssd_norm_w[0][None]
    pw = pool_w[0].astype(BF16)
    psc = pool_scale[0][None]
    wout = w_out[0].astype(BF16)
    n2 = norm2_w[0][None]
    wup = w_up[0].astype(BF16)
    gate_half = jnp.where(jnp.arange(2 * D_FF) < D_FF, 0.5, 1.0).astype(F32)
    fcw, fcb = ffn_conv_w[0] * gate_half, (ffn_conv_b[0] * gate_half)[None]
    wdn = w_down[0].astype(BF16)
    fn = final_norm_w[None]

    xp = x_prompt.reshape(nseq * seq, D_MODEL)
    xs_p, bc_p, bmt_p, acum_p, ast_p, gate_p, yp_p, ctail, vtail = _inproj_prompt(
        xp, n1, w_cat, 0.5 * cw, 0.5 * cb, dtb, alog, pw, psc, nseq, seq, tm=1024)
    per_seq = lambda a: a.reshape(nseq, -1, a.shape[-1])
    ys_p, hfin = _ssd_prompt(per_seq(xs_p), per_seq(bc_p), per_seq(bmt_p), per_seq(acum_p), per_seq(ast_p),
                             per_seq(gate_p), dskip, nw, nseq, seq)
    ys_p = ys_p.reshape(nseq * seq, SSD_WIDTH)
    y_p, ffn_p = _outffn_prompt(xp, ys_p, yp_p, wout, n2, wup, fcw, fcb, wdn, fn, nseq, seq, tm=256)

    xs_in = x_sample.reshape(nsmp, D_MODEL)
    z_s, xbc_s, v_s, dt_s = _inproj(xs_in, n1, w_cat, tm=nsmp)
    h_t = jnp.transpose(state_ssm[0], (1, 2, 3, 0))
    cs_t = jnp.transpose(state_conv[0], (1, 0, 2))
    ps_t = jnp.transpose(state_pool[0], (1, 0, 2))
    hnew_t, ys_s, yp_s, cs_new, ps_new = _step_mix(xbc_s, z_s, dt_s, v_s, cs_t, ps_t, h_t, cw, cb, dtb, alog,
                                                   dskip, nw, pw, psc, PAST_LEN)
    y_s, ffn_new = _outffn_decode(xs_in, ys_s, yp_s, state_ffn_conv[0], wout, n2, wup, fcw, fcb, wdn, fn)

    return (y_p.reshape(nseq, seq, D_MODEL),
            y_s.reshape(nsmp, 1, D_MODEL),
            hfin.reshape(nseq, N_HEADS, HEAD_DIM, D_STATE)[None],
            jnp.transpose(hnew_t, (3, 0, 1, 2))[None],
            ctail[:, SUBLANES - (CONV_W - 1):][None],
            jnp.transpose(cs_new, (1, 0, 2))[None],
            vtail[:, 2 * SUBLANES - (POOL_MAX - 1):][None],
            jnp.transpose(ps_new, (1, 0, 2))[None],
            ffn_p[None],
            ffn_new[None])
```

```python
import functools

import jax
import jax.numpy as jnp
from jax import lax
from jax.experimental import pallas as pl
from jax.experimental.pallas import tpu as pltpu

F32 = jnp.float32
BF16 = jnp.bfloat16

D_MODEL = 1024
SSD_WIDTH = 1536
HEAD_DIM = 64
N_HEADS = SSD_WIDTH // HEAD_DIM
N_GROUPS = 4
HEADS_PER_GROUP = N_HEADS // N_GROUPS
D_STATE = 64
BC_WIDTH = N_GROUPS * D_STATE
CONV_DIM = SSD_WIDTH + 2 * BC_WIDTH
CONV_W = 4
CHUNK = 128
POOL_WIDTH = 512
POOL_WINDOWS = (2, 4, 8, 16)
POOL_GROUP_DIM = POOL_WIDTH // len(POOL_WINDOWS)
POOL_MAX = max(POOL_WINDOWS)
D_MIX = SSD_WIDTH + POOL_WIDTH
D_FF = 2816
FFN_CONV_W = 3
NORM_GROUP = SSD_WIDTH // N_GROUPS
EPS = 1e-6
LOG2_E = 1.4426950408889634
PAST_LEN = 16384

LANES = 128
SUBLANES = 8
DT_PAD = LANES
VMEM_LIMIT = 56 * 1024 * 1024
HEAD_PAIRS = N_HEADS // 2

_Z0, _Z1 = 0, SSD_WIDTH
_X0, _X1 = _Z1, _Z1 + CONV_DIM
_V0, _V1 = _X1, _X1 + POOL_WIDTH
_T0, _T1 = _V1, _V1 + DT_PAD


def _rms(x, w):
    ms = jnp.mean(x * x, axis=-1, keepdims=True)
    return x * lax.rsqrt(ms + EPS) * w


def _silu_half(h):
    return h + h * jnp.tanh(h)


def _silu(x):
    return _silu_half(0.5 * x)


def _softplus(x):
    return jnp.maximum(x, 0.0) + jnp.log1p(jnp.exp(-jnp.abs(x)))


def _const_spec(shape):
    nd = len(shape)
    return pl.BlockSpec(shape, lambda *_: (0,) * nd, pipeline_mode=pl.Buffered(1))


def _delay_rows(u, prev8, s):
    if s == SUBLANES:
        return jnp.concatenate([prev8, u[:u.shape[0] - SUBLANES]], axis=0)
    r = pltpu.roll(u, s, axis=0)
    c = pltpu.roll(prev8, s, axis=0)
    row = lax.broadcasted_iota(jnp.int32, prev8.shape, 0)
    top = jnp.where(row < s, c, r[0:SUBLANES])
    return jnp.concatenate([top, r[SUBLANES:]], axis=0)


def _gate_and_norm(y, xs, gate, dskip, nw):
    y = (y + dskip * xs) * gate
    outs = []
    for g in range(N_GROUPS):
        sl = slice(g * NORM_GROUP, (g + 1) * NORM_GROUP)
        outs.append(_rms(y[:, sl], nw[:, sl]))
    return jnp.concatenate(outs, axis=1)


def _pool_project(win_sums, v, cnts, pw_ref, pscale):
    outs = []
    for g in range(len(POOL_WINDOWS)):
        sl = slice(g * POOL_GROUP_DIM, (g + 1) * POOL_GROUP_DIM)
        m = win_sums[g] * (1.0 / cnts[g]) - v[:, sl]
        outs.append(jnp.dot(m.astype(BF16), pw_ref[g], preferred_element_type=F32))
    return jnp.concatenate(outs, axis=1) * pscale


def _cumsum_rows(x):
    n = x.shape[0]
    row = lax.broadcasted_iota(jnp.int32, x.shape, 0)
    k = 1
    while k < n:
        x = x + jnp.where(row >= k, pltpu.roll(x, k, axis=0), 0.0)
        k *= 2
    return x


def _inproj_kernel(*refs, tm, prompt, pos0):
    if prompt:
        (x_ref, nw_ref, w_ref, cw_ref, cb_ref, dtb_ref, alog_ref, pw_ref, psc_ref,
         xs_ref, bc_ref, bmt_ref, acum_ref, ast_ref, gate_ref, yp_ref, ctail_ref, vtail_ref,
         cc_ref, vc_ref, s2c_ref, s4c_ref, s8c_ref) = refs
    else:
        x_ref, nw_ref, w_ref, z_ref, xbc_ref, v_ref, dt_ref = refs
    hn = _rms(x_ref[...], nw_ref[...]).astype(BF16)

    def mm(lo, hi):
        return jnp.dot(hn, w_ref[:, lo:hi], preferred_element_type=F32)

    if not prompt:
        z_ref[...] = mm(_Z0, _Z1)
        xbc_ref[...] = mm(_X0, _X1)
        v_ref[...] = mm(_V0, _V1)
        dt_ref[...] = mm(_T0, _T1)
        return

    j = pl.program_id(1)
    carries = (cc_ref, vc_ref, s2c_ref, s4c_ref, s8c_ref)

    @pl.when(j == 0)
    def _():
        for r in carries:
            r[...] = jnp.zeros(r.shape, F32)

    last = j == pl.num_programs(1) - 1

    def conv_stage(lo, hi):
        def fn(xbc):
            prev8 = cc_ref[:, lo:hi]
            conv = cb_ref[:, lo:hi] + xbc * cw_ref[CONV_W - 1:CONV_W, lo:hi]
            for k in range(CONV_W - 1):
                conv = conv + _delay_rows(xbc, prev8, CONV_W - 1 - k) * cw_ref[k:k + 1, lo:hi]
            cc_ref[:, lo:hi] = xbc[tm - SUBLANES:tm]
            act = _silu_half(conv)
            if hi <= SSD_WIDTH:
                xs_ref[:, lo:hi] = act.astype(BF16)
            else:
                bc_ref[:, lo - SSD_WIDTH:hi - SSD_WIDTH] = act
                for c in range(tm // CHUNK):
                    for i in range(BC_WIDTH // LANES):
                        blk = act[c * CHUNK:(c + 1) * CHUNK, i * LANES:(i + 1) * LANES]
                        bmt_ref[c * BC_WIDTH + i * LANES:c * BC_WIDTH + (i + 1) * LANES, :] = blk.T
        return fn

    def pool_stage(v):
        s2 = v + _delay_rows(v, vc_ref[...], 1)
        s4 = s2 + _delay_rows(s2, s2c_ref[...], 2)
        s8 = s4 + _delay_rows(s4, s4c_ref[...], 4)
        s16 = s8 + _delay_rows(s8, s8c_ref[...], 8)
        for r, val in zip(carries[1:], (v, s2, s4, s8)):
            r[...] = val[tm - SUBLANES:tm]
        pos = pos0 + j * tm + lax.broadcasted_iota(jnp.int32, (tm, 1), 0)
        sums, cnts = [], []
        for g, (w, s) in enumerate(zip(POOL_WINDOWS, (s2, s4, s8, s16))):
            sums.append(s[:, g * POOL_GROUP_DIM:(g + 1) * POOL_GROUP_DIM])
            cnts.append(jnp.minimum(pos + 1, w).astype(F32))
        yp_ref[...] = _pool_project(sums, v, cnts, pw_ref, psc_ref[...]).astype(BF16)

        @pl.when(last)
        def _():
            vtail_ref[0] = v[tm - 2 * SUBLANES:tm]

    def gate_stage(lo, hi):
        def fn(z):
            gate_ref[:, lo:hi] = _silu(z).astype(gate_ref.dtype)
        return fn

    def dt_stage(dt_raw):
        dt = _softplus(dt_raw + dtb_ref[...])
        neg_a = -jnp.exp(alog_ref[...])
        for c in range(tm // CHUNK):
            rows = slice(c * CHUNK, (c + 1) * CHUNK)
            acum = _cumsum_rows(dt[rows] * neg_a) * LOG2_E
            acum_ref[rows, :] = acum
            ast_ref[rows, :] = acum.T - jnp.log2(dt[rows]).T

    stages = [((_Z0, _Z1), gate_stage(0, SSD_WIDTH)),
              ((_T0, _T1), dt_stage),
              ((_X0, _X0 + SSD_WIDTH), conv_stage(0, SSD_WIDTH)),
              ((_X0 + SSD_WIDTH, _X1), conv_stage(SSD_WIDTH, CONV_DIM)),
              ((_V0, _V1), pool_stage)]
    for cols, fn in stages:
        fn(mm(*cols))

    @pl.when(last)
    def _():
        ctail_ref[0] = cc_ref[...]


def _inproj(x2d, norm_w, w_cat, tm):
    m = x2d.shape[0]
    row = lambda i: (i, 0)
    return pl.pallas_call(
        functools.partial(_inproj_kernel, tm=tm, prompt=False, pos0=0),
        grid=(m // tm,),
        in_specs=[pl.BlockSpec((tm, D_MODEL), row),
                  _const_spec((1, D_MODEL)),
                  _const_spec((D_MODEL, _T1))],
        out_specs=[pl.BlockSpec((tm, SSD_WIDTH), row),
                   pl.BlockSpec((tm, CONV_DIM), row),
                   pl.BlockSpec((tm, POOL_WIDTH), row),
                   pl.BlockSpec((tm, DT_PAD), row)],
        out_shape=[jax.ShapeDtypeStruct((m, SSD_WIDTH), F32),
                   jax.ShapeDtypeStruct((m, CONV_DIM), F32),
                   jax.ShapeDtypeStruct((m, POOL_WIDTH), F32),
                   jax.ShapeDtypeStruct((m, DT_PAD), F32)],
        compiler_params=pltpu.CompilerParams(dimension_semantics=("arbitrary",),
                                             vmem_limit_bytes=VMEM_LIMIT),
        name="inproj",
    )(x2d, norm_w, w_cat)


def _inproj_prompt(x2d, norm_w, w_cat, cw, cb, dtb, alog, pw, psc, nseq, seq, tm):
    nj = seq // tm
    m = nseq * seq
    row = lambda b, j: (b * nj + j, 0)
    per_seq = lambda b, j: (b, 0, 0)
    return pl.pallas_call(
        functools.partial(_inproj_kernel, tm=tm, prompt=True, pos0=0),
        grid=(nseq, nj),
        in_specs=[pl.BlockSpec((tm, D_MODEL), row),
                  _const_spec((1, D_MODEL)),
                  _const_spec((D_MODEL, _T1)),
                  _const_spec((CONV_W, CONV_DIM)),
                  _const_spec((1, CONV_DIM)),
                  _const_spec((1, DT_PAD)),
                  _const_spec((1, DT_PAD)),
                  _const_spec((len(POOL_WINDOWS), POOL_GROUP_DIM, POOL_GROUP_DIM)),
                  _const_spec((1, POOL_WIDTH))],
        out_specs=[pl.BlockSpec((tm, SSD_WIDTH), row),
                   pl.BlockSpec((tm, 2 * BC_WIDTH), row),
                   pl.BlockSpec((2 * tm, LANES), row),
                   pl.BlockSpec((tm, DT_PAD), row),
                   pl.BlockSpec((tm, DT_PAD), row),
                   pl.BlockSpec((tm, SSD_WIDTH), row),
                   pl.BlockSpec((tm, POOL_WIDTH), row),
                   pl.BlockSpec((1, SUBLANES, CONV_DIM), per_seq),
                   pl.BlockSpec((1, 2 * SUBLANES, POOL_WIDTH), per_seq)],
        out_shape=[jax.ShapeDtypeStruct((m, SSD_WIDTH), BF16),
                   jax.ShapeDtypeStruct((m, 2 * BC_WIDTH), F32),
                   jax.ShapeDtypeStruct((2 * m, LANES), F32),
                   jax.ShapeDtypeStruct((m, DT_PAD), F32),
                   jax.ShapeDtypeStruct((m, DT_PAD), F32),
                   jax.ShapeDtypeStruct((m, SSD_WIDTH), BF16),
                   jax.ShapeDtypeStruct((m, POOL_WIDTH), BF16),
                   jax.ShapeDtypeStruct((nseq, SUBLANES, CONV_DIM), F32),
                   jax.ShapeDtypeStruct((nseq, 2 * SUBLANES, POOL_WIDTH), F32)],
        scratch_shapes=[pltpu.VMEM((SUBLANES, CONV_DIM), F32),
                        pltpu.VMEM((SUBLANES, POOL_WIDTH), F32),
                        pltpu.VMEM((SUBLANES, POOL_WIDTH), F32),
                        pltpu.VMEM((SUBLANES, POOL_WIDTH), F32),
                        pltpu.VMEM((SUBLANES, POOL_WIDTH), F32)],
        compiler_params=pltpu.CompilerParams(dimension_semantics=("arbitrary", "arbitrary"),
                                             vmem_limit_bytes=VMEM_LIMIT),
        name="inproj_prompt",
    )(x2d, norm_w, w_cat, cw, cb, dtb, alog, pw, psc)


SSD_SEQS = 8


def _ssd_chunk(q, xs_ref, bc_ref, bmt_ref, acum_ref, ast_ref, gate_ref, dskip, nw, y_ref, state_ref):
    xs = xs_ref[q]
    bm = bc_ref[q, :, :BC_WIDTH]
    cm = bc_ref[q, :, BC_WIDTH:]
    bm_t = bmt_ref[q]
    acum = acum_ref[q]
    as_t = ast_ref[q]
    li = lax.broadcasted_iota(jnp.int32, (CHUNK, CHUNK), 0)
    si = lax.broadcasted_iota(jnp.int32, (CHUNK, CHUNK), 1)
    causal = li >= si
    first_half = si < HEAD_DIM
    first_half_n = lax.broadcasted_iota(jnp.int32, (D_STATE, LANES), 1) < HEAD_DIM
    first_half_row = lax.broadcasted_iota(jnp.int32, (1, LANES), 1) < HEAD_DIM
    zeros_n = jnp.zeros((D_STATE, LANES), BF16)
    updates = []

    for g in range(N_GROUPS):
        gs = slice(g * D_STATE, (g + 1) * D_STATE)
        c_g = cm[:, gs]
        c2 = jnp.concatenate([c_g, c_g], axis=1)
        cb = lax.dot_general(c_g.astype(BF16), bm[:, gs].astype(BF16), (((1,), (1,)), ((), ())),
                             preferred_element_type=F32)
        bt_g = bm_t[gs, :]
        y_pairs = []
        for k in range(HEADS_PER_GROUP // 2):
            pair = (g * HEADS_PER_GROUP) // 2 + k
            x_pair = xs[:, pair * LANES:(pair + 1) * LANES]
            s_pair = state_ref[q, pair]
            s_b = s_pair.astype(BF16)
            heads = (2 * pair, 2 * pair + 1)
            a_l = [jnp.broadcast_to(acum[:, h:h + 1], (CHUNK, CHUNK)) for h in heads]
            cdec = (c2 * jnp.exp2(jnp.where(first_half, a_l[0], a_l[1]))).astype(BF16)
            rhs = [jnp.concatenate([x_pair, s_b, zeros_n], axis=0),
                   jnp.concatenate([x_pair, zeros_n, s_b], axis=0)]
            res, lhs_s, cds = [], [], []
            for i, h in enumerate(heads):
                a_s = as_t[h:h + 1, :]
                mix = cb * jnp.exp2(jnp.where(causal, a_l[i] - a_s, -jnp.inf))
                lhs = jnp.concatenate([mix.astype(BF16), cdec], axis=1)
                res.append(jnp.dot(lhs, rhs[i], preferred_element_type=F32))
                a_last = acum[CHUNK - 1:CHUNK, h:h + 1]
                lhs_s.append(bt_g * jnp.exp2(a_last - a_s))
                cds.append(jnp.exp2(a_last))
            y_pairs.append(jnp.where(first_half, res[0], res[1]))
            updates.append((pair, jnp.concatenate(lhs_s, axis=0).astype(BF16), x_pair,
                            jnp.where(first_half_row, cds[0], cds[1]) * s_pair))

        ns = slice(g * NORM_GROUP, (g + 1) * NORM_GROUP)
        y_g = jnp.concatenate(y_pairs, axis=1)
        y_g = (y_g + dskip[:, ns] * xs[:, ns].astype(F32)) * gate_ref[q, :, ns].astype(F32)
        y_ref[q, :, ns] = _rms(y_g, nw[:, ns]).astype(BF16)

    for pair, lhs_s, x_pair, decayed in updates:
        upd = jnp.dot(lhs_s, x_pair, preferred_element_type=F32)
        state_ref[q, pair] = decayed + jnp.where(first_half_n, upd[:D_STATE], upd[D_STATE:])


def _ssd_prompt_kernel(xs_ref, bc_ref, bmt_ref, acum_ref, ast_ref, gate_ref, dskip_ref, nw_ref,
                       y_ref, hfin_ref, state_ref):
    c = pl.program_id(1)

    @pl.when(c == 0)
    def _():
        state_ref[...] = jnp.zeros(state_ref.shape, F32)

    for q in range(SSD_SEQS):
        _ssd_chunk(q, xs_ref, bc_ref, bmt_ref, acum_ref, ast_ref, gate_ref, dskip_ref[...], nw_ref[...],
                   y_ref, state_ref)

    @pl.when(c == pl.num_programs(1) - 1)
    def _():
        for q in range(SSD_SEQS):
            for pair in range(HEAD_PAIRS):
                t = jnp.concatenate([state_ref[q, pair], jnp.zeros((LANES - D_STATE, LANES), F32)], axis=0).T
                hfin_ref[q, pair * LANES:(pair + 1) * LANES, :] = t[:, :D_STATE]


def _ssd_prompt(xs, bc, bmt, acum, ast, gate, dskip, nw, nseq, seq):
    nc = seq // CHUNK
    blk = lambda width, rows=CHUNK: pl.BlockSpec((SSD_SEQS, rows, width), lambda b, c: (b, c, 0))
    return pl.pallas_call(
        _ssd_prompt_kernel,
        grid=(nseq // SSD_SEQS, nc),
        in_specs=[blk(SSD_WIDTH), blk(2 * BC_WIDTH), blk(LANES, BC_WIDTH), blk(DT_PAD), blk(DT_PAD),
                  blk(SSD_WIDTH),
                  _const_spec((1, SSD_WIDTH)),
                  _const_spec((1, SSD_WIDTH))],
        out_specs=[blk(SSD_WIDTH),
                   pl.BlockSpec((SSD_SEQS, SSD_WIDTH, D_STATE), lambda b, c: (b, 0, 0))],
        out_shape=[jax.ShapeDtypeStruct((nseq, seq, SSD_WIDTH), BF16),
                   jax.ShapeDtypeStruct((nseq, SSD_WIDTH, D_STATE), F32)],
        scratch_shapes=[pltpu.VMEM((SSD_SEQS, HEAD_PAIRS, D_STATE, LANES), F32)],
        compiler_params=pltpu.CompilerParams(dimension_semantics=("arbitrary", "arbitrary"),
                                             vmem_limit_bytes=VMEM_LIMIT),
        name="ssd_prompt",
    )(xs, bc, bmt, acum, ast, gate, dskip, nw)


STEP_HEADS = 4


def _step_mix_kernel(xbc_ref, z_ref, dt_ref, v_ref, cs_ref, ps_ref, h_ref,
                     cw_ref, cb_ref, dtb_ref, alog_ref, dskip_ref, nw_ref, pw_ref, psc_ref,
                     hout_ref, y_ref, yp_ref, cso_ref, pso_ref,
                     xs_ref, xdt_ref, bt_ref, ct_ref, dect_ref, yt_ref, *, pos0):
    h = pl.program_id(0)
    nprev = POOL_MAX - 1

    @pl.when(h == 0)
    def _prepare():
        xnew = xbc_ref[...]
        conv = cb_ref[...] + xnew * cw_ref[CONV_W - 1:CONV_W, :]
        for k in range(CONV_W - 1):
            conv = conv + cs_ref[k] * cw_ref[k:k + 1, :]
        xbc = _silu(conv)
        xs = xbc[:, :SSD_WIDTH]
        xs_ref[...] = xs
        dt = _softplus(dt_ref[...] + dtb_ref[...])
        dect_ref[...] = jnp.exp(dt * (-jnp.exp(alog_ref[...]))).T
        dt_t = dt.T
        for hh in range(N_HEADS):
            blk = xs[:, (hh // 2) * LANES:(hh // 2 + 1) * LANES].T
            half = blk[(hh % 2) * HEAD_DIM:(hh % 2 + 1) * HEAD_DIM, :]
            xdt_ref[hh * HEAD_DIM:(hh + 1) * HEAD_DIM, :] = half * dt_t[hh:hh + 1, :]
        for i in range(BC_WIDTH // LANES):
            sl = slice(i * LANES, (i + 1) * LANES)
            bt_ref[sl, :] = xbc[:, SSD_WIDTH + i * LANES:SSD_WIDTH + (i + 1) * LANES].T
            ct_ref[sl, :] = xbc[:, SSD_WIDTH + BC_WIDTH + i * LANES:SSD_WIDTH + BC_WIDTH + (i + 1) * LANES].T
        for k in range(CONV_W - 2):
            cso_ref[k] = cs_ref[k + 1]
        cso_ref[CONV_W - 2] = xnew
        for k in range(nprev - 1):
            pso_ref[k] = ps_ref[k + 1]
        pso_ref[nprev - 1] = v_ref[...]

    for i in range(STEP_HEADS):
        head = h * STEP_HEADS + i
        g = lax.div(head, HEADS_PER_GROUP)
        hrow = pl.multiple_of(head * HEAD_DIM, HEAD_DIM)
        grow = pl.multiple_of(g * D_STATE, D_STATE)
        b_t = bt_ref[pl.ds(grow, D_STATE), :]
        c_t = ct_ref[pl.ds(grow, D_STATE), :]
        dec = dect_ref[pl.ds(head, 1), :]

        def per_p(p, carry, i=i, hrow=hrow, b_t=b_t, c_t=c_t, dec=dec):
            xrow = xdt_ref[pl.ds(hrow + p, 1), :]
            hn = h_ref[i, p] * dec + xrow * b_t
            hout_ref[i, p] = hn
            yt_ref[pl.ds(hrow + p, 1), :] = jnp.sum(hn * c_t, axis=0, keepdims=True)
            return carry

        lax.fori_loop(0, HEAD_DIM, per_p, 0, unroll=4)

    @pl.when(h == pl.num_programs(0) - 1)
    def _finish():
        y0 = jnp.concatenate([yt_ref[i * LANES:(i + 1) * LANES, :].T for i in range(SSD_WIDTH // LANES)], axis=1)
        y = _gate_and_norm(y0, xs_ref[...], _silu(z_ref[...]), dskip_ref[...], nw_ref[...])
        y_ref[...] = y.astype(BF16)
        v = v_ref[...]
        sums, cnts = [], []
        for gi, w in enumerate(POOL_WINDOWS):
            sl = slice(gi * POOL_GROUP_DIM, (gi + 1) * POOL_GROUP_DIM)
            acc = v[:, sl]
            for j in range(1, w):
                acc = acc + ps_ref[nprev - j][:, sl]
            sums.append(acc)
            cnts.append(float(min(pos0 + 1, w)))
        yp_ref[...] = _pool_project(sums, v, cnts, pw_ref, psc_ref[...]).astype(BF16)


def _step_mix(xbc, z, dt, v, cs_t, ps_t, h_t, cw, cb, dtb, alog, dskip, nw, pw, psc, pos0):
    n = xbc.shape[0]
    nprev = POOL_MAX - 1
    full = lambda shape: pl.BlockSpec(shape, lambda i: (0,) * len(shape))
    hblk = pl.BlockSpec((STEP_HEADS, HEAD_DIM, D_STATE, n), lambda i: (i, 0, 0, 0))
    return pl.pallas_call(
        functools.partial(_step_mix_kernel, pos0=pos0),
        grid=(N_HEADS // STEP_HEADS,),
        in_specs=[full((n, CONV_DIM)), full((n, SSD_WIDTH)), full((n, DT_PAD)), full((n, POOL_WIDTH)),
                  _const_spec((CONV_W - 1, n, CONV_DIM)), _const_spec((nprev, n, POOL_WIDTH)), hblk,
                  full((CONV_W, CONV_DIM)), full((1, CONV_DIM)), full((1, DT_PAD)), full((1, DT_PAD)),
                  full((1, SSD_WIDTH)), full((1, SSD_WIDTH)),
                  full((len(POOL_WINDOWS), POOL_GROUP_DIM, POOL_GROUP_DIM)), full((1, POOL_WIDTH))],
        out_specs=[hblk, full((n, SSD_WIDTH)), full((n, POOL_WIDTH)),
                   full((CONV_W - 1, n, CONV_DIM)), full((nprev, n, POOL_WIDTH))],
        out_shape=[jax.ShapeDtypeStruct(h_t.shape, F32),
                   jax.ShapeDtypeStruct((n, SSD_WIDTH), BF16),
                   jax.ShapeDtypeStruct((n, POOL_WIDTH), BF16),
                   jax.ShapeDtypeStruct((CONV_W - 1, n, CONV_DIM), F32),
                   jax.ShapeDtypeStruct((nprev, n, POOL_WIDTH), F32)],
        scratch_shapes=[pltpu.VMEM((n, SSD_WIDTH), F32),
                        pltpu.VMEM((SSD_WIDTH, n), F32),
                        pltpu.VMEM((BC_WIDTH, n), F32),
                        pltpu.VMEM((BC_WIDTH, n), F32),
                        pltpu.VMEM((DT_PAD, n), F32),
                        pltpu.VMEM((SSD_WIDTH, n), F32)],
        compiler_params=pltpu.CompilerParams(dimension_semantics=("arbitrary",),
                                             vmem_limit_bytes=VMEM_LIMIT),
        name="step_mix",
    )(xbc, z, dt, v, cs_t, ps_t, h_t, cw, cb, dtb, alog, dskip, nw, pw, psc)


FFN_CHUNK = 256


def _outffn_kernel(*refs, tm, decode):
    if decode:
        (x_ref, ys_ref, yp_ref, st_ref, wout_ref, n2_ref, wup_ref, fcw_ref, fcb_ref, wdn_ref, fn_ref,
         y_ref, sto_ref) = refs
    else:
        (x_ref, ys_ref, yp_ref, wout_ref, n2_ref, wup_ref, fcw_ref, fcb_ref, wdn_ref, fn_ref,
         y_ref, ust_ref, carry_ref) = refs
        j = pl.program_id(1)

        @pl.when(j == 0)
        def _():
            carry_ref[...] = jnp.zeros(carry_ref.shape, F32)

    ymix = jnp.concatenate([ys_ref[...], yp_ref[...]], axis=1)
    x1 = x_ref[...] + jnp.dot(ymix, wout_ref[...], preferred_element_type=F32)
    hn = _rms(x1, n2_ref[...]).astype(BF16)
    acc = x1
    nchunk = D_FF // FFN_CHUNK

    def col_slices(k):
        return [slice(base + k * FFN_CHUNK, base + (k + 1) * FFN_CHUNK) for base in (0, D_FF)]

    def up_proj(k):
        return [jnp.dot(hn, wup_ref[:, cs], preferred_element_type=F32) for cs in col_slices(k)]

    u_next = up_proj(0)
    for k in range(nchunk):
        u_cur = u_next
        if k + 1 < nchunk:
            u_next = up_proj(k + 1)
        halves = []
        for u, cs in zip(u_cur, col_slices(k)):
            if decode:
                prev2, prev1 = st_ref[:, 0, cs], st_ref[:, 1, cs]
                sto_ref[:, 0, cs] = prev1
                sto_ref[:, 1, cs] = u
            else:
                prev8 = carry_ref[:, cs]
                prev1 = _delay_rows(u, prev8, 1)
                prev2 = _delay_rows(u, prev8, 2)
                carry_ref[:, cs] = u[tm - SUBLANES:tm]
            halves.append(fcb_ref[:, cs] + prev2 * fcw_ref[0:1, cs] + prev1 * fcw_ref[1:2, cs]
                          + u * fcw_ref[2:3, cs])
        act = (_silu_half(halves[0]) * halves[1]).astype(BF16)
        acc = acc + jnp.dot(act, wdn_ref[k * FFN_CHUNK:(k + 1) * FFN_CHUNK, :], preferred_element_type=F32)
    y_ref[...] = _rms(acc, fn_ref[...])
    if not decode:
        @pl.when(j == pl.num_programs(1) - 1)
        def _():
            ust_ref[0] = carry_ref[SUBLANES - (FFN_CONV_W - 1):SUBLANES, :]


def _outffn_prompt(x2d, ys, yp, wout, n2, wup, fcw, fcb, wdn, fn, nseq, seq, tm):
    nj = seq // tm
    row = lambda b, j: (b * nj + j, 0)
    return pl.pallas_call(
        functools.partial(_outffn_kernel, tm=tm, decode=False),
        grid=(nseq, nj),
        in_specs=[pl.BlockSpec((tm, D_MODEL), row),
                  pl.BlockSpec((tm, SSD_WIDTH), row),
                  pl.BlockSpec((tm, POOL_WIDTH), row),
                  _const_spec((D_MIX, D_MODEL)),
                  _const_spec((1, D_MODEL)),
                  _const_spec((D_MODEL, 2 * D_FF)),
                  _const_spec((FFN_CONV_W, 2 * D_FF)),
                  _const_spec((1, 2 * D_FF)),
                  _const_spec((D_FF, D_MODEL)),
                  _const_spec((1, D_MODEL))],
        out_specs=[pl.BlockSpec((tm, D_MODEL), row),
                   pl.BlockSpec((1, FFN_CONV_W - 1, 2 * D_FF), lambda b, j: (b, 0, 0))],
        out_shape=[jax.ShapeDtypeStruct((nseq * seq, D_MODEL), F32),
                   jax.ShapeDtypeStruct((nseq, FFN_CONV_W - 1, 2 * D_FF), F32)],
        scratch_shapes=[pltpu.VMEM((SUBLANES, 2 * D_FF), F32)],
        compiler_params=pltpu.CompilerParams(dimension_semantics=("arbitrary", "arbitrary"),
                                             vmem_limit_bytes=VMEM_LIMIT),
        name="outffn_prompt",
    )(x2d, ys, yp, wout, n2, wup, fcw, fcb, wdn, fn)


def _outffn_decode(x2d, ys, yp, ffn_state, wout, n2, wup, fcw, fcb, wdn, fn):
    n = x2d.shape[0]
    full = lambda shape: pl.BlockSpec(shape, lambda i: (0,) * len(shape))
    return pl.pallas_call(
        functools.partial(_outffn_kernel, tm=n, decode=True),
        grid=(1,),
        in_specs=[full((n, D_MODEL)), full((n, SSD_WIDTH)), full((n, POOL_WIDTH)), full(ffn_state.shape),
                  _const_spec((D_MIX, D_MODEL)),
                  _const_spec((1, D_MODEL)),
                  _const_spec((D_MODEL, 2 * D_FF)),
                  _const_spec((FFN_CONV_W, 2 * D_FF)),
                  _const_spec((1, 2 * D_FF)),
                  _const_spec((D_FF, D_MODEL)),
                  _const_spec((1, D_MODEL))],
        out_specs=[full((n, D_MODEL)), full(ffn_state.shape)],
        out_shape=[jax.ShapeDtypeStruct((n, D_MODEL), F32),
                   jax.ShapeDtypeStruct(ffn_state.shape, F32)],
        compiler_params=pltpu.CompilerParams(dimension_semantics=("arbitrary",),
                                             vmem_limit_bytes=VMEM_LIMIT),
        name="outffn_decode",
    )(x2d, ys, yp, ffn_state, wout, n2, wup, fcw, fcb, wdn, fn)


def kernel(x_prompt, x_sample, state_ssm, state_conv, state_pool, state_ffn_conv, norm1_w, w_in, conv_w, conv_b,
           dt_bias, a_log, d_skip, ssd_norm_w, pool_w, pool_scale, w_out, norm2_w, w_up, ffn_conv_w, ffn_conv_b,
           w_down, final_norm_w):
    assert w_in.shape[0] == 1, "single-layer model"
    nseq, seq, _ = x_prompt.shape
    nsmp, one, _ = x_sample.shape
    assert one == 1 and seq % CHUNK == 0 and nsmp == LANES

    wi = w_in[0]
    w_dt = jnp.pad(wi[:, SSD_WIDTH + CONV_DIM:SSD_WIDTH + CONV_DIM + N_HEADS], ((0, 0), (0, DT_PAD - N_HEADS)))
    w_cat = jnp.concatenate([wi[:, :SSD_WIDTH], wi[:, SSD_WIDTH:SSD_WIDTH + CONV_DIM],
                             wi[:, SSD_WIDTH + CONV_DIM + N_HEADS:], w_dt], axis=1).astype(BF16)
    n1 = norm1_w[0][None]
    cw, cb = conv_w[0], conv_b[0][None]
    dtb = jnp.pad(dt_bias[0], (0, DT_PAD - N_HEADS))[None]
    alog = jnp.pad(a_log[0], (0, DT_PAD - N_HEADS))[None]
    dskip = jnp.repeat(d_skip[0], HEAD_DIM)[None]
    nw = ssd_norm_w[0][None]
    pw = pool_w[0].astype(BF16)
    psc = pool_scale[0][None]
    wout = w_out[0].astype(BF16)
    n2 = norm2_w[0][None]
    wup = w_up[0].astype(BF16)
    gate_half = jnp.where(jnp.arange(2 * D_FF) < D_FF, 0.5, 1.0).astype(F32)
    fcw, fcb = ffn_conv_w[0] * gate_half, (ffn_conv_b[0] * gate_half)[None]
    wdn = w_down[0].astype(BF16)
    fn = final_norm_w[None]

    xp = x_prompt.reshape(nseq * seq, D_MODEL)
    xs_p, bc_p, bmt_p, acum_p, ast_p, gate_p, yp_p, ctail, vtail = _inproj_prompt(
        xp, n1, w_cat, 0.5 * cw, 0.5 * cb, dtb, alog, pw, psc, nseq, seq, tm=1024)
    per_seq = lambda a: a.reshape(nseq, -1, a.shape[-1])
    ys_p, hfin = _ssd_prompt(per_seq(xs_p), per_seq(bc_p), per_seq(bmt_p), per_seq(acum_p), per_seq(ast_p),
                             per_seq(gate_p), dskip, nw, nseq, seq)
    ys_p = ys_p.reshape(nseq * seq, SSD_WIDTH)
    y_p, ffn_p = _outffn_prompt(xp, ys_p, yp_p, wout, n2, wup, fcw, fcb, wdn, fn, nseq, seq, tm=256)

    xs_in = x_sample.reshape(nsmp, D_MODEL)
    z_s, xbc_s, v_s, dt_s = _inproj(xs_in, n1, w_cat, tm=nsmp)
    h_t = jnp.transpose(state_ssm[0], (1, 2, 3, 0))
    cs_t = jnp.transpose(state_conv[0], (1, 0, 2))
    ps_t = jnp.transpose(state_pool[0], (1, 0, 2))
    hnew_t, ys_s, yp_s, cs_new, ps_new = _step_mix(xbc_s, z_s, dt_s, v_s, cs_t, ps_t, h_t, cw, cb, dtb, alog,
                                                   dskip, nw, pw, psc, PAST_LEN)
    y_s, ffn_new = _outffn_decode(xs_in, ys_s, yp_s, state_ffn_conv[0], wout, n2, wup, fcw, fcb, wdn, fn)

    return (y_p.reshape(nseq, seq, D_MODEL),
            y_s.reshape(nsmp, 1, D_MODEL),
            hfin.reshape(nseq, N_HEADS, HEAD_DIM, D_STATE)[None],
            jnp.transpose(hnew_t, (3, 0, 1, 2))[None],
            ctail[:, SUBLANES - (CONV_W - 1):][None],
            jnp.transpose(cs_new, (1, 0, 2))[None],
            vtail[:, 2 * SUBLANES - (POOL_MAX - 1):][None],
            jnp.transpose(ps_new, (1, 0, 2))[None],
            ffn_p[None],
            ffn_new[None])
```

```python
import functools

import jax
import jax.numpy as jnp
from jax import lax
from jax.experimental import pallas as pl
from jax.experimental.pallas import tpu as pltpu

F32 = jnp.float32
BF16 = jnp.bfloat16

D_MODEL = 1024
SSD_WIDTH = 1536
HEAD_DIM = 64
N_HEADS = SSD_WIDTH // HEAD_DIM
N_GROUPS = 4
HEADS_PER_GROUP = N_HEADS // N_GROUPS
D_STATE = 64
BC_WIDTH = N_GROUPS * D_STATE
CONV_DIM = SSD_WIDTH + 2 * BC_WIDTH
CONV_W = 4
CHUNK = 128
POOL_WIDTH = 512
POOL_WINDOWS = (2, 4, 8, 16)
POOL_GROUP_DIM = POOL_WIDTH // len(POOL_WINDOWS)
POOL_MAX = max(POOL_WINDOWS)
D_MIX = SSD_WIDTH + POOL_WIDTH
D_FF = 2816
FFN_CONV_W = 3
NORM_GROUP = SSD_WIDTH // N_GROUPS
EPS = 1e-6
LOG2_E = 1.4426950408889634
PAST_LEN = 16384

LANES = 128
SUBLANES = 8
DT_PAD = LANES
VMEM_LIMIT = 56 * 1024 * 1024
HEAD_PAIRS = N_HEADS // 2

_Z0, _Z1 = 0, SSD_WIDTH
_X0, _X1 = _Z1, _Z1 + CONV_DIM
_V0, _V1 = _X1, _X1 + POOL_WIDTH
_T0, _T1 = _V1, _V1 + DT_PAD


def _rms(x, w):
    ms = jnp.mean(x * x, axis=-1, keepdims=True)
    return x * lax.rsqrt(ms + EPS) * w


def _silu_half(h):
    return h + h * jnp.tanh(h)


def _silu(x):
    return _silu_half(0.5 * x)


def _softplus(x):
    return jnp.maximum(x, 0.0) + jnp.log1p(jnp.exp(-jnp.abs(x)))


def _const_spec(shape):
    nd = len(shape)
    return pl.BlockSpec(shape, lambda *_: (0,) * nd, pipeline_mode=pl.Buffered(1))


def _delay_rows(u, prev8, s):
    if s == SUBLANES:
        return jnp.concatenate([prev8, u[:u.shape[0] - SUBLANES]], axis=0)
    r = pltpu.roll(u, s, axis=0)
    c = pltpu.roll(prev8, s, axis=0)
    row = lax.broadcasted_iota(jnp.int32, prev8.shape, 0)
    top = jnp.where(row < s, c, r[0:SUBLANES])
    return jnp.concatenate([top, r[SUBLANES:]], axis=0)


def _gate_and_norm(y, xs, gate, dskip, nw):
    y = (y + dskip * xs) * gate
    outs = []
    for g in range(N_GROUPS):
        sl = slice(g * NORM_GROUP, (g + 1) * NORM_GROUP)
        outs.append(_rms(y[:, sl], nw[:, sl]))
    return jnp.concatenate(outs, axis=1)


def _pool_project(win_sums, v, cnts, pw_ref, pscale):
    outs = []
    for g in range(len(POOL_WINDOWS)):
        sl = slice(g * POOL_GROUP_DIM, (g + 1) * POOL_GROUP_DIM)
        m = win_sums[g] * (1.0 / cnts[g]) - v[:, sl]
        outs.append(jnp.dot(m.astype(BF16), pw_ref[g], preferred_element_type=F32))
    return jnp.concatenate(outs, axis=1) * pscale


def _cumsum_rows(x):
    n = x.shape[0]
    row = lax.broadcasted_iota(jnp.int32, x.shape, 0)
    k = 1
    while k < n:
        x = x + jnp.where(row >= k, pltpu.roll(x, k, axis=0), 0.0)
        k *= 2
    return x


def _inproj_kernel(*refs, tm, prompt, pos0):
    if prompt:
        (x_ref, nw_ref, w_ref, cw_ref, cb_ref, dtb_ref, alog_ref, pw_ref, psc_ref,
         xs_ref, bc_ref, bmt_ref, acum_ref, ast_ref, gate_ref, yp_ref, ctail_ref, vtail_ref,
         cc_ref, vc_ref, s2c_ref, s4c_ref, s8c_ref, oc_ref) = refs
    else:
        x_ref, nw_ref, w_ref, z_ref, xbc_ref, v_ref, dt_ref = refs
    hn = _rms(x_ref[...], nw_ref[...]).astype(BF16)

    def mm(lo, hi):
        return jnp.dot(hn, w_ref[:, lo:hi], preferred_element_type=F32)

    if not prompt:
        z_ref[...] = mm(_Z0, _Z1)
        xbc_ref[...] = mm(_X0, _X1)
        v_ref[...] = mm(_V0, _V1)
        dt_ref[...] = mm(_T0, _T1)
        return

    j = pl.program_id(1)
    carries = (cc_ref, vc_ref, s2c_ref, s4c_ref, s8c_ref)

    @pl.when(j == 0)
    def _():
        for r in carries + (oc_ref,):
            r[...] = jnp.zeros(r.shape, F32)

    last = j == pl.num_programs(1) - 1

    def conv_stage(lo, hi):
        def fn(xbc):
            assert CONV_W == 4
            w0, w1, w2, w3 = (cw_ref[k:k + 1, lo:hi] for k in range(CONV_W))
            x2 = _delay_rows(xbc, cc_ref[:, lo:hi], 2)
            odd = w2 * xbc + w0 * x2
            conv = cb_ref[:, lo:hi] + (w3 * xbc + w1 * x2) + _delay_rows(odd, oc_ref[:, lo:hi], 1)
            cc_ref[:, lo:hi] = xbc[tm - SUBLANES:tm]
            oc_ref[:, lo:hi] = odd[tm - SUBLANES:tm]
            act = _silu_half(conv)
            if hi <= SSD_WIDTH:
                xs_ref[:, lo:hi] = act.astype(BF16)
            else:
                bc_ref[:, lo - SSD_WIDTH:hi - SSD_WIDTH] = act
                for c in range(tm // CHUNK):
                    for i in range(BC_WIDTH // LANES):
                        blk = act[c * CHUNK:(c + 1) * CHUNK, i * LANES:(i + 1) * LANES]
                        bmt_ref[c * BC_WIDTH + i * LANES:c * BC_WIDTH + (i + 1) * LANES, :] = blk.T
        return fn

    def pool_stage(v):
        s2 = v + _delay_rows(v, vc_ref[...], 1)
        s4 = s2 + _delay_rows(s2, s2c_ref[...], 2)
        s8 = s4 + _delay_rows(s4, s4c_ref[...], 4)
        s16 = s8 + _delay_rows(s8, s8c_ref[...], 8)
        for r, val in zip(carries[1:], (v, s2, s4, s8)):
            r[...] = val[tm - SUBLANES:tm]
        pos = pos0 + j * tm + lax.broadcasted_iota(jnp.int32, (tm, 1), 0)
        sums, cnts = [], []
        for g, (w, s) in enumerate(zip(POOL_WINDOWS, (s2, s4, s8, s16))):
            sums.append(s[:, g * POOL_GROUP_DIM:(g + 1) * POOL_GROUP_DIM])
            cnts.append(jnp.minimum(pos + 1, w).astype(F32))
        yp_ref[...] = _pool_project(sums, v, cnts, pw_ref, psc_ref[...]).astype(BF16)

        @pl.when(last)
        def _():
            vtail_ref[0] = v[tm - 2 * SUBLANES:tm]

    def gate_stage(lo, hi):
        def fn(z):
            gate_ref[:, lo:hi] = _silu(z)
        return fn

    def dt_stage(dt_raw):
        dt = _softplus(dt_raw + dtb_ref[...])
        neg_a = -jnp.exp(alog_ref[...])
        for c in range(tm // CHUNK):
            rows = slice(c * CHUNK, (c + 1) * CHUNK)
            acum = _cumsum_rows(dt[rows] * neg_a) * LOG2_E
            acum_ref[rows, :] = acum
            ast_ref[rows, :] = acum.T - jnp.log2(dt[rows]).T

    stages = [((_Z0, _Z1), gate_stage(0, SSD_WIDTH)),
              ((_T0, _T1), dt_stage),
              ((_X0, _X0 + SSD_WIDTH), conv_stage(0, SSD_WIDTH)),
              ((_X0 + SSD_WIDTH, _X1), conv_stage(SSD_WIDTH, CONV_DIM)),
              ((_V0, _V1), pool_stage)]
    for cols, fn in stages:
        fn(mm(*cols))

    @pl.when(last)
    def _():
        ctail_ref[0] = cc_ref[...]


def _inproj(x2d, norm_w, w_cat, tm):
    m = x2d.shape[0]
    row = lambda i: (i, 0)
    return pl.pallas_call(
        functools.partial(_inproj_kernel, tm=tm, prompt=False, pos0=0),
        grid=(m // tm,),
        in_specs=[pl.BlockSpec((tm, D_MODEL), row),
                  _const_spec((1, D_MODEL)),
                  _const_spec((D_MODEL, _T1))],
        out_specs=[pl.BlockSpec((tm, SSD_WIDTH), row),
                   pl.BlockSpec((tm, CONV_DIM), row),
                   pl.BlockSpec((tm, POOL_WIDTH), row),
                   pl.BlockSpec((tm, DT_PAD), row)],
        out_shape=[jax.ShapeDtypeStruct((m, SSD_WIDTH), F32),
                   jax.ShapeDtypeStruct((m, CONV_DIM), F32),
                   jax.ShapeDtypeStruct((m, POOL_WIDTH), F32),
                   jax.ShapeDtypeStruct((m, DT_PAD), F32)],
        compiler_params=pltpu.CompilerParams(dimension_semantics=("arbitrary",),
                                             vmem_limit_bytes=VMEM_LIMIT),
        name="inproj",
    )(x2d, norm_w, w_cat)


def _inproj_prompt(x2d, norm_w, w_cat, cw, cb, dtb, alog, pw, psc, nseq, seq, tm):
    nj = seq // tm
    m = nseq * seq
    row = lambda b, j: (b * nj + j, 0)
    per_seq = lambda b, j: (b, 0, 0)
    return pl.pallas_call(
        functools.partial(_inproj_kernel, tm=tm, prompt=True, pos0=0),
        grid=(nseq, nj),
        in_specs=[pl.BlockSpec((tm, D_MODEL), row),
                  _const_spec((1, D_MODEL)),
                  _const_spec((D_MODEL, _T1)),
                  _const_spec((CONV_W, CONV_DIM)),
                  _const_spec((1, CONV_DIM)),
                  _const_spec((1, DT_PAD)),
                  _const_spec((1, DT_PAD)),
                  _const_spec((len(POOL_WINDOWS), POOL_GROUP_DIM, POOL_GROUP_DIM)),
                  _const_spec((1, POOL_WIDTH))],
        out_specs=[pl.BlockSpec((tm, SSD_WIDTH), row),
                   pl.BlockSpec((tm, 2 * BC_WIDTH), row),
                   pl.BlockSpec((2 * tm, LANES), row),
                   pl.BlockSpec((tm, DT_PAD), row),
                   pl.BlockSpec((tm, DT_PAD), row),
                   pl.BlockSpec((tm, SSD_WIDTH), row),
                   pl.BlockSpec((tm, POOL_WIDTH), row),
                   pl.BlockSpec((1, SUBLANES, CONV_DIM), per_seq),
                   pl.BlockSpec((1, 2 * SUBLANES, POOL_WIDTH), per_seq)],
        out_shape=[jax.ShapeDtypeStruct((m, SSD_WIDTH), BF16),
                   jax.ShapeDtypeStruct((m, 2 * BC_WIDTH), F32),
                   jax.ShapeDtypeStruct((2 * m, LANES), F32),
                   jax.ShapeDtypeStruct((m, DT_PAD), F32),
                   jax.ShapeDtypeStruct((m, DT_PAD), F32),
                   jax.ShapeDtypeStruct((m, SSD_WIDTH), F32),
                   jax.ShapeDtypeStruct((m, POOL_WIDTH), BF16),
                   jax.ShapeDtypeStruct((nseq, SUBLANES, CONV_DIM), F32),
                   jax.ShapeDtypeStruct((nseq, 2 * SUBLANES, POOL_WIDTH), F32)],
        scratch_shapes=[pltpu.VMEM((SUBLANES, CONV_DIM), F32),
                        pltpu.VMEM((SUBLANES, POOL_WIDTH), F32),
                        pltpu.VMEM((SUBLANES, POOL_WIDTH), F32),
                        pltpu.VMEM((SUBLANES, POOL_WIDTH), F32),
                        pltpu.VMEM((SUBLANES, POOL_WIDTH), F32),
                        pltpu.VMEM((SUBLANES, CONV_DIM), F32)],
        compiler_params=pltpu.CompilerParams(dimension_semantics=("arbitrary", "arbitrary"),
                                             vmem_limit_bytes=VMEM_LIMIT),
        name="inproj_prompt",
    )(x2d, norm_w, w_cat, cw, cb, dtb, alog, pw, psc)


SSD_SEQS = 8


def _ssd_chunk(q, xs_ref, bc_ref, bmt_ref, acum_ref, ast_ref, gate_ref, dskip, nw, y_ref, state_ref):
    xs = xs_ref[q]
    bm = bc_ref[q, :, :BC_WIDTH]
    cm = bc_ref[q, :, BC_WIDTH:]
    bm_t = bmt_ref[q]
    acum = acum_ref[q]
    as_t = ast_ref[q]
    li = lax.broadcasted_iota(jnp.int32, (CHUNK, CHUNK), 0)
    si = lax.broadcasted_iota(jnp.int32, (CHUNK, CHUNK), 1)
    causal = li >= si
    first_half = si < HEAD_DIM
    first_half_n = lax.broadcasted_iota(jnp.int32, (D_STATE, LANES), 1) < HEAD_DIM
    first_half_row = lax.broadcasted_iota(jnp.int32, (1, LANES), 1) < HEAD_DIM
    zeros_n = jnp.zeros((D_STATE, LANES), BF16)
    updates = []

    for g in range(N_GROUPS):
        gs = slice(g * D_STATE, (g + 1) * D_STATE)
        c_g = cm[:, gs]
        c2 = jnp.concatenate([c_g, c_g], axis=1)
        cb = lax.dot_general(c_g.astype(BF16), bm[:, gs].astype(BF16), (((1,), (1,)), ((), ())),
                             preferred_element_type=F32)
        bt_g = bm_t[gs, :]
        y_pairs = []
        for k in range(HEADS_PER_GROUP // 2):
            pair = (g * HEADS_PER_GROUP) // 2 + k
            x_pair = xs[:, pair * LANES:(pair + 1) * LANES]
            s_pair = state_ref[q, pair]
            s_b = s_pair.astype(BF16)
            heads = (2 * pair, 2 * pair + 1)
            a_l = [jnp.broadcast_to(acum[:, h:h + 1], (CHUNK, CHUNK)) for h in heads]
            cdec = (c2 * jnp.exp2(jnp.where(first_half, a_l[0], a_l[1]))).astype(BF16)
            rhs = [jnp.concatenate([x_pair, s_b, zeros_n], axis=0),
                   jnp.concatenate([x_pair, zeros_n, s_b], axis=0)]
            res, lhs_s, cds = [], [], []
            for i, h in enumerate(heads):
                a_s = as_t[h:h + 1, :]
                mix = cb * jnp.exp2(jnp.where(causal, a_l[i] - a_s, -jnp.inf))
                lhs = jnp.concatenate([mix.astype(BF16), cdec], axis=1)
                res.append(jnp.dot(lhs, rhs[i], preferred_element_type=F32))
                a_last = acum[CHUNK - 1:CHUNK, h:h + 1]
                lhs_s.append(bt_g * jnp.exp2(a_last - a_s))
                cds.append(jnp.exp2(a_last))
            y_pairs.append(jnp.where(first_half, res[0], res[1]))
            updates.append((pair, jnp.concatenate(lhs_s, axis=0).astype(BF16), x_pair,
                            jnp.where(first_half_row, cds[0], cds[1]) * s_pair))

        ns = slice(g * NORM_GROUP, (g + 1) * NORM_GROUP)
        y_g = jnp.concatenate(y_pairs, axis=1)
        y_g = (y_g + dskip[:, ns] * xs[:, ns].astype(F32)) * gate_ref[q, :, ns]
        y_ref[q, :, ns] = _rms(y_g, nw[:, ns]).astype(BF16)

    for pair, lhs_s, x_pair, decayed in updates:
        upd = jnp.dot(lhs_s, x_pair, preferred_element_type=F32)
        state_ref[q, pair] = decayed + jnp.where(first_half_n, upd[:D_STATE], upd[D_STATE:])


def _ssd_prompt_kernel(xs_ref, bc_ref, bmt_ref, acum_ref, ast_ref, gate_ref, dskip_ref, nw_ref,
                       y_ref, hfin_ref, state_ref):
    c = pl.program_id(1)

    @pl.when(c == 0)
    def _():
        state_ref[...] = jnp.zeros(state_ref.shape, F32)

    for q in range(SSD_SEQS):
        _ssd_chunk(q, xs_ref, bc_ref, bmt_ref, acum_ref, ast_ref, gate_ref, dskip_ref[...], nw_ref[...],
                   y_ref, state_ref)

    @pl.when(c == pl.num_programs(1) - 1)
    def _():
        for q in range(SSD_SEQS):
            for pair in range(HEAD_PAIRS):
                t = jnp.concatenate([state_ref[q, pair], jnp.zeros((LANES - D_STATE, LANES), F32)], axis=0).T
                hfin_ref[q, pair * LANES:(pair + 1) * LANES, :] = t[:, :D_STATE]


def _ssd_prompt(xs, bc, bmt, acum, ast, gate, dskip, nw, nseq, seq):
    nc = seq // CHUNK
    blk = lambda width, rows=CHUNK: pl.BlockSpec((SSD_SEQS, rows, width), lambda b, c: (b, c, 0))
    return pl.pallas_call(
        _ssd_prompt_kernel,
        grid=(nseq // SSD_SEQS, nc),
        in_specs=[blk(SSD_WIDTH), blk(2 * BC_WIDTH), blk(LANES, BC_WIDTH), blk(DT_PAD), blk(DT_PAD),
                  blk(SSD_WIDTH),
                  _const_spec((1, SSD_WIDTH)),
                  _const_spec((1, SSD_WIDTH))],
        out_specs=[blk(SSD_WIDTH),
                   pl.BlockSpec((SSD_SEQS, SSD_WIDTH, D_STATE), lambda b, c: (b, 0, 0))],
        out_shape=[jax.ShapeDtypeStruct((nseq, seq, SSD_WIDTH), BF16),
                   jax.ShapeDtypeStruct((nseq, SSD_WIDTH, D_STATE), F32)],
        scratch_shapes=[pltpu.VMEM((SSD_SEQS, HEAD_PAIRS, D_STATE, LANES), F32)],
        compiler_params=pltpu.CompilerParams(dimension_semantics=("arbitrary", "arbitrary"),
                                             vmem_limit_bytes=VMEM_LIMIT),
        name="ssd_prompt",
    )(xs, bc, bmt, acum, ast, gate, dskip, nw)


STEP_HEADS = 4


def _step_mix_kernel(xbc_ref, z_ref, dt_ref, v_ref, cs_ref, ps_ref, h_ref,
                     cw_ref, cb_ref, dtb_ref, alog_ref, dskip_ref, nw_ref, pw_ref, psc_ref,
                     hout_ref, y_ref, yp_ref, cso_ref, pso_ref,
                     xs_ref, xdt_ref, bt_ref, ct_ref, dect_ref, yt_ref, *, pos0):
    h = pl.program_id(0)
    nprev = POOL_MAX - 1

    @pl.when(h == 0)
    def _prepare():
        xnew = xbc_ref[...]
        conv = cb_ref[...] + xnew * cw_ref[CONV_W - 1:CONV_W, :]
        for k in range(CONV_W - 1):
            conv = conv + cs_ref[k] * cw_ref[k:k + 1, :]
        xbc = _silu(conv)
        xs = xbc[:, :SSD_WIDTH]
        xs_ref[...] = xs
        dt = _softplus(dt_ref[...] + dtb_ref[...])
        dect_ref[...] = jnp.exp(dt * (-jnp.exp(alog_ref[...]))).T
        dt_t = dt.T
        for hh in range(N_HEADS):
            blk = xs[:, (hh // 2) * LANES:(hh // 2 + 1) * LANES].T
            half = blk[(hh % 2) * HEAD_DIM:(hh % 2 + 1) * HEAD_DIM, :]
            xdt_ref[hh * HEAD_DIM:(hh + 1) * HEAD_DIM, :] = half * dt_t[hh:hh + 1, :]
        for i in range(BC_WIDTH // LANES):
            sl = slice(i * LANES, (i + 1) * LANES)
            bt_ref[sl, :] = xbc[:, SSD_WIDTH + i * LANES:SSD_WIDTH + (i + 1) * LANES].T
            ct_ref[sl, :] = xbc[:, SSD_WIDTH + BC_WIDTH + i * LANES:SSD_WIDTH + BC_WIDTH + (i + 1) * LANES].T
        for k in range(CONV_W - 2):
            cso_ref[k] = cs_ref[k + 1]
        cso_ref[CONV_W - 2] = xnew
        for k in range(nprev - 1):
            pso_ref[k] = ps_ref[k + 1]
        pso_ref[nprev - 1] = v_ref[...]

    for i in range(STEP_HEADS):
        head = h * STEP_HEADS + i
        g = lax.div(head, HEADS_PER_GROUP)
        hrow = pl.multiple_of(head * HEAD_DIM, HEAD_DIM)
        grow = pl.multiple_of(g * D_STATE, D_STATE)
        b_t = bt_ref[pl.ds(grow, D_STATE), :]
        c_t = ct_ref[pl.ds(grow, D_STATE), :]
        dec = dect_ref[pl.ds(head, 1), :]

        def per_p(p, carry, i=i, hrow=hrow, b_t=b_t, c_t=c_t, dec=dec):
            xrow = xdt_ref[pl.ds(hrow + p, 1), :]
            hn = h_ref[i, p] * dec + xrow * b_t
            hout_ref[i, p] = hn
            yt_ref[pl.ds(hrow + p, 1), :] = jnp.sum(hn * c_t, axis=0, keepdims=True)
            return carry

        lax.fori_loop(0, HEAD_DIM, per_p, 0, unroll=4)

    @pl.when(h == pl.num_programs(0) - 1)
    def _finish():
        y0 = jnp.concatenate([yt_ref[i * LANES:(i + 1) * LANES, :].T for i in range(SSD_WIDTH // LANES)], axis=1)
        y = _gate_and_norm(y0, xs_ref[...], _silu(z_ref[...]), dskip_ref[...], nw_ref[...])
        y_ref[...] = y.astype(BF16)
        v = v_ref[...]
        sums, cnts = [], []
        for gi, w in enumerate(POOL_WINDOWS):
            sl = slice(gi * POOL_GROUP_DIM, (gi + 1) * POOL_GROUP_DIM)
            acc = v[:, sl]
            for j in range(1, w):
                acc = acc + ps_ref[nprev - j][:, sl]
            sums.append(acc)
            cnts.append(float(min(pos0 + 1, w)))
        yp_ref[...] = _pool_project(sums, v, cnts, pw_ref, psc_ref[...]).astype(BF16)


def _step_mix(xbc, z, dt, v, cs_t, ps_t, h_t, cw, cb, dtb, alog, dskip, nw, pw, psc, pos0):
    n = xbc.shape[0]
    nprev = POOL_MAX - 1
    full = lambda shape: pl.BlockSpec(shape, lambda i: (0,) * len(shape))
    hblk = pl.BlockSpec((STEP_HEADS, HEAD_DIM, D_STATE, n), lambda i: (i, 0, 0, 0))
    return pl.pallas_call(
        functools.partial(_step_mix_kernel, pos0=pos0),
        grid=(N_HEADS // STEP_HEADS,),
        in_specs=[full((n, CONV_DIM)), full((n, SSD_WIDTH)), full((n, DT_PAD)), full((n, POOL_WIDTH)),
                  _const_spec((CONV_W - 1, n, CONV_DIM)), _const_spec((nprev, n, POOL_WIDTH)), hblk,
                  full((CONV_W, CONV_DIM)), full((1, CONV_DIM)), full((1, DT_PAD)), full((1, DT_PAD)),
                  full((1, SSD_WIDTH)), full((1, SSD_WIDTH)),
                  full((len(POOL_WINDOWS), POOL_GROUP_DIM, POOL_GROUP_DIM)), full((1, POOL_WIDTH))],
        out_specs=[hblk, full((n, SSD_WIDTH)), full((n, POOL_WIDTH)),
                   full((CONV_W - 1, n, CONV_DIM)), full((nprev, n, POOL_WIDTH))],
        out_shape=[jax.ShapeDtypeStruct(h_t.shape, F32),
                   jax.ShapeDtypeStruct((n, SSD_WIDTH), BF16),
                   jax.ShapeDtypeStruct((n, POOL_WIDTH), BF16),
                   jax.ShapeDtypeStruct((CONV_W - 1, n, CONV_DIM), F32),
                   jax.ShapeDtypeStruct((nprev, n, POOL_WIDTH), F32)],
        scratch_shapes=[pltpu.VMEM((n, SSD_WIDTH), F32),
                        pltpu.VMEM((SSD_WIDTH, n), F32),
                        pltpu.VMEM((BC_WIDTH, n), F32),
                        pltpu.VMEM((BC_WIDTH, n), F32),
                        pltpu.VMEM((DT_PAD, n), F32),
                        pltpu.VMEM((SSD_WIDTH, n), F32)],
        compiler_params=pltpu.CompilerParams(dimension_semantics=("arbitrary",),
                                             vmem_limit_bytes=VMEM_LIMIT),
        name="step_mix",
    )(xbc, z, dt, v, cs_t, ps_t, h_t, cw, cb, dtb, alog, dskip, nw, pw, psc)


FFN_CHUNK = 256


def _outffn_kernel(*refs, tm, decode):
    if decode:
        (x_ref, ys_ref, yp_ref, st_ref, wout_ref, n2_ref, wup_ref, fcw_ref, fcb_ref, wdn_ref, fn_ref,
         y_ref, sto_ref) = refs
    else:
        (x_ref, ys_ref, yp_ref, wout_ref, n2_ref, wup_ref, fcw_ref, fcb_ref, wdn_ref, fn_ref,
         y_ref, ust_ref, carry_ref) = refs
        j = pl.program_id(1)

        @pl.when(j == 0)
        def _():
            carry_ref[...] = jnp.zeros(carry_ref.shape, F32)

    ymix = jnp.concatenate([ys_ref[...], yp_ref[...]], axis=1)
    x1 = x_ref[...] + jnp.dot(ymix, wout_ref[...], preferred_element_type=F32)
    hn = _rms(x1, n2_ref[...]).astype(BF16)
    acc = x1
    nchunk = D_FF // FFN_CHUNK

    def col_slices(k):
        return [slice(base + k * FFN_CHUNK, base + (k + 1) * FFN_CHUNK) for base in (0, D_FF)]

    def up_proj(k):
        return [jnp.dot(hn, wup_ref[:, cs], preferred_element_type=F32) for cs in col_slices(k)]

    u_next = up_proj(0)
    for k in range(nchunk):
        u_cur = u_next
        if k + 1 < nchunk:
            u_next = up_proj(k + 1)
        halves = []
        for u, cs in zip(u_cur, col_slices(k)):
            if decode:
                prev2, prev1 = st_ref[:, 0, cs], st_ref[:, 1, cs]
                sto_ref[:, 0, cs] = prev1
                sto_ref[:, 1, cs] = u
            else:
                prev8 = carry_ref[:, cs]
                prev1 = _delay_rows(u, prev8, 1)
                prev2 = _delay_rows(u, prev8, 2)
                carry_ref[:, cs] = u[tm - SUBLANES:tm]
            halves.append(fcb_ref[:, cs] + prev2 * fcw_ref[0:1, cs] + prev1 * fcw_ref[1:2, cs]
                          + u * fcw_ref[2:3, cs])
        act = (_silu_half(halves[0]) * halves[1]).astype(BF16)
        acc = acc + jnp.dot(act, wdn_ref[k * FFN_CHUNK:(k + 1) * FFN_CHUNK, :], preferred_element_type=F32)
    y_ref[...] = _rms(acc, fn_ref[...])
    if not decode:
        @pl.when(j == pl.num_programs(1) - 1)
        def _():
            ust_ref[0] = carry_ref[SUBLANES - (FFN_CONV_W - 1):SUBLANES, :]


def _outffn_prompt(x2d, ys, yp, wout, n2, wup, fcw, fcb, wdn, fn, nseq, seq, tm):
    nj = seq // tm
    row = lambda b, j: (b * nj + j, 0)
    return pl.pallas_call(
        functools.partial(_outffn_kernel, tm=tm, decode=False),
        grid=(nseq, nj),
        in_specs=[pl.BlockSpec((tm, D_MODEL), row),
                  pl.BlockSpec((tm, SSD_WIDTH), row),
                  pl.BlockSpec((tm, POOL_WIDTH), row),
                  _const_spec((D_MIX, D_MODEL)),
                  _const_spec((1, D_MODEL)),
                  _const_spec((D_MODEL, 2 * D_FF)),
                  _const_spec((FFN_CONV_W, 2 * D_FF)),
                  _const_spec((1, 2 * D_FF)),
                  _const_spec((D_FF, D_MODEL)),
                  _const_spec((1, D_MODEL))],
        out_specs=[pl.BlockSpec((tm, D_MODEL), row),
                   pl.BlockSpec((1, FFN_CONV_W - 1, 2 * D_FF), lambda b, j: (b, 0, 0))],
        out_shape=[jax.ShapeDtypeStruct((nseq * seq, D_MODEL), F32),
                   jax.ShapeDtypeStruct((nseq, FFN_CONV_W - 1, 2 * D_FF), F32)],
        scratch_shapes=[pltpu.VMEM((SUBLANES, 2 * D_FF), F32)],
        compiler_params=pltpu.CompilerParams(dimension_semantics=("arbitrary", "arbitrary"),
                                             vmem_limit_bytes=VMEM_LIMIT),
        name="outffn_prompt",
    )(x2d, ys, yp, wout, n2, wup, fcw, fcb, wdn, fn)


def _outffn_decode(x2d, ys, yp, ffn_state, wout, n2, wup, fcw, fcb, wdn, fn):
    n = x2d.shape[0]
    full = lambda shape: pl.BlockSpec(shape, lambda i: (0,) * len(shape))
    return pl.pallas_call(
        functools.partial(_outffn_kernel, tm=n, decode=True),
        grid=(1,),
        in_specs=[full((n, D_MODEL)), full((n, SSD_WIDTH)), full((n, POOL_WIDTH)), full(ffn_state.shape),
                  _const_spec((D_MIX, D_MODEL)),
                  _const_spec((1, D_MODEL)),
                  _const_spec((D_MODEL, 2 * D_FF)),
                  _const_spec((FFN_CONV_W, 2 * D_FF)),
                  _const_spec((1, 2 * D_FF)),
                  _const_spec((D_FF, D_MODEL)),
                  _const_spec((1, D_MODEL))],
        out_specs=[full((n, D_MODEL)), full(ffn_state.shape)],
        out_shape=[jax.ShapeDtypeStruct((n, D_MODEL), F32),
                   jax.ShapeDtypeStruct(ffn_state.shape, F32)],
        compiler_params=pltpu.CompilerParams(dimension_semantics=("arbitrary",),
                                             vmem_limit_bytes=VMEM_LIMIT),
        name="outffn_decode",
    )(x2d, ys, yp, ffn_state, wout, n2, wup, fcw, fcb, wdn, fn)


def kernel(x_prompt, x_sample, state_ssm, state_conv, state_pool, state_ffn_conv, norm1_w, w_in, conv_w, conv_b,
           dt_bias, a_log, d_skip, ssd_norm_w, pool_w, pool_scale, w_out, norm2_w, w_up, ffn_conv_w, ffn_conv_b,
           w_down, final_norm_w):
    assert w_in.shape[0] == 1, "single-layer model"
    nseq, seq, _ = x_prompt.shape
    nsmp, one, _ = x_sample.shape
    assert one == 1 and seq % CHUNK == 0 and nsmp == LANES

    wi = w_in[0]
    w_dt = jnp.pad(wi[:, SSD_WIDTH + CONV_DIM:SSD_WIDTH + CONV_DIM + N_HEADS], ((0, 0), (0, DT_PAD - N_HEADS)))
    w_cat = jnp.concatenate([wi[:, :SSD_WIDTH], wi[:, SSD_WIDTH:SSD_WIDTH + CONV_DIM],
                             wi[:, SSD_WIDTH + CONV_DIM + N_HEADS:], w_dt], axis=1).astype(BF16)
    n1 = norm1_w[0][None]
    cw, cb = conv_w[0], conv_b[0][None]
    dtb = jnp.pad(dt_bias[0], (0, DT_PAD - N_HEADS))[None]
    alog = jnp.pad(a_log[0], (0, DT_PAD - N_HEADS))[None]
    dskip = jnp.repeat(d_skip[0], HEAD_DIM)[None]
    nw = ssd_norm_w[0][None]
    pw = pool_w[0].astype(BF16)
    psc = pool_scale[0][None]
    wout = w_out[0].astype(BF16)
    n2 = norm2_w[0][None]
    wup = w_up[0].astype(BF16)
    gate_half = jnp.where(jnp.arange(2 * D_FF) < D_FF, 0.5, 1.0).astype(F32)
    fcw, fcb = ffn_conv_w[0] * gate_half, (ffn_conv_b[0] * gate_half)[None]
    wdn = w_down[0].astype(BF16)
    fn = final_norm_w[None]

    xp = x_prompt.reshape(nseq * seq, D_MODEL)
    xs_p, bc_p, bmt_p, acum_p, ast_p, gate_p, yp_p, ctail, vtail = _inproj_prompt(
        xp, n1, w_cat, 0.5 * cw, 0.5 * cb, dtb, alog, pw, psc, nseq, seq, tm=1024)
    per_seq = lambda a: a.reshape(nseq, -1, a.shape[-1])
    ys_p, hfin = _ssd_prompt(per_seq(xs_p), per_seq(bc_p), per_seq(bmt_p), per_seq(acum_p), per_seq(ast_p),
                             per_seq(gate_p), dskip, nw, nseq, seq)
    ys_p = ys_p.reshape(nseq * seq, SSD_WIDTH)
    y_p, ffn_p = _outffn_prompt(xp, ys_p, yp_p, wout, n2, wup, fcw, fcb, wdn, fn, nseq, seq, tm=256)

    xs_in = x_sample.reshape(nsmp, D_MODEL)
    z_s, xbc_s, v_s, dt_s = _inproj(xs_in, n1, w_cat, tm=nsmp)
    h_t = jnp.transpose(state_ssm[0], (1, 2, 3, 0))
    cs_t = jnp.transpose(state_conv[0], (1, 0, 2))
    ps_t = jnp.transpose(state_pool[0], (1, 0, 2))
    hnew_t, ys_s, yp_s, cs_new, ps_new = _step_mix(xbc_s, z_s, dt_s, v_s, cs_t, ps_t, h_t, cw, cb, dtb, alog,
                                                   dskip, nw, pw, psc, PAST_LEN)
    y_s, ffn_new = _outffn_decode(xs_in, ys_s, yp_s, state_ffn_conv[0], wout, n2, wup, fcw, fcb, wdn, fn)

    return (y_p.reshape(nseq, seq, D_MODEL),
            y_s.reshape(nsmp, 1, D_MODEL),
            hfin.reshape(nseq, N_HEADS, HEAD_DIM, D_STATE)[None],
            jnp.transpose(hnew_t, (3, 0, 1, 2))[None],
            ctail[:, SUBLANES - (CONV_W - 1):][None],
            jnp.transpose(cs_new, (1, 0, 2))[None],
            vtail[:, 2 * SUBLANES - (POOL_MAX - 1):][None],
            jnp.transpose(ps_new, (1, 0, 2))[None],
            ffn_p[None],
            ffn_new[None])
```

```python
import functools

import jax
import jax.numpy as jnp
from jax import lax
from jax.experimental import pallas as pl
from jax.experimental.pallas import tpu as pltpu

F32 = jnp.float32
BF16 = jnp.bfloat16

D_MODEL = 1024
SSD_WIDTH = 1536
HEAD_DIM = 64
N_HEADS = SSD_WIDTH // HEAD_DIM
N_GROUPS = 4
HEADS_PER_GROUP = N_HEADS // N_GROUPS
D_STATE = 64
BC_WIDTH = N_GROUPS * D_STATE
CONV_DIM = SSD_WIDTH + 2 * BC_WIDTH
CONV_W = 4
CHUNK = 128
POOL_WIDTH = 512
POOL_WINDOWS = (2, 4, 8, 16)
POOL_GROUP_DIM = POOL_WIDTH // len(POOL_WINDOWS)
POOL_MAX = max(POOL_WINDOWS)
D_MIX = SSD_WIDTH + POOL_WIDTH
D_FF = 2816
FFN_CONV_W = 3
NORM_GROUP = SSD_WIDTH // N_GROUPS
EPS = 1e-6
LOG2_E = 1.4426950408889634
PAST_LEN = 16384

LANES = 128
SUBLANES = 8
DT_PAD = LANES
VMEM_LIMIT = 56 * 1024 * 1024
HEAD_PAIRS = N_HEADS // 2

_Z0, _Z1 = 0, SSD_WIDTH
_X0, _X1 = _Z1, _Z1 + CONV_DIM
_V0, _V1 = _X1, _X1 + POOL_WIDTH
_T0, _T1 = _V1, _V1 + DT_PAD


def _rms(x, w):
    ms = jnp.mean(x * x, axis=-1, keepdims=True)
    return x * lax.rsqrt(ms + EPS) * w


def _silu_half(h):
    return h + h * jnp.tanh(h)


def _silu(x):
    return _silu_half(0.5 * x)


def _softplus(x):
    return jnp.maximum(x, 0.0) + jnp.log1p(jnp.exp(-jnp.abs(x)))


def _const_spec(shape):
    nd = len(shape)
    return pl.BlockSpec(shape, lambda *_: (0,) * nd, pipeline_mode=pl.Buffered(1))


def _delay_rows(u, prev8, s):
    if s == SUBLANES:
        return jnp.concatenate([prev8, u[:u.shape[0] - SUBLANES]], axis=0)
    r = pltpu.roll(u, s, axis=0)
    c = pltpu.roll(prev8, s, axis=0)
    row = lax.broadcasted_iota(jnp.int32, prev8.shape, 0)
    top = jnp.where(row < s, c, r[0:SUBLANES])
    return jnp.concatenate([top, r[SUBLANES:]], axis=0)


def _gate_and_norm(y, xs, gate, dskip, nw):
    y = (y + dskip * xs) * gate
    outs = []
    for g in range(N_GROUPS):
        sl = slice(g * NORM_GROUP, (g + 1) * NORM_GROUP)
        outs.append(_rms(y[:, sl], nw[:, sl]))
    return jnp.concatenate(outs, axis=1)


def _pool_project(win_sums, v, cnts, pw_ref, pscale):
    outs = []
    for g in range(len(POOL_WINDOWS)):
        sl = slice(g * POOL_GROUP_DIM, (g + 1) * POOL_GROUP_DIM)
        m = win_sums[g] * (1.0 / cnts[g]) - v[:, sl]
        outs.append(jnp.dot(m.astype(BF16), pw_ref[g], preferred_element_type=F32))
    return jnp.concatenate(outs, axis=1) * pscale


def _cumsum_rows(x):
    n = x.shape[0]
    row = lax.broadcasted_iota(jnp.int32, x.shape, 0)
    k = 1
    while k < n:
        x = x + jnp.where(row >= k, pltpu.roll(x, k, axis=0), 0.0)
        k *= 2
    return x


def _inproj_kernel(*refs, tm, prompt, pos0):
    if prompt:
        (x_ref, nw_ref, w_ref, cw_ref, cb_ref, dtb_ref, alog_ref, pw_ref, psc_ref,
         xs_ref, bc_ref, bmt_ref, acum_ref, ast_ref, dtt_ref, gate_ref, yp_ref, ctail_ref, vtail_ref,
         cc_ref, vc_ref, s2c_ref, s4c_ref, s8c_ref, oc_ref) = refs
    else:
        x_ref, nw_ref, w_ref, z_ref, xbc_ref, v_ref, dt_ref = refs
    hn = _rms(x_ref[...], nw_ref[...]).astype(BF16)

    def mm(lo, hi):
        return jnp.dot(hn, w_ref[:, lo:hi], preferred_element_type=F32)

    if not prompt:
        z_ref[...] = mm(_Z0, _Z1)
        xbc_ref[...] = mm(_X0, _X1)
        v_ref[...] = mm(_V0, _V1)
        dt_ref[...] = mm(_T0, _T1)
        return

    j = pl.program_id(1)
    carries = (cc_ref, vc_ref, s2c_ref, s4c_ref, s8c_ref)

    @pl.when(j == 0)
    def _():
        for r in carries + (oc_ref,):
            r[...] = jnp.zeros(r.shape, F32)

    last = j == pl.num_programs(1) - 1

    def conv_stage(lo, hi):
        def fn(xbc):
            assert CONV_W == 4
            w0, w1, w2, w3 = (cw_ref[k:k + 1, lo:hi] for k in range(CONV_W))
            x2 = _delay_rows(xbc, cc_ref[:, lo:hi], 2)
            odd = w2 * xbc + w0 * x2
            conv = cb_ref[:, lo:hi] + (w3 * xbc + w1 * x2) + _delay_rows(odd, oc_ref[:, lo:hi], 1)
            cc_ref[:, lo:hi] = xbc[tm - SUBLANES:tm]
            oc_ref[:, lo:hi] = odd[tm - SUBLANES:tm]
            act = _silu_half(conv)
            if hi <= SSD_WIDTH:
                xs_ref[:, lo:hi] = act.astype(BF16)
            else:
                bc_ref[:, lo - SSD_WIDTH:hi - SSD_WIDTH] = act
                for c in range(tm // CHUNK):
                    for i in range(BC_WIDTH // LANES):
                        blk = act[c * CHUNK:(c + 1) * CHUNK, i * LANES:(i + 1) * LANES]
                        bmt_ref[c * BC_WIDTH + i * LANES:c * BC_WIDTH + (i + 1) * LANES, :] = blk.T
        return fn

    def pool_stage(v):
        s2 = v + _delay_rows(v, vc_ref[...], 1)
        s4 = s2 + _delay_rows(s2, s2c_ref[...], 2)
        s8 = s4 + _delay_rows(s4, s4c_ref[...], 4)
        s16 = s8 + _delay_rows(s8, s8c_ref[...], 8)
        for r, val in zip(carries[1:], (v, s2, s4, s8)):
            r[...] = val[tm - SUBLANES:tm]
        pos = pos0 + j * tm + lax.broadcasted_iota(jnp.int32, (tm, 1), 0)
        sums, cnts = [], []
        for g, (w, s) in enumerate(zip(POOL_WINDOWS, (s2, s4, s8, s16))):
            sums.append(s[:, g * POOL_GROUP_DIM:(g + 1) * POOL_GROUP_DIM])
            cnts.append(jnp.minimum(pos + 1, w).astype(F32))
        yp_ref[...] = _pool_project(sums, v, cnts, pw_ref, psc_ref[...]).astype(BF16)

        @pl.when(last)
        def _():
            vtail_ref[0] = v[tm - 2 * SUBLANES:tm]

    def gate_stage(lo, hi):
        def fn(z):
            gate_ref[:, lo:hi] = _silu(z)
        return fn

    def dt_stage(dt_raw):
        dt = _softplus(dt_raw + dtb_ref[...])
        neg_a = -jnp.exp(alog_ref[...])
        for c in range(tm // CHUNK):
            rows = slice(c * CHUNK, (c + 1) * CHUNK)
            acum = _cumsum_rows(dt[rows] * neg_a) * LOG2_E
            acum_ref[rows, :] = acum
            ast_ref[rows, :] = acum.T
            dtt_ref[rows, :] = dt[rows].T

    stages = [((_Z0, _Z1), gate_stage(0, SSD_WIDTH)),
              ((_T0, _T1), dt_stage),
              ((_X0, _X0 + SSD_WIDTH), conv_stage(0, SSD_WIDTH)),
              ((_X0 + SSD_WIDTH, _X1), conv_stage(SSD_WIDTH, CONV_DIM)),
              ((_V0, _V1), pool_stage)]
    for cols, fn in stages:
        fn(mm(*cols))

    @pl.when(last)
    def _():
        ctail_ref[0] = cc_ref[...]


def _inproj(x2d, norm_w, w_cat, tm):
    m = x2d.shape[0]
    row = lambda i: (i, 0)
    return pl.pallas_call(
        functools.partial(_inproj_kernel, tm=tm, prompt=False, pos0=0),
        grid=(m // tm,),
        in_specs=[pl.BlockSpec((tm, D_MODEL), row),
                  _const_spec((1, D_MODEL)),
                  _const_spec((D_MODEL, _T1))],
        out_specs=[pl.BlockSpec((tm, SSD_WIDTH), row),
                   pl.BlockSpec((tm, CONV_DIM), row),
                   pl.BlockSpec((tm, POOL_WIDTH), row),
                   pl.BlockSpec((tm, DT_PAD), row)],
        out_shape=[jax.ShapeDtypeStruct((m, SSD_WIDTH), F32),
                   jax.ShapeDtypeStruct((m, CONV_DIM), F32),
                   jax.ShapeDtypeStruct((m, POOL_WIDTH), F32),
                   jax.ShapeDtypeStruct((m, DT_PAD), F32)],
        compiler_params=pltpu.CompilerParams(dimension_semantics=("arbitrary",),
                                             vmem_limit_bytes=VMEM_LIMIT),
        name="inproj",
    )(x2d, norm_w, w_cat)


def _inproj_prompt(x2d, norm_w, w_cat, cw, cb, dtb, alog, pw, psc, nseq, seq, tm):
    nj = seq // tm
    m = nseq * seq
    row = lambda b, j: (b * nj + j, 0)
    per_seq = lambda b, j: (b, 0, 0)
    return pl.pallas_call(
        functools.partial(_inproj_kernel, tm=tm, prompt=True, pos0=0),
        grid=(nseq, nj),
        in_specs=[pl.BlockSpec((tm, D_MODEL), row),
                  _const_spec((1, D_MODEL)),
                  _const_spec((D_MODEL, _T1)),
                  _const_spec((CONV_W, CONV_DIM)),
                  _const_spec((1, CONV_DIM)),
                  _const_spec((1, DT_PAD)),
                  _const_spec((1, DT_PAD)),
                  _const_spec((len(POOL_WINDOWS), POOL_GROUP_DIM, POOL_GROUP_DIM)),
                  _const_spec((1, POOL_WIDTH))],
        out_specs=[pl.BlockSpec((tm, SSD_WIDTH), row),
                   pl.BlockSpec((tm, 2 * BC_WIDTH), row),
                   pl.BlockSpec((2 * tm, LANES), row),
                   pl.BlockSpec((tm, DT_PAD), row),
                   pl.BlockSpec((tm, DT_PAD), row),
                   pl.BlockSpec((tm, DT_PAD), row),
                   pl.BlockSpec((tm, SSD_WIDTH), row),
                   pl.BlockSpec((tm, POOL_WIDTH), row),
                   pl.BlockSpec((1, SUBLANES, CONV_DIM), per_seq),
                   pl.BlockSpec((1, 2 * SUBLANES, POOL_WIDTH), per_seq)],
        out_shape=[jax.ShapeDtypeStruct((m, SSD_WIDTH), BF16),
                   jax.ShapeDtypeStruct((m, 2 * BC_WIDTH), F32),
                   jax.ShapeDtypeStruct((2 * m, LANES), F32),
                   jax.ShapeDtypeStruct((m, DT_PAD), F32),
                   jax.ShapeDtypeStruct((m, DT_PAD), F32),
                   jax.ShapeDtypeStruct((m, DT_PAD), F32),
                   jax.ShapeDtypeStruct((m, SSD_WIDTH), F32),
                   jax.ShapeDtypeStruct((m, POOL_WIDTH), BF16),
                   jax.ShapeDtypeStruct((nseq, SUBLANES, CONV_DIM), F32),
                   jax.ShapeDtypeStruct((nseq, 2 * SUBLANES, POOL_WIDTH), F32)],
        scratch_shapes=[pltpu.VMEM((SUBLANES, CONV_DIM), F32),
                        pltpu.VMEM((SUBLANES, POOL_WIDTH), F32),
                        pltpu.VMEM((SUBLANES, POOL_WIDTH), F32),
                        pltpu.VMEM((SUBLANES, POOL_WIDTH), F32),
                        pltpu.VMEM((SUBLANES, POOL_WIDTH), F32),
                        pltpu.VMEM((SUBLANES, CONV_DIM), F32)],
        compiler_params=pltpu.CompilerParams(dimension_semantics=("arbitrary", "arbitrary"),
                                             vmem_limit_bytes=VMEM_LIMIT),
        name="inproj_prompt",
    )(x2d, norm_w, w_cat, cw, cb, dtb, alog, pw, psc)


SSD_SEQS = 8


def _ssd_chunk(q, xs_ref, bc_ref, bmt_ref, acum_ref, ast_ref, dtt_ref, gate_ref, dskip, nw, y_ref, state_ref):
    xs = xs_ref[q]
    bm = bc_ref[q, :, :BC_WIDTH]
    cm = bc_ref[q, :, BC_WIDTH:]
    bm_t = bmt_ref[q]
    acum = acum_ref[q]
    as_t = ast_ref[q]
    dt_t = dtt_ref[q]
    li = lax.broadcasted_iota(jnp.int32, (CHUNK, CHUNK), 0)
    si = lax.broadcasted_iota(jnp.int32, (CHUNK, CHUNK), 1)
    causal = li >= si
    first_half = si < HEAD_DIM
    first_half_n = lax.broadcasted_iota(jnp.int32, (D_STATE, LANES), 1) < HEAD_DIM
    first_half_row = lax.broadcasted_iota(jnp.int32, (1, LANES), 1) < HEAD_DIM
    zeros_n = jnp.zeros((D_STATE, LANES), BF16)
    updates = []

    for g in range(N_GROUPS):
        gs = slice(g * D_STATE, (g + 1) * D_STATE)
        c_g = cm[:, gs]
        c2 = jnp.concatenate([c_g, c_g], axis=1)
        cb = lax.dot_general(c_g.astype(BF16), bm[:, gs].astype(BF16), (((1,), (1,)), ((), ())),
                             preferred_element_type=F32)
        bt_g = bm_t[gs, :]
        y_pairs = []
        for k in range(HEADS_PER_GROUP // 2):
            pair = (g * HEADS_PER_GROUP) // 2 + k
            x_pair = xs[:, pair * LANES:(pair + 1) * LANES]
            s_pair = state_ref[q, pair]
            s_b = s_pair.astype(BF16)
            heads = (2 * pair, 2 * pair + 1)
            a_l = [jnp.broadcast_to(acum[:, h:h + 1], (CHUNK, CHUNK)) for h in heads]
            cdec = (c2 * jnp.exp2(jnp.where(first_half, a_l[0], a_l[1]))).astype(BF16)
            rhs = [jnp.concatenate([x_pair, s_b, zeros_n], axis=0),
                   jnp.concatenate([x_pair, zeros_n, s_b], axis=0)]
            res, lhs_s, cds = [], [], []
            for i, h in enumerate(heads):
                a_s, dt_s = as_t[h:h + 1, :], dt_t[h:h + 1, :]
                mix = cb * jnp.exp2(jnp.where(causal, a_l[i] - a_s, -jnp.inf)) * dt_s
                lhs = jnp.concatenate([mix.astype(BF16), cdec], axis=1)
                res.append(jnp.dot(lhs, rhs[i], preferred_element_type=F32))
                a_last = acum[CHUNK - 1:CHUNK, h:h + 1]
                lhs_s.append(bt_g * (jnp.exp2(a_last - a_s) * dt_s))
                cds.append(jnp.exp2(a_last))
            y_pairs.append(jnp.where(first_half, res[0], res[1]))
            updates.append((pair, jnp.concatenate(lhs_s, axis=0).astype(BF16), x_pair,
                            jnp.where(first_half_row, cds[0], cds[1]) * s_pair))

        ns = slice(g * NORM_GROUP, (g + 1) * NORM_GROUP)
        y_g = jnp.concatenate(y_pairs, axis=1)
        y_g = (y_g + dskip[:, ns] * xs[:, ns].astype(F32)) * gate_ref[q, :, ns]
        y_ref[q, :, ns] = _rms(y_g, nw[:, ns]).astype(BF16)

    for pair, lhs_s, x_pair, decayed in updates:
        upd = jnp.dot(lhs_s, x_pair, preferred_element_type=F32)
        state_ref[q, pair] = decayed + jnp.where(first_half_n, upd[:D_STATE], upd[D_STATE:])


def _ssd_prompt_kernel(xs_ref, bc_ref, bmt_ref, acum_ref, ast_ref, dtt_ref, gate_ref, dskip_ref, nw_ref,
                       y_ref, hfin_ref, state_ref):
    c = pl.program_id(1)

    @pl.when(c == 0)
    def _():
        state_ref[...] = jnp.zeros(state_ref.shape, F32)

    for q in range(SSD_SEQS):
        _ssd_chunk(q, xs_ref, bc_ref, bmt_ref, acum_ref, ast_ref, dtt_ref, gate_ref, dskip_ref[...], nw_ref[...],
                   y_ref, state_ref)

    @pl.when(c == pl.num_programs(1) - 1)
    def _():
        for q in range(SSD_SEQS):
            for pair in range(HEAD_PAIRS):
                t = jnp.concatenate([state_ref[q, pair], jnp.zeros((LANES - D_STATE, LANES), F32)], axis=0).T
                hfin_ref[q, pair * LANES:(pair + 1) * LANES, :] = t[:, :D_STATE]


def _ssd_prompt(xs, bc, bmt, acum, ast, dtt, gate, dskip, nw, nseq, seq):
    nc = seq // CHUNK
    blk = lambda width, rows=CHUNK: pl.BlockSpec((SSD_SEQS, rows, width), lambda b, c: (b, c, 0))
    return pl.pallas_call(
        _ssd_prompt_kernel,
        grid=(nseq // SSD_SEQS, nc),
        in_specs=[blk(SSD_WIDTH), blk(2 * BC_WIDTH), blk(LANES, BC_WIDTH), blk(DT_PAD), blk(DT_PAD), blk(DT_PAD),
                  blk(SSD_WIDTH),
                  _const_spec((1, SSD_WIDTH)),
                  _const_spec((1, SSD_WIDTH))],
        out_specs=[blk(SSD_WIDTH),
                   pl.BlockSpec((SSD_SEQS, SSD_WIDTH, D_STATE), lambda b, c: (b, 0, 0))],
        out_shape=[jax.ShapeDtypeStruct((nseq, seq, SSD_WIDTH), BF16),
                   jax.ShapeDtypeStruct((nseq, SSD_WIDTH, D_STATE), F32)],
        scratch_shapes=[pltpu.VMEM((SSD_SEQS, HEAD_PAIRS, D_STATE, LANES), F32)],
        compiler_params=pltpu.CompilerParams(dimension_semantics=("arbitrary", "arbitrary"),
                                             vmem_limit_bytes=VMEM_LIMIT),
        name="ssd_prompt",
    )(xs, bc, bmt, acum, ast, dtt, gate, dskip, nw)


STEP_HEADS = 4


def _step_mix_kernel(xbc_ref, z_ref, dt_ref, v_ref, cs_ref, ps_ref, h_ref,
                     cw_ref, cb_ref, dtb_ref, alog_ref, dskip_ref, nw_ref, pw_ref, psc_ref,
                     hout_ref, y_ref, yp_ref, cso_ref, pso_ref,
                     xs_ref, xdt_ref, bt_ref, ct_ref, dect_ref, yt_ref, *, pos0):
    h = pl.program_id(0)
    nprev = POOL_MAX - 1

    @pl.when(h == 0)
    def _prepare():
        xnew = xbc_ref[...]
        conv = cb_ref[...] + xnew * cw_ref[CONV_W - 1:CONV_W, :]
        for k in range(CONV_W - 1):
            conv = conv + cs_ref[k] * cw_ref[k:k + 1, :]
        xbc = _silu(conv)
        xs = xbc[:, :SSD_WIDTH]
        xs_ref[...] = xs
        dt = _softplus(dt_ref[...] + dtb_ref[...])
        dect_ref[...] = jnp.exp(dt * (-jnp.exp(alog_ref[...]))).T
        dt_t = dt.T
        for hh in range(N_HEADS):
            blk = xs[:, (hh // 2) * LANES:(hh // 2 + 1) * LANES].T
            half = blk[(hh % 2) * HEAD_DIM:(hh % 2 + 1) * HEAD_DIM, :]
            xdt_ref[hh * HEAD_DIM:(hh + 1) * HEAD_DIM, :] = half * dt_t[hh:hh + 1, :]
        for i in range(BC_WIDTH // LANES):
            sl = slice(i * LANES, (i + 1) * LANES)
            bt_ref[sl, :] = xbc[:, SSD_WIDTH + i * LANES:SSD_WIDTH + (i + 1) * LANES].T
            ct_ref[sl, :] = xbc[:, SSD_WIDTH + BC_WIDTH + i * LANES:SSD_WIDTH + BC_WIDTH + (i + 1) * LANES].T
        for k in range(CONV_W - 2):
            cso_ref[k] = cs_ref[k + 1]
        cso_ref[CONV_W - 2] = xnew
        for k in range(nprev - 1):
            pso_ref[k] = ps_ref[k + 1]
        pso_ref[nprev - 1] = v_ref[...]

    for i in range(STEP_HEADS):
        head = h * STEP_HEADS + i
        g = lax.div(head, HEADS_PER_GROUP)
        hrow = pl.multiple_of(head * HEAD_DIM, HEAD_DIM)
        grow = pl.multiple_of(g * D_STATE, D_STATE)
        b_t = bt_ref[pl.ds(grow, D_STATE), :]
        c_t = ct_ref[pl.ds(grow, D_STATE), :]
        dec = dect_ref[pl.ds(head, 1), :]

        def per_p(p, carry, i=i, hrow=hrow, b_t=b_t, c_t=c_t, dec=dec):
            xrow = xdt_ref[pl.ds(hrow + p, 1), :]
            hn = h_ref[i, p] * dec + xrow * b_t
            hout_ref[i, p] = hn
            yt_ref[pl.ds(hrow + p, 1), :] = jnp.sum(hn * c_t, axis=0, keepdims=True)
            return carry

        lax.fori_loop(0, HEAD_DIM, per_p, 0, unroll=4)

    @pl.when(h == pl.num_programs(0) - 1)
    def _finish():
        y0 = jnp.concatenate([yt_ref[i * LANES:(i + 1) * LANES, :].T for i in range(SSD_WIDTH // LANES)], axis=1)
        y = _gate_and_norm(y0, xs_ref[...], _silu(z_ref[...]), dskip_ref[...], nw_ref[...])
        y_ref[...] = y.astype(BF16)
        v = v_ref[...]
        sums, cnts = [], []
        for gi, w in enumerate(POOL_WINDOWS):
            sl = slice(gi * POOL_GROUP_DIM, (gi + 1) * POOL_GROUP_DIM)
            acc = v[:, sl]
            for j in range(1, w):
                acc = acc + ps_ref[nprev - j][:, sl]
            sums.append(acc)
            cnts.append(float(min(pos0 + 1, w)))
        yp_ref[...] = _pool_project(sums, v, cnts, pw_ref, psc_ref[...]).astype(BF16)


def _step_mix(xbc, z, dt, v, cs_t, ps_t, h_t, cw, cb, dtb, alog, dskip, nw, pw, psc, pos0):
    n = xbc.shape[0]
    nprev = POOL_MAX - 1
    full = lambda shape: pl.BlockSpec(shape, lambda i: (0,) * len(shape))
    hblk = pl.BlockSpec((STEP_HEADS, HEAD_DIM, D_STATE, n), lambda i: (i, 0, 0, 0))
    return pl.pallas_call(
        functools.partial(_step_mix_kernel, pos0=pos0),
        grid=(N_HEADS // STEP_HEADS,),
        in_specs=[full((n, CONV_DIM)), full((n, SSD_WIDTH)), full((n, DT_PAD)), full((n, POOL_WIDTH)),
                  _const_spec((CONV_W - 1, n, CONV_DIM)), _const_spec((nprev, n, POOL_WIDTH)), hblk,
                  full((CONV_W, CONV_DIM)), full((1, CONV_DIM)), full((1, DT_PAD)), full((1, DT_PAD)),
                  full((1, SSD_WIDTH)), full((1, SSD_WIDTH)),
                  full((len(POOL_WINDOWS), POOL_GROUP_DIM, POOL_GROUP_DIM)), full((1, POOL_WIDTH))],
        out_specs=[hblk, full((n, SSD_WIDTH)), full((n, POOL_WIDTH)),
                   full((CONV_W - 1, n, CONV_DIM)), full((nprev, n, POOL_WIDTH))],
        out_shape=[jax.ShapeDtypeStruct(h_t.shape, F32),
                   jax.ShapeDtypeStruct((n, SSD_WIDTH), BF16),
                   jax.ShapeDtypeStruct((n, POOL_WIDTH), BF16),
                   jax.ShapeDtypeStruct((CONV_W - 1, n, CONV_DIM), F32),
                   jax.ShapeDtypeStruct((nprev, n, POOL_WIDTH), F32)],
        scratch_shapes=[pltpu.VMEM((n, SSD_WIDTH), F32),
                        pltpu.VMEM((SSD_WIDTH, n), F32),
                        pltpu.VMEM((BC_WIDTH, n), F32),
                        pltpu.VMEM((BC_WIDTH, n), F32),
                        pltpu.VMEM((DT_PAD, n), F32),
                        pltpu.VMEM((SSD_WIDTH, n), F32)],
        compiler_params=pltpu.CompilerParams(dimension_semantics=("arbitrary",),
                                             vmem_limit_bytes=VMEM_LIMIT),
        name="step_mix",
    )(xbc, z, dt, v, cs_t, ps_t, h_t, cw, cb, dtb, alog, dskip, nw, pw, psc)


FFN_CHUNK = 256


def _outffn_kernel(*refs, tm, decode):
    if decode:
        (x_ref, ys_ref, yp_ref, st_ref, wout_ref, n2_ref, wup_ref, fcw_ref, fcb_ref, wdn_ref, fn_ref,
         y_ref, sto_ref) = refs
    else:
        (x_ref, ys_ref, yp_ref, wout_ref, n2_ref, wup_ref, fcw_ref, fcb_ref, wdn_ref, fn_ref,
         y_ref, ust_ref, carry_ref) = refs
        j = pl.program_id(1)

        @pl.when(j == 0)
        def _():
            carry_ref[...] = jnp.zeros(carry_ref.shape, F32)

    ymix = jnp.concatenate([ys_ref[...], yp_ref[...]], axis=1)
    x1 = x_ref[...] + jnp.dot(ymix, wout_ref[...], preferred_element_type=F32)
    hn = _rms(x1, n2_ref[...]).astype(BF16)
    acc = x1
    nchunk = D_FF // FFN_CHUNK

    def col_slices(k):
        return [slice(base + k * FFN_CHUNK, base + (k + 1) * FFN_CHUNK) for base in (0, D_FF)]

    def up_proj(k):
        return [jnp.dot(hn, wup_ref[:, cs], preferred_element_type=F32) for cs in col_slices(k)]

    u_next = up_proj(0)
    for k in range(nchunk):
        u_cur = u_next
        if k + 1 < nchunk:
            u_next = up_proj(k + 1)
        halves = []
        for u, cs in zip(u_cur, col_slices(k)):
            if decode:
                prev2, prev1 = st_ref[:, 0, cs], st_ref[:, 1, cs]
                sto_ref[:, 0, cs] = prev1
                sto_ref[:, 1, cs] = u
            else:
                prev8 = carry_ref[:, cs]
                prev1 = _delay_rows(u, prev8, 1)
                prev2 = _delay_rows(u, prev8, 2)
                carry_ref[:, cs] = u[tm - SUBLANES:tm]
            halves.append(fcb_ref[:, cs] + prev2 * fcw_ref[0:1, cs] + prev1 * fcw_ref[1:2, cs]
                          + u * fcw_ref[2:3, cs])
        act = (_silu_half(halves[0]) * halves[1]).astype(BF16)
        acc = acc + jnp.dot(act, wdn_ref[k * FFN_CHUNK:(k + 1) * FFN_CHUNK, :], preferred_element_type=F32)
    y_ref[...] = _rms(acc, fn_ref[...])
    if not decode:
        @pl.when(j == pl.num_programs(1) - 1)
        def _():
            ust_ref[0] = carry_ref[SUBLANES - (FFN_CONV_W - 1):SUBLANES, :]


def _outffn_prompt(x2d, ys, yp, wout, n2, wup, fcw, fcb, wdn, fn, nseq, seq, tm):
    nj = seq // tm
    row = lambda b, j: (b * nj + j, 0)
    return pl.pallas_call(
        functools.partial(_outffn_kernel, tm=tm, decode=False),
        grid=(nseq, nj),
        in_specs=[pl.BlockSpec((tm, D_MODEL), row),
                  pl.BlockSpec((tm, SSD_WIDTH), row),
                  pl.BlockSpec((tm, POOL_WIDTH), row),
                  _const_spec((D_MIX, D_MODEL)),
                  _const_spec((1, D_MODEL)),
                  _const_spec((D_MODEL, 2 * D_FF)),
                  _const_spec((FFN_CONV_W, 2 * D_FF)),
                  _const_spec((1, 2 * D_FF)),
                  _const_spec((D_FF, D_MODEL)),
                  _const_spec((1, D_MODEL))],
        out_specs=[pl.BlockSpec((tm, D_MODEL), row),
                   pl.BlockSpec((1, FFN_CONV_W - 1, 2 * D_FF), lambda b, j: (b, 0, 0))],
        out_shape=[jax.ShapeDtypeStruct((nseq * seq, D_MODEL), F32),
                   jax.ShapeDtypeStruct((nseq, FFN_CONV_W - 1, 2 * D_FF), F32)],
        scratch_shapes=[pltpu.VMEM((SUBLANES, 2 * D_FF), F32)],
        compiler_params=pltpu.CompilerParams(dimension_semantics=("arbitrary", "arbitrary"),
                                             vmem_limit_bytes=VMEM_LIMIT),
        name="outffn_prompt",
    )(x2d, ys, yp, wout, n2, wup, fcw, fcb, wdn, fn)


def _outffn_decode(x2d, ys, yp, ffn_state, wout, n2, wup, fcw, fcb, wdn, fn):
    n = x2d.shape[0]
    full = lambda shape: pl.BlockSpec(shape, lambda i: (0,) * len(shape))
    return pl.pallas_call(
        functools.partial(_outffn_kernel, tm=n, decode=True),
        grid=(1,),
        in_specs=[full((n, D_MODEL)), full((n, SSD_WIDTH)), full((n, POOL_WIDTH)), full(ffn_state.shape),
                  _const_spec((D_MIX, D_MODEL)),
                  _const_spec((1, D_MODEL)),
                  _const_spec((D_MODEL, 2 * D_FF)),
                  _const_spec((FFN_CONV_W, 2 * D_FF)),
                  _const_spec((1, 2 * D_FF)),
                  _const_spec((D_FF, D_MODEL)),
                  _const_spec((1, D_MODEL))],
        out_specs=[full((n, D_MODEL)), full(ffn_state.shape)],
        out_shape=[jax.ShapeDtypeStruct((n, D_MODEL), F32),
                   jax.ShapeDtypeStruct(ffn_state.shape, F32)],
        compiler_params=pltpu.CompilerParams(dimension_semantics=("arbitrary",),
                                             vmem_limit_bytes=VMEM_LIMIT),
        name="outffn_decode",
    )(x2d, ys, yp, ffn_state, wout, n2, wup, fcw, fcb, wdn, fn)


def kernel(x_prompt, x_sample, state_ssm, state_conv, state_pool, state_ffn_conv, norm1_w, w_in, conv_w, conv_b,
           dt_bias, a_log, d_skip, ssd_norm_w, pool_w, pool_scale, w_out, norm2_w, w_up, ffn_conv_w, ffn_conv_b,
           w_down, final_norm_w):
    assert w_in.shape[0] == 1, "single-layer model"
    nseq, seq, _ = x_prompt.shape
    nsmp, one, _ = x_sample.shape
    assert one == 1 and seq % CHUNK == 0 and nsmp == LANES

    wi = w_in[0]
    w_dt = jnp.pad(wi[:, SSD_WIDTH + CONV_DIM:SSD_WIDTH + CONV_DIM + N_HEADS], ((0, 0), (0, DT_PAD - N_HEADS)))
    w_cat = jnp.concatenate([wi[:, :SSD_WIDTH], wi[:, SSD_WIDTH:SSD_WIDTH + CONV_DIM],
                             wi[:, SSD_WIDTH + CONV_DIM + N_HEADS:], w_dt], axis=1).astype(BF16)
    n1 = norm1_w[0][None]
    cw, cb = conv_w[0], conv_b[0][None]
    dtb = jnp.pad(dt_bias[0], (0, DT_PAD - N_HEADS))[None]
    alog = jnp.pad(a_log[0], (0, DT_PAD - N_HEADS))[None]
    dskip = jnp.repeat(d_skip[0], HEAD_DIM)[None]
    nw = ssd_norm_w[0][None]
    pw = pool_w[0].astype(BF16)
    psc = pool_scale[0][None]
    wout = w_out[0].astype(BF16)
    n2 = norm2_w[0][None]
    wup = w_up[0].astype(BF16)
    gate_half = jnp.where(jnp.arange(2 * D_FF) < D_FF, 0.5, 1.0).astype(F32)
    fcw, fcb = ffn_conv_w[0] * gate_half, (ffn_conv_b[0] * gate_half)[None]
    wdn = w_down[0].astype(BF16)
    fn = final_norm_w[None]

    xp = x_prompt.reshape(nseq * seq, D_MODEL)
    xs_p, bc_p, bmt_p, acum_p, ast_p, dtt_p, gate_p, yp_p, ctail, vtail = _inproj_prompt(
        xp, n1, w_cat, 0.5 * cw, 0.5 * cb, dtb, alog, pw, psc, nseq, seq, tm=1024)
    per_seq = lambda a: a.reshape(nseq, -1, a.shape[-1])
    ys_p, hfin = _ssd_prompt(per_seq(xs_p), per_seq(bc_p), per_seq(bmt_p), per_seq(acum_p), per_seq(ast_p),
                             per_seq(dtt_p), per_seq(gate_p), dskip, nw, nseq, seq)
    ys_p = ys_p.reshape(nseq * seq, SSD_WIDTH)
    y_p, ffn_p = _outffn_prompt(xp, ys_p, yp_p, wout, n2, wup, fcw, fcb, wdn, fn, nseq, seq, tm=256)

    xs_in = x_sample.reshape(nsmp, D_MODEL)
    z_s, xbc_s, v_s, dt_s = _inproj(xs_in, n1, w_cat, tm=nsmp)
    h_t = jnp.transpose(state_ssm[0], (1, 2, 3, 0))
    cs_t = jnp.transpose(state_conv[0], (1, 0, 2))
    ps_t = jnp.transpose(state_pool[0], (1, 0, 2))
    hnew_t, ys_s, yp_s, cs_new, ps_new = _step_mix(xbc_s, z_s, dt_s, v_s, cs_t, ps_t, h_t, cw, cb, dtb, alog,
                                                   dskip, nw, pw, psc, PAST_LEN)
    y_s, ffn_new = _outffn_decode(xs_in, ys_s, yp_s, state_ffn_conv[0], wout, n2, wup, fcw, fcb, wdn, fn)

    return (y_p.reshape(nseq, seq, D_MODEL),
            y_s.reshape(nsmp, 1, D_MODEL),
            hfin.reshape(nseq, N_HEADS, HEAD_DIM, D_STATE)[None],
            jnp.transpose(hnew_t, (3, 0, 1, 2))[None],
            ctail[:, SUBLANES - (CONV_W - 1):][None],
            jnp.transpose(cs_new, (1, 0, 2))[None],
            vtail[:, 2 * SUBLANES - (POOL_MAX - 1):][None],
            jnp.transpose(ps_new, (1, 0, 2))[None],
            ffn_p[None],
            ffn_new[None])
```

```python
import functools

import jax
import jax.numpy as jnp
from jax import lax
from jax.experimental import pallas as pl
from jax.experimental.pallas import tpu as pltpu

F32 = jnp.float32
BF16 = jnp.bfloat16

D_MODEL = 1024
SSD_WIDTH = 1536
HEAD_DIM = 64
N_HEADS = SSD_WIDTH // HEAD_DIM
N_GROUPS = 4
HEADS_PER_GROUP = N_HEADS // N_GROUPS
D_STATE = 64
BC_WIDTH = N_GROUPS * D_STATE
CONV_DIM = SSD_WIDTH + 2 * BC_WIDTH
CONV_W = 4
CHUNK = 128
POOL_WIDTH = 512
POOL_WINDOWS = (2, 4, 8, 16)
POOL_GROUP_DIM = POOL_WIDTH // len(POOL_WINDOWS)
POOL_MAX = max(POOL_WINDOWS)
D_MIX = SSD_WIDTH + POOL_WIDTH
D_FF = 2816
FFN_CONV_W = 3
NORM_GROUP = SSD_WIDTH // N_GROUPS
EPS = 1e-6
LOG2_E = 1.4426950408889634
PAST_LEN = 16384

LANES = 128
SUBLANES = 8
DT_PAD = LANES
VMEM_LIMIT = 56 * 1024 * 1024
HEAD_PAIRS = N_HEADS // 2

_Z0, _Z1 = 0, SSD_WIDTH
_X0, _X1 = _Z1, _Z1 + CONV_DIM
_V0, _V1 = _X1, _X1 + POOL_WIDTH
_T0, _T1 = _V1, _V1 + DT_PAD


def _rms(x, w):
    ms = jnp.mean(x * x, axis=-1, keepdims=True)
    return x * lax.rsqrt(ms + EPS) * w


def _silu_half(h):
    return h + h * jnp.tanh(h)


def _silu(x):
    return _silu_half(0.5 * x)


def _softplus(x):
    return jnp.maximum(x, 0.0) + jnp.log1p(jnp.exp(-jnp.abs(x)))


def _const_spec(shape):
    nd = len(shape)
    return pl.BlockSpec(shape, lambda *_: (0,) * nd, pipeline_mode=pl.Buffered(1))


def _delay_rows(u, prev8, s):
    if s == SUBLANES:
        return jnp.concatenate([prev8, u[:u.shape[0] - SUBLANES]], axis=0)
    r = pltpu.roll(u, s, axis=0)
    c = pltpu.roll(prev8, s, axis=0)
    row = lax.broadcasted_iota(jnp.int32, prev8.shape, 0)
    top = jnp.where(row < s, c, r[0:SUBLANES])
    return jnp.concatenate([top, r[SUBLANES:]], axis=0)


def _gate_and_norm(y, xs, gate, dskip, nw):
    y = (y + dskip * xs) * gate
    outs = []
    for g in range(N_GROUPS):
        sl = slice(g * NORM_GROUP, (g + 1) * NORM_GROUP)
        outs.append(_rms(y[:, sl], nw[:, sl]))
    return jnp.concatenate(outs, axis=1)


def _pool_project(win_sums, v, cnts, pw_ref, pscale):
    outs = []
    for g in range(len(POOL_WINDOWS)):
        sl = slice(g * POOL_GROUP_DIM, (g + 1) * POOL_GROUP_DIM)
        m = win_sums[g] * (1.0 / cnts[g]) - v[:, sl]
        outs.append(jnp.dot(m.astype(BF16), pw_ref[g], preferred_element_type=F32))
    return jnp.concatenate(outs, axis=1) * pscale


def _cumsum_rows(x):
    n = x.shape[0]
    row = lax.broadcasted_iota(jnp.int32, x.shape, 0)
    k = 1
    while k < n:
        x = x + jnp.where(row >= k, pltpu.roll(x, k, axis=0), 0.0)
        k *= 2
    return x


def _inproj_kernel(*refs, tm, prompt, pos0):
    if prompt:
        (x_ref, nw_ref, w_ref, cw_ref, cb_ref, dtb_ref, alog_ref, pw_ref, psc_ref,
         xs_ref, bc_ref, bmt_ref, acum_ref, ast_ref, dtt_ref, gate_ref, yp_ref, ctail_ref, vtail_ref,
         cc_ref, vc_ref, s2c_ref, s4c_ref, s8c_ref, oc_ref) = refs
    else:
        x_ref, nw_ref, w_ref, z_ref, xbc_ref, v_ref, dt_ref = refs
    hn = _rms(x_ref[...], nw_ref[...]).astype(BF16)

    def mm(lo, hi):
        return jnp.dot(hn, w_ref[:, lo:hi], preferred_element_type=F32)

    if not prompt:
        z_ref[...] = mm(_Z0, _Z1)
        xbc_ref[...] = mm(_X0, _X1)
        v_ref[...] = mm(_V0, _V1)
        dt_ref[...] = mm(_T0, _T1)
        return

    j = pl.program_id(1)
    carries = (cc_ref, vc_ref, s2c_ref, s4c_ref, s8c_ref)

    @pl.when(j == 0)
    def _():
        for r in carries + (oc_ref,):
            r[...] = jnp.zeros(r.shape, F32)

    last = j == pl.num_programs(1) - 1

    def conv_stage(lo, hi):
        def fn(xbc):
            assert CONV_W == 4
            w0, w1, w2, w3 = (cw_ref[k:k + 1, lo:hi] for k in range(CONV_W))
            x2 = _delay_rows(xbc, cc_ref[:, lo:hi], 2)
            odd = w2 * xbc + w0 * x2
            conv = cb_ref[:, lo:hi] + (w3 * xbc + w1 * x2) + _delay_rows(odd, oc_ref[:, lo:hi], 1)
            cc_ref[:, lo:hi] = xbc[tm - SUBLANES:tm]
            oc_ref[:, lo:hi] = odd[tm - SUBLANES:tm]
            act = _silu_half(conv)
            if hi <= SSD_WIDTH:
                xs_ref[:, lo:hi] = act.astype(BF16)
            else:
                bc_ref[...] = act[:, BC_WIDTH:]
                for c in range(tm // CHUNK):
                    for i in range(BC_WIDTH // LANES):
                        blk = act[c * CHUNK:(c + 1) * CHUNK, i * LANES:(i + 1) * LANES]
                        bmt_ref[c * BC_WIDTH + i * LANES:c * BC_WIDTH + (i + 1) * LANES, :] = blk.T
        return fn

    def pool_stage(v):
        s2 = v + _delay_rows(v, vc_ref[...], 1)
        s4 = s2 + _delay_rows(s2, s2c_ref[...], 2)
        s8 = s4 + _delay_rows(s4, s4c_ref[...], 4)
        s16 = s8 + _delay_rows(s8, s8c_ref[...], 8)
        for r, val in zip(carries[1:], (v, s2, s4, s8)):
            r[...] = val[tm - SUBLANES:tm]
        pos = pos0 + j * tm + lax.broadcasted_iota(jnp.int32, (tm, 1), 0)
        sums, cnts = [], []
        for g, (w, s) in enumerate(zip(POOL_WINDOWS, (s2, s4, s8, s16))):
            sums.append(s[:, g * POOL_GROUP_DIM:(g + 1) * POOL_GROUP_DIM])
            cnts.append(jnp.minimum(pos + 1, w).astype(F32))
        yp_ref[...] = _pool_project(sums, v, cnts, pw_ref, psc_ref[...]).astype(BF16)

        @pl.when(last)
        def _():
            vtail_ref[0] = v[tm - 2 * SUBLANES:tm]

    def gate_stage(lo, hi):
        def fn(z):
            gate_ref[:, lo:hi] = _silu(z)
        return fn

    def dt_stage(dt_raw):
        dt = _softplus(dt_raw + dtb_ref[...])
        neg_a = -jnp.exp(alog_ref[...])
        for c in range(tm // CHUNK):
            rows = slice(c * CHUNK, (c + 1) * CHUNK)
            acum = _cumsum_rows(dt[rows] * neg_a) * LOG2_E
            acum_ref[rows, :] = acum
            ast_ref[rows, :] = acum.T
            dtt_ref[rows, :] = dt[rows].T

    stages = [((_Z0, _Z1), gate_stage(0, SSD_WIDTH)),
              ((_T0, _T1), dt_stage),
              ((_X0, _X0 + SSD_WIDTH), conv_stage(0, SSD_WIDTH)),
              ((_X0 + SSD_WIDTH, _X1), conv_stage(SSD_WIDTH, CONV_DIM)),
              ((_V0, _V1), pool_stage)]
    for cols, fn in stages:
        fn(mm(*cols))

    @pl.when(last)
    def _():
        ctail_ref[0] = cc_ref[...]


def _inproj(x2d, norm_w, w_cat, tm):
    m = x2d.shape[0]
    row = lambda i: (i, 0)
    return pl.pallas_call(
        functools.partial(_inproj_kernel, tm=tm, prompt=False, pos0=0),
        grid=(m // tm,),
        in_specs=[pl.BlockSpec((tm, D_MODEL), row),
                  _const_spec((1, D_MODEL)),
                  _const_spec((D_MODEL, _T1))],
        out_specs=[pl.BlockSpec((tm, SSD_WIDTH), row),
                   pl.BlockSpec((tm, CONV_DIM), row),
                   pl.BlockSpec((tm, POOL_WIDTH), row),
                   pl.BlockSpec((tm, DT_PAD), row)],
        out_shape=[jax.ShapeDtypeStruct((m, SSD_WIDTH), F32),
                   jax.ShapeDtypeStruct((m, CONV_DIM), F32),
                   jax.ShapeDtypeStruct((m, POOL_WIDTH), F32),
                   jax.ShapeDtypeStruct((m, DT_PAD), F32)],
        compiler_params=pltpu.CompilerParams(dimension_semantics=("arbitrary",),
                                             vmem_limit_bytes=VMEM_LIMIT),
        name="inproj",
    )(x2d, norm_w, w_cat)


def _inproj_prompt(x2d, norm_w, w_cat, cw, cb, dtb, alog, pw, psc, nseq, seq, tm):
    nj = seq // tm
    m = nseq * seq
    row = lambda b, j: (b * nj + j, 0)
    per_seq = lambda b, j: (b, 0, 0)
    return pl.pallas_call(
        functools.partial(_inproj_kernel, tm=tm, prompt=True, pos0=0),
        grid=(nseq, nj),
        in_specs=[pl.BlockSpec((tm, D_MODEL), row),
                  _const_spec((1, D_MODEL)),
                  _const_spec((D_MODEL, _T1)),
                  _const_spec((CONV_W, CONV_DIM)),
                  _const_spec((1, CONV_DIM)),
                  _const_spec((1, DT_PAD)),
                  _const_spec((1, DT_PAD)),
                  _const_spec((len(POOL_WINDOWS), POOL_GROUP_DIM, POOL_GROUP_DIM)),
                  _const_spec((1, POOL_WIDTH))],
        out_specs=[pl.BlockSpec((tm, SSD_WIDTH), row),
                   pl.BlockSpec((tm, BC_WIDTH), row),
                   pl.BlockSpec((2 * tm, LANES), row),
                   pl.BlockSpec((tm, DT_PAD), row),
                   pl.BlockSpec((tm, DT_PAD), row),
                   pl.BlockSpec((tm, DT_PAD), row),
                   pl.BlockSpec((tm, SSD_WIDTH), row),
                   pl.BlockSpec((tm, POOL_WIDTH), row),
                   pl.BlockSpec((1, SUBLANES, CONV_DIM), per_seq),
                   pl.BlockSpec((1, 2 * SUBLANES, POOL_WIDTH), per_seq)],
        out_shape=[jax.ShapeDtypeStruct((m, SSD_WIDTH), BF16),
                   jax.ShapeDtypeStruct((m, BC_WIDTH), F32),
                   jax.ShapeDtypeStruct((2 * m, LANES), F32),
                   jax.ShapeDtypeStruct((m, DT_PAD), F32),
                   jax.ShapeDtypeStruct((m, DT_PAD), F32),
                   jax.ShapeDtypeStruct((m, DT_PAD), F32),
                   jax.ShapeDtypeStruct((m, SSD_WIDTH), F32),
                   jax.ShapeDtypeStruct((m, POOL_WIDTH), BF16),
                   jax.ShapeDtypeStruct((nseq, SUBLANES, CONV_DIM), F32),
                   jax.ShapeDtypeStruct((nseq, 2 * SUBLANES, POOL_WIDTH), F32)],
        scratch_shapes=[pltpu.VMEM((SUBLANES, CONV_DIM), F32),
                        pltpu.VMEM((SUBLANES, POOL_WIDTH), F32),
                        pltpu.VMEM((SUBLANES, POOL_WIDTH), F32),
                        pltpu.VMEM((SUBLANES, POOL_WIDTH), F32),
                        pltpu.VMEM((SUBLANES, POOL_WIDTH), F32),
                        pltpu.VMEM((SUBLANES, CONV_DIM), F32)],
        compiler_params=pltpu.CompilerParams(dimension_semantics=("arbitrary", "arbitrary"),
                                             vmem_limit_bytes=VMEM_LIMIT),
        name="inproj_prompt",
    )(x2d, norm_w, w_cat, cw, cb, dtb, alog, pw, psc)


SSD_SEQS = 8


def _ssd_chunk(q, xs_ref, bc_ref, bmt_ref, acum_ref, ast_ref, dtt_ref, gate_ref, dskip, nw, y_ref, state_ref):
    xs = xs_ref[q]
    cm = bc_ref[q]
    bm_t = bmt_ref[q]
    acum = acum_ref[q]
    as_t = ast_ref[q]
    dt_t = dtt_ref[q]
    li = lax.broadcasted_iota(jnp.int32, (CHUNK, CHUNK), 0)
    si = lax.broadcasted_iota(jnp.int32, (CHUNK, CHUNK), 1)
    causal = li >= si
    first_half = si < HEAD_DIM
    first_half_n = lax.broadcasted_iota(jnp.int32, (D_STATE, LANES), 1) < HEAD_DIM
    first_half_row = lax.broadcasted_iota(jnp.int32, (1, LANES), 1) < HEAD_DIM
    zeros_n = jnp.zeros((D_STATE, LANES), BF16)
    updates = []

    for g in range(N_GROUPS):
        gs = slice(g * D_STATE, (g + 1) * D_STATE)
        c_g = cm[:, gs]
        c2 = jnp.concatenate([c_g, c_g], axis=1)
        bt_g = bm_t[gs, :]
        cb = jnp.dot(c_g.astype(BF16), bt_g.astype(BF16), preferred_element_type=F32)
        y_pairs = []
        for k in range(HEADS_PER_GROUP // 2):
            pair = (g * HEADS_PER_GROUP) // 2 + k
            x_pair = xs[:, pair * LANES:(pair + 1) * LANES]
            s_pair = state_ref[q, pair]
            s_b = s_pair.astype(BF16)
            heads = (2 * pair, 2 * pair + 1)
            a_l = [jnp.broadcast_to(acum[:, h:h + 1], (CHUNK, CHUNK)) for h in heads]
            cdec = (c2 * jnp.exp2(jnp.where(first_half, a_l[0], a_l[1]))).astype(BF16)
            rhs = [jnp.concatenate([x_pair, s_b, zeros_n], axis=0),
                   jnp.concatenate([x_pair, zeros_n, s_b], axis=0)]
            res, lhs_s, cds = [], [], []
            for i, h in enumerate(heads):
                a_s, dt_s = as_t[h:h + 1, :], dt_t[h:h + 1, :]
                mix = cb * jnp.exp2(jnp.where(causal, a_l[i] - a_s, -jnp.inf)) * dt_s
                lhs = jnp.concatenate([mix.astype(BF16), cdec], axis=1)
                res.append(jnp.dot(lhs, rhs[i], preferred_element_type=F32))
                a_last = acum[CHUNK - 1:CHUNK, h:h + 1]
                lhs_s.append(bt_g * (jnp.exp2(a_last - a_s) * dt_s))
                cds.append(jnp.exp2(a_last))
            y_pairs.append(jnp.where(first_half, res[0], res[1]))
            updates.append((pair, jnp.concatenate(lhs_s, axis=0).astype(BF16), x_pair,
                            jnp.where(first_half_row, cds[0], cds[1]) * s_pair))

        ns = slice(g * NORM_GROUP, (g + 1) * NORM_GROUP)
        y_g = jnp.concatenate(y_pairs, axis=1)
        y_g = (y_g + dskip[:, ns] * xs[:, ns].astype(F32)) * gate_ref[q, :, ns]
        y_ref[q, :, ns] = _rms(y_g, nw[:, ns]).astype(BF16)

    for pair, lhs_s, x_pair, decayed in updates:
        upd = jnp.dot(lhs_s, x_pair, preferred_element_type=F32)
        state_ref[q, pair] = decayed + jnp.where(first_half_n, upd[:D_STATE], upd[D_STATE:])


def _ssd_prompt_kernel(xs_ref, bc_ref, bmt_ref, acum_ref, ast_ref, dtt_ref, gate_ref, dskip_ref, nw_ref,
                       y_ref, hfin_ref, state_ref):
    c = pl.program_id(1)

    @pl.when(c == 0)
    def _():
        state_ref[...] = jnp.zeros(state_ref.shape, F32)

    for q in range(SSD_SEQS):
        _ssd_chunk(q, xs_ref, bc_ref, bmt_ref, acum_ref, ast_ref, dtt_ref, gate_ref, dskip_ref[...], nw_ref[...],
                   y_ref, state_ref)

    @pl.when(c == pl.num_programs(1) - 1)
    def _():
        for q in range(SSD_SEQS):
            for pair in range(HEAD_PAIRS):
                t = jnp.concatenate([state_ref[q, pair], jnp.zeros((LANES - D_STATE, LANES), F32)], axis=0).T
                hfin_ref[q, pair * LANES:(pair + 1) * LANES, :] = t[:, :D_STATE]


def _ssd_prompt(xs, bc, bmt, acum, ast, dtt, gate, dskip, nw, nseq, seq):
    nc = seq // CHUNK
    blk = lambda width, rows=CHUNK: pl.BlockSpec((SSD_SEQS, rows, width), lambda b, c: (b, c, 0))
    return pl.pallas_call(
        _ssd_prompt_kernel,
        grid=(nseq // SSD_SEQS, nc),
        in_specs=[blk(SSD_WIDTH), blk(BC_WIDTH), blk(LANES, BC_WIDTH), blk(DT_PAD), blk(DT_PAD), blk(DT_PAD),
                  blk(SSD_WIDTH),
                  _const_spec((1, SSD_WIDTH)),
                  _const_spec((1, SSD_WIDTH))],
        out_specs=[blk(SSD_WIDTH),
                   pl.BlockSpec((SSD_SEQS, SSD_WIDTH, D_STATE), lambda b, c: (b, 0, 0))],
        out_shape=[jax.ShapeDtypeStruct((nseq, seq, SSD_WIDTH), BF16),
                   jax.ShapeDtypeStruct((nseq, SSD_WIDTH, D_STATE), F32)],
        scratch_shapes=[pltpu.VMEM((SSD_SEQS, HEAD_PAIRS, D_STATE, LANES), F32)],
        compiler_params=pltpu.CompilerParams(dimension_semantics=("arbitrary", "arbitrary"),
                                             vmem_limit_bytes=VMEM_LIMIT),
        name="ssd_prompt",
    )(xs, bc, bmt, acum, ast, dtt, gate, dskip, nw)


STEP_HEADS = 4


def _step_mix_kernel(xbc_ref, z_ref, dt_ref, v_ref, cs_ref, ps_ref, h_ref,
                     cw_ref, cb_ref, dtb_ref, alog_ref, dskip_ref, nw_ref, pw_ref, psc_ref,
                     hout_ref, y_ref, yp_ref, cso_ref, pso_ref,
                     xs_ref, xdt_ref, bt_ref, ct_ref, dect_ref, yt_ref, *, pos0):
    h = pl.program_id(0)
    nprev = POOL_MAX - 1

    @pl.when(h == 0)
    def _prepare():
        xnew = xbc_ref[...]
        conv = cb_ref[...] + xnew * cw_ref[CONV_W - 1:CONV_W, :]
        for k in range(CONV_W - 1):
            conv = conv + cs_ref[k] * cw_ref[k:k + 1, :]
        xbc = _silu(conv)
        xs = xbc[:, :SSD_WIDTH]
        xs_ref[...] = xs
        dt = _softplus(dt_ref[...] + dtb_ref[...])
        dect_ref[...] = jnp.exp(dt * (-jnp.exp(alog_ref[...]))).T
        dt_t = dt.T
        for hh in range(N_HEADS):
            blk = xs[:, (hh // 2) * LANES:(hh // 2 + 1) * LANES].T
            half = blk[(hh % 2) * HEAD_DIM:(hh % 2 + 1) * HEAD_DIM, :]
            xdt_ref[hh * HEAD_DIM:(hh + 1) * HEAD_DIM, :] = half * dt_t[hh:hh + 1, :]
        for i in range(BC_WIDTH // LANES):
            sl = slice(i * LANES, (i + 1) * LANES)
            bt_ref[sl, :] = xbc[:, SSD_WIDTH + i * LANES:SSD_WIDTH + (i + 1) * LANES].T
            ct_ref[sl, :] = xbc[:, SSD_WIDTH + BC_WIDTH + i * LANES:SSD_WIDTH + BC_WIDTH + (i + 1) * LANES].T
        for k in range(CONV_W - 2):
            cso_ref[k] = cs_ref[k + 1]
        cso_ref[CONV_W - 2] = xnew
        for k in range(nprev - 1):
            pso_ref[k] = ps_ref[k + 1]
        pso_ref[nprev - 1] = v_ref[...]

    for i in range(STEP_HEADS):
        head = h * STEP_HEADS + i
        g = lax.div(head, HEADS_PER_GROUP)
        hrow = pl.multiple_of(head * HEAD_DIM, HEAD_DIM)
        grow = pl.multiple_of(g * D_STATE, D_STATE)
        b_t = bt_ref[pl.ds(grow, D_STATE), :]
        c_t = ct_ref[pl.ds(grow, D_STATE), :]
        dec = dect_ref[pl.ds(head, 1), :]

        def per_p(p, carry, i=i, hrow=hrow, b_t=b_t, c_t=c_t, dec=dec):
            xrow = xdt_ref[pl.ds(hrow + p, 1), :]
            hn = h_ref[i, p] * dec + xrow * b_t
            hout_ref[i, p] = hn
            yt_ref[pl.ds(hrow + p, 1), :] = jnp.sum(hn * c_t, axis=0, keepdims=True)
            return carry

        lax.fori_loop(0, HEAD_DIM, per_p, 0, unroll=4)

    @pl.when(h == pl.num_programs(0) - 1)
    def _finish():
        y0 = jnp.concatenate([yt_ref[i * LANES:(i + 1) * LANES, :].T for i in range(SSD_WIDTH // LANES)], axis=1)
        y = _gate_and_norm(y0, xs_ref[...], _silu(z_ref[...]), dskip_ref[...], nw_ref[...])
        y_ref[...] = y.astype(BF16)
        v = v_ref[...]
        sums, cnts = [], []
        for gi, w in enumerate(POOL_WINDOWS):
            sl = slice(gi * POOL_GROUP_DIM, (gi + 1) * POOL_GROUP_DIM)
            acc = v[:, sl]
            for j in range(1, w):
                acc = acc + ps_ref[nprev - j][:, sl]
            sums.append(acc)
            cnts.append(float(min(pos0 + 1, w)))
        yp_ref[...] = _pool_project(sums, v, cnts, pw_ref, psc_ref[...]).astype(BF16)


def _step_mix(xbc, z, dt, v, cs_t, ps_t, h_t, cw, cb, dtb, alog, dskip, nw, pw, psc, pos0):
    n = xbc.shape[0]
    nprev = POOL_MAX - 1
    full = lambda shape: pl.BlockSpec(shape, lambda i: (0,) * len(shape))
    hblk = pl.BlockSpec((STEP_HEADS, HEAD_DIM, D_STATE, n), lambda i: (i, 0, 0, 0))
    return pl.pallas_call(
        functools.partial(_step_mix_kernel, pos0=pos0),
        grid=(N_HEADS // STEP_HEADS,),
        in_specs=[full((n, CONV_DIM)), full((n, SSD_WIDTH)), full((n, DT_PAD)), full((n, POOL_WIDTH)),
                  _const_spec((CONV_W - 1, n, CONV_DIM)), _const_spec((nprev, n, POOL_WIDTH)), hblk,
                  full((CONV_W, CONV_DIM)), full((1, CONV_DIM)), full((1, DT_PAD)), full((1, DT_PAD)),
                  full((1, SSD_WIDTH)), full((1, SSD_WIDTH)),
                  full((len(POOL_WINDOWS), POOL_GROUP_DIM, POOL_GROUP_DIM)), full((1, POOL_WIDTH))],
        out_specs=[hblk, full((n, SSD_WIDTH)), full((n, POOL_WIDTH)),
                   full((CONV_W - 1, n, CONV_DIM)), full((nprev, n, POOL_WIDTH))],
        out_shape=[jax.ShapeDtypeStruct(h_t.shape, F32),
                   jax.ShapeDtypeStruct((n, SSD_WIDTH), BF16),
                   jax.ShapeDtypeStruct((n, POOL_WIDTH), BF16),
                   jax.ShapeDtypeStruct((CONV_W - 1, n, CONV_DIM), F32),
                   jax.ShapeDtypeStruct((nprev, n, POOL_WIDTH), F32)],
        scratch_shapes=[pltpu.VMEM((n, SSD_WIDTH), F32),
                        pltpu.VMEM((SSD_WIDTH, n), F32),
                        pltpu.VMEM((BC_WIDTH, n), F32),
                        pltpu.VMEM((BC_WIDTH, n), F32),
                        pltpu.VMEM((DT_PAD, n), F32),
                        pltpu.VMEM((SSD_WIDTH, n), F32)],
        compiler_params=pltpu.CompilerParams(dimension_semantics=("arbitrary",),
                                             vmem_limit_bytes=VMEM_LIMIT),
        name="step_mix",
    )(xbc, z, dt, v, cs_t, ps_t, h_t, cw, cb, dtb, alog, dskip, nw, pw, psc)


FFN_CHUNK = 256


def _outffn_kernel(*refs, tm, decode):
    if decode:
        (x_ref, ys_ref, yp_ref, st_ref, wout_ref, n2_ref, wup_ref, fcw_ref, fcb_ref, wdn_ref, fn_ref,
         y_ref, sto_ref) = refs
    else:
        (x_ref, ys_ref, yp_ref, wout_ref, n2_ref, wup_ref, fcw_ref, fcb_ref, wdn_ref, fn_ref,
         y_ref, ust_ref, carry_ref) = refs
        j = pl.program_id(1)

        @pl.when(j == 0)
        def _():
            carry_ref[...] = jnp.zeros(carry_ref.shape, F32)

    ymix = jnp.concatenate([ys_ref[...], yp_ref[...]], axis=1)
    x1 = x_ref[...] + jnp.dot(ymix, wout_ref[...], preferred_element_type=F32)
    hn = _rms(x1, n2_ref[...]).astype(BF16)
    acc = x1
    nchunk = D_FF // FFN_CHUNK

    def col_slices(k):
        return [slice(base + k * FFN_CHUNK, base + (k + 1) * FFN_CHUNK) for base in (0, D_FF)]

    def up_proj(k):
        return [jnp.dot(hn, wup_ref[:, cs], preferred_element_type=F32) for cs in col_slices(k)]

    u_next = up_proj(0)
    for k in range(nchunk):
        u_cur = u_next
        if k + 1 < nchunk:
            u_next = up_proj(k + 1)
        halves = []
        for u, cs in zip(u_cur, col_slices(k)):
            if decode:
                prev2, prev1 = st_ref[:, 0, cs], st_ref[:, 1, cs]
                sto_ref[:, 0, cs] = prev1
                sto_ref[:, 1, cs] = u
            else:
                prev8 = carry_ref[:, cs]
                prev1 = _delay_rows(u, prev8, 1)
                prev2 = _delay_rows(u, prev8, 2)
                carry_ref[:, cs] = u[tm - SUBLANES:tm]
            halves.append(fcb_ref[:, cs] + prev2 * fcw_ref[0:1, cs] + prev1 * fcw_ref[1:2, cs]
                          + u * fcw_ref[2:3, cs])
        act = (_silu_half(halves[0]) * halves[1]).astype(BF16)
        acc = acc + jnp.dot(act, wdn_ref[k * FFN_CHUNK:(k + 1) * FFN_CHUNK, :], preferred_element_type=F32)
    y_ref[...] = _rms(acc, fn_ref[...])
    if not decode:
        @pl.when(j == pl.num_programs(1) - 1)
        def _():
            ust_ref[0] = carry_ref[SUBLANES - (FFN_CONV_W - 1):SUBLANES, :]


def _outffn_prompt(x2d, ys, yp, wout, n2, wup, fcw, fcb, wdn, fn, nseq, seq, tm):
    nj = seq // tm
    row = lambda b, j: (b * nj + j, 0)
    return pl.pallas_call(
        functools.partial(_outffn_kernel, tm=tm, decode=False),
        grid=(nseq, nj),
        in_specs=[pl.BlockSpec((tm, D_MODEL), row),
                  pl.BlockSpec((tm, SSD_WIDTH), row),
                  pl.BlockSpec((tm, POOL_WIDTH), row),
                  _const_spec((D_MIX, D_MODEL)),
                  _const_spec((1, D_MODEL)),
                  _const_spec((D_MODEL, 2 * D_FF)),
                  _const_spec((FFN_CONV_W, 2 * D_FF)),
                  _const_spec((1, 2 * D_FF)),
                  _const_spec((D_FF, D_MODEL)),
                  _const_spec((1, D_MODEL))],
        out_specs=[pl.BlockSpec((tm, D_MODEL), row),
                   pl.BlockSpec((1, FFN_CONV_W - 1, 2 * D_FF), lambda b, j: (b, 0, 0))],
        out_shape=[jax.ShapeDtypeStruct((nseq * seq, D_MODEL), F32),
                   jax.ShapeDtypeStruct((nseq, FFN_CONV_W - 1, 2 * D_FF), F32)],
        scratch_shapes=[pltpu.VMEM((SUBLANES, 2 * D_FF), F32)],
        compiler_params=pltpu.CompilerParams(dimension_semantics=("arbitrary", "arbitrary"),
                                             vmem_limit_bytes=VMEM_LIMIT),
        name="outffn_prompt",
    )(x2d, ys, yp, wout, n2, wup, fcw, fcb, wdn, fn)


def _outffn_decode(x2d, ys, yp, ffn_state, wout, n2, wup, fcw, fcb, wdn, fn):
    n = x2d.shape[0]
    full = lambda shape: pl.BlockSpec(shape, lambda i: (0,) * len(shape))
    return pl.pallas_call(
        functools.partial(_outffn_kernel, tm=n, decode=True),
        grid=(1,),
        in_specs=[full((n, D_MODEL)), full((n, SSD_WIDTH)), full((n, POOL_WIDTH)), full(ffn_state.shape),
                  _const_spec((D_MIX, D_MODEL)),
                  _const_spec((1, D_MODEL)),
                  _const_spec((D_MODEL, 2 * D_FF)),
                  _const_spec((FFN_CONV_W, 2 * D_FF)),
                  _const_spec((1, 2 * D_FF)),
                  _const_spec((D_FF, D_MODEL)),
                  _const_spec((1, D_MODEL))],
        out_specs=[full((n, D_MODEL)), full(ffn_state.shape)],
        out_shape=[jax.ShapeDtypeStruct((n, D_MODEL), F32),
                   jax.ShapeDtypeStruct(ffn_state.shape, F32)],
        compiler_params=pltpu.CompilerParams(dimension_semantics=("arbitrary",),
                                             vmem_limit_bytes=VMEM_LIMIT),
        name="outffn_decode",
    )(x2d, ys, yp, ffn_state, wout, n2, wup, fcw, fcb, wdn, fn)


def kernel(x_prompt, x_sample, state_ssm, state_conv, state_pool, state_ffn_conv, norm1_w, w_in, conv_w, conv_b,
           dt_bias, a_log, d_skip, ssd_norm_w, pool_w, pool_scale, w_out, norm2_w, w_up, ffn_conv_w, ffn_conv_b,
           w_down, final_norm_w):
    assert w_in.shape[0] == 1, "single-layer model"
    nseq, seq, _ = x_prompt.shape
    nsmp, one, _ = x_sample.shape
    assert one == 1 and seq % CHUNK == 0 and nsmp == LANES

    wi = w_in[0]
    w_dt = jnp.pad(wi[:, SSD_WIDTH + CONV_DIM:SSD_WIDTH + CONV_DIM + N_HEADS], ((0, 0), (0, DT_PAD - N_HEADS)))
    w_cat = jnp.concatenate([wi[:, :SSD_WIDTH], wi[:, SSD_WIDTH:SSD_WIDTH + CONV_DIM],
                             wi[:, SSD_WIDTH + CONV_DIM + N_HEADS:], w_dt], axis=1).astype(BF16)
    n1 = norm1_w[0][None]
    cw, cb = conv_w[0], conv_b[0][None]
    dtb = jnp.pad(dt_bias[0], (0, DT_PAD - N_HEADS))[None]
    alog = jnp.pad(a_log[0], (0, DT_PAD - N_HEADS))[None]
    dskip = jnp.repeat(d_skip[0], HEAD_DIM)[None]
    nw = ssd_norm_w[0][None]
    pw = pool_w[0].astype(BF16)
    psc = pool_scale[0][None]
    wout = w_out[0].astype(BF16)
    n2 = norm2_w[0][None]
    wup = w_up[0].astype(BF16)
    gate_half = jnp.where(jnp.arange(2 * D_FF) < D_FF, 0.5, 1.0).astype(F32)
    fcw, fcb = ffn_conv_w[0] * gate_half, (ffn_conv_b[0] * gate_half)[None]
    wdn = w_down[0].astype(BF16)
    fn = final_norm_w[None]

    xp = x_prompt.reshape(nseq * seq, D_MODEL)
    xs_p, bc_p, bmt_p, acum_p, ast_p, dtt_p, gate_p, yp_p, ctail, vtail = _inproj_prompt(
        xp, n1, w_cat, 0.5 * cw, 0.5 * cb, dtb, alog, pw, psc, nseq, seq, tm=1024)
    per_seq = lambda a: a.reshape(nseq, -1, a.shape[-1])
    ys_p, hfin = _ssd_prompt(per_seq(xs_p), per_seq(bc_p), per_seq(bmt_p), per_seq(acum_p), per_seq(ast_p),
                             per_seq(dtt_p), per_seq(gate_p), dskip, nw, nseq, seq)
    ys_p = ys_p.reshape(nseq * seq, SSD_WIDTH)
    y_p, ffn_p = _outffn_prompt(xp, ys_p, yp_p, wout, n2, wup, fcw, fcb, wdn, fn, nseq, seq, tm=256)

    xs_in = x_sample.reshape(nsmp, D_MODEL)
    z_s, xbc_s, v_s, dt_s = _inproj(xs_in, n1, w_cat, tm=nsmp)
    h_t = jnp.transpose(state_ssm[0], (1, 2, 3, 0))
    cs_t = jnp.transpose(state_conv[0], (1, 0, 2))
    ps_t = jnp.transpose(state_pool[0], (1, 0, 2))
    hnew_t, ys_s, yp_s, cs_new, ps_new = _step_mix(xbc_s, z_s, dt_s, v_s, cs_t, ps_t, h_t, cw, cb, dtb, alog,
                                                   dskip, nw, pw, psc, PAST_LEN)
    y_s, ffn_new = _outffn_decode(xs_in, ys_s, yp_s, state_ffn_conv[0], wout, n2, wup, fcw, fcb, wdn, fn)

    return (y_p.reshape(nseq, seq, D_MODEL),
            y_s.reshape(nsmp, 1, D_MODEL),
            hfin.reshape(nseq, N_HEADS, HEAD_DIM, D_STATE)[None],
            jnp.transpose(hnew_t, (3, 0, 1, 2))[None],
            ctail[:, SUBLANES - (CONV_W - 1):][None],
            jnp.transpose(cs_new, (1, 0, 2))[None],
            vtail[:, 2 * SUBLANES - (POOL_MAX - 1):][None],
            jnp.transpose(ps_new, (1, 0, 2))[None],
            ffn_p[None],
            ffn_new[None])
```

```python
import functools

import jax
import jax.numpy as jnp
from jax import lax
from jax.experimental import pallas as pl
from jax.experimental.pallas import tpu as pltpu

F32 = jnp.float32
BF16 = jnp.bfloat16

D_MODEL = 1024
SSD_WIDTH = 1536
HEAD_DIM = 64
N_HEADS = SSD_WIDTH // HEAD_DIM
N_GROUPS = 4
HEADS_PER_GROUP = N_HEADS // N_GROUPS
D_STATE = 64
BC_WIDTH = N_GROUPS * D_STATE
CONV_DIM = SSD_WIDTH + 2 * BC_WIDTH
CONV_W = 4
CHUNK = 128
POOL_WIDTH = 512
POOL_WINDOWS = (2, 4, 8, 16)
POOL_GROUP_DIM = POOL_WIDTH // len(POOL_WINDOWS)
POOL_MAX = max(POOL_WINDOWS)
D_MIX = SSD_WIDTH + POOL_WIDTH
D_FF = 2816
FFN_CONV_W = 3
NORM_GROUP = SSD_WIDTH // N_GROUPS
EPS = 1e-6
LOG2_E = 1.4426950408889634
PAST_LEN = 16384

LANES = 128
SUBLANES = 8
DT_PAD = LANES
VMEM_LIMIT = 56 * 1024 * 1024
HEAD_PAIRS = N_HEADS // 2

_Z0, _Z1 = 0, SSD_WIDTH
_X0, _X1 = _Z1, _Z1 + CONV_DIM
_V0, _V1 = _X1, _X1 + POOL_WIDTH
_T0, _T1 = _V1, _V1 + DT_PAD


def _rms(x, w):
    ms = jnp.mean(x * x, axis=-1, keepdims=True)
    return x * lax.rsqrt(ms + EPS) * w


def _silu_half(h):
    return h + h * jnp.tanh(h)


def _silu(x):
    return _silu_half(0.5 * x)


def _softplus(x):
    return jnp.maximum(x, 0.0) + jnp.log1p(jnp.exp(-jnp.abs(x)))


def _const_spec(shape):
    nd = len(shape)
    return pl.BlockSpec(shape, lambda *_: (0,) * nd, pipeline_mode=pl.Buffered(1))


def _delay_rows(u, prev8, s):
    if s == SUBLANES:
        return jnp.concatenate([prev8, u[:u.shape[0] - SUBLANES]], axis=0)
    r = pltpu.roll(u, s, axis=0)
    c = pltpu.roll(prev8, s, axis=0)
    row = lax.broadcasted_iota(jnp.int32, prev8.shape, 0)
    top = jnp.where(row < s, c, r[0:SUBLANES])
    return jnp.concatenate([top, r[SUBLANES:]], axis=0)


def _gate_and_norm(y, xs, gate, dskip, nw):
    y = (y + dskip * xs) * gate
    outs = []
    for g in range(N_GROUPS):
        sl = slice(g * NORM_GROUP, (g + 1) * NORM_GROUP)
        outs.append(_rms(y[:, sl], nw[:, sl]))
    return jnp.concatenate(outs, axis=1)


def _pool_project(win_sums, v, cnts, pw_ref, pscale):
    outs = []
    for g in range(len(POOL_WINDOWS)):
        sl = slice(g * POOL_GROUP_DIM, (g + 1) * POOL_GROUP_DIM)
        m = win_sums[g] * (1.0 / cnts[g]) - v[:, sl]
        outs.append(jnp.dot(m.astype(BF16), pw_ref[g], preferred_element_type=F32))
    return jnp.concatenate(outs, axis=1) * pscale


def _cumsum_rows(x):
    n = x.shape[0]
    row = lax.broadcasted_iota(jnp.int32, x.shape, 0)
    k = 1
    while k < n:
        x = x + jnp.where(row >= k, pltpu.roll(x, k, axis=0), 0.0)
        k *= 2
    return x


def _inproj_kernel(*refs, tm, prompt, pos0):
    if prompt:
        (x_ref, nw_ref, w_ref, cw_ref, cb_ref, dtb_ref, alog_ref, pw_ref, psc_ref,
         xs_ref, bc_ref, bmt_ref, acum_ref, ast_ref, dtt_ref, gate_ref, yp_ref, ctail_ref, vtail_ref,
         cc_ref, vc_ref, s2c_ref, s4c_ref, s8c_ref, oc_ref, wt_ref) = refs
    else:
        x_ref, nw_ref, w_ref, z_ref, xbc_ref, v_ref, dt_ref = refs
    hn = _rms(x_ref[...], nw_ref[...]).astype(BF16)

    if not prompt:
        def mm(lo, hi):
            return lax.dot_general(hn, w_ref[lo:hi, :], (((1,), (1,)), ((), ())), preferred_element_type=F32)

        z_ref[...] = mm(_Z0, _Z1)
        xbc_ref[...] = mm(_X0, _X1)
        v_ref[...] = mm(_V0, _V1)
        dt_ref[...] = mm(_T0, _T1)
        return

    @pl.when((pl.program_id(0) == 0) & (pl.program_id(1) == 0))
    def _():
        for lo in range(0, _T1, 2 * LANES):
            hi = min(lo + 2 * LANES, _T1)
            wt_ref[:, lo:hi] = w_ref[lo:hi, :].T

    def mm(lo, hi):
        return jnp.dot(hn, wt_ref[:, lo:hi], preferred_element_type=F32)

    j = pl.program_id(1)
    carries = (cc_ref, vc_ref, s2c_ref, s4c_ref, s8c_ref)

    @pl.when(j == 0)
    def _():
        for r in carries + (oc_ref,):
            r[...] = jnp.zeros(r.shape, F32)

    last = j == pl.num_programs(1) - 1

    def conv_stage(lo, hi):
        def fn(xbc):
            assert CONV_W == 4
            w0, w1, w2, w3 = (cw_ref[k:k + 1, lo:hi] for k in range(CONV_W))
            x2 = _delay_rows(xbc, cc_ref[:, lo:hi], 2)
            odd = w2 * xbc + w0 * x2
            conv = cb_ref[:, lo:hi] + (w3 * xbc + w1 * x2) + _delay_rows(odd, oc_ref[:, lo:hi], 1)
            cc_ref[:, lo:hi] = xbc[tm - SUBLANES:tm]
            oc_ref[:, lo:hi] = odd[tm - SUBLANES:tm]
            act = _silu_half(conv)
            if hi <= SSD_WIDTH:
                xs_ref[:, lo:hi] = act.astype(BF16)
            else:
                bc_ref[...] = act[:, BC_WIDTH:]
                for c in range(tm // CHUNK):
                    for i in range(BC_WIDTH // LANES):
                        blk = act[c * CHUNK:(c + 1) * CHUNK, i * LANES:(i + 1) * LANES]
                        bmt_ref[c * BC_WIDTH + i * LANES:c * BC_WIDTH + (i + 1) * LANES, :] = blk.T
        return fn

    def pool_stage(v):
        s2 = v + _delay_rows(v, vc_ref[...], 1)
        s4 = s2 + _delay_rows(s2, s2c_ref[...], 2)
        s8 = s4 + _delay_rows(s4, s4c_ref[...], 4)
        s16 = s8 + _delay_rows(s8, s8c_ref[...], 8)
        for r, val in zip(carries[1:], (v, s2, s4, s8)):
            r[...] = val[tm - SUBLANES:tm]
        pos = pos0 + j * tm + lax.broadcasted_iota(jnp.int32, (tm, 1), 0)
        sums, cnts = [], []
        for g, (w, s) in enumerate(zip(POOL_WINDOWS, (s2, s4, s8, s16))):
            sums.append(s[:, g * POOL_GROUP_DIM:(g + 1) * POOL_GROUP_DIM])
            cnts.append(jnp.minimum(pos + 1, w).astype(F32))
        yp_ref[...] = _pool_project(sums, v, cnts, pw_ref, psc_ref[...]).astype(BF16)

        @pl.when(last)
        def _():
            vtail_ref[0] = v[tm - 2 * SUBLANES:tm]

    def gate_stage(lo, hi):
        def fn(z):
            gate_ref[:, lo:hi] = _silu(z)
        return fn

    def dt_stage(dt_raw):
        dt = _softplus(dt_raw + dtb_ref[...])
        neg_a = -jnp.exp(alog_ref[...])
        for c in range(tm // CHUNK):
            rows = slice(c * CHUNK, (c + 1) * CHUNK)
            acum = _cumsum_rows(dt[rows] * neg_a) * LOG2_E
            acum_ref[rows, :] = acum
            ast_ref[rows, :] = acum.T
            dtt_ref[rows, :] = dt[rows].T

    stages = [((_Z0, _Z1), gate_stage(0, SSD_WIDTH)),
              ((_T0, _T1), dt_stage),
              ((_X0, _X0 + SSD_WIDTH), conv_stage(0, SSD_WIDTH)),
              ((_X0 + SSD_WIDTH, _X1), conv_stage(SSD_WIDTH, CONV_DIM)),
              ((_V0, _V1), pool_stage)]
    for cols, fn in stages:
        fn(mm(*cols))

    @pl.when(last)
    def _():
        ctail_ref[0] = cc_ref[...]


def _inproj(x2d, norm_w, w_cat, tm):
    m = x2d.shape[0]
    row = lambda i: (i, 0)
    return pl.pallas_call(
        functools.partial(_inproj_kernel, tm=tm, prompt=False, pos0=0),
        grid=(m // tm,),
        in_specs=[pl.BlockSpec((tm, D_MODEL), row),
                  _const_spec((1, D_MODEL)),
                  _const_spec((_T1, D_MODEL))],
        out_specs=[pl.BlockSpec((tm, SSD_WIDTH), row),
                   pl.BlockSpec((tm, CONV_DIM), row),
                   pl.BlockSpec((tm, POOL_WIDTH), row),
                   pl.BlockSpec((tm, DT_PAD), row)],
        out_shape=[jax.ShapeDtypeStruct((m, SSD_WIDTH), F32),
                   jax.ShapeDtypeStruct((m, CONV_DIM), F32),
                   jax.ShapeDtypeStruct((m, POOL_WIDTH), F32),
                   jax.ShapeDtypeStruct((m, DT_PAD), F32)],
        compiler_params=pltpu.CompilerParams(dimension_semantics=("arbitrary",),
                                             vmem_limit_bytes=VMEM_LIMIT),
        name="inproj",
    )(x2d, norm_w, w_cat)


def _inproj_prompt(x2d, norm_w, w_cat, cw, cb, dtb, alog, pw, psc, nseq, seq, tm):
    nj = seq // tm
    m = nseq * seq
    row = lambda b, j: (b * nj + j, 0)
    per_seq = lambda b, j: (b, 0, 0)
    return pl.pallas_call(
        functools.partial(_inproj_kernel, tm=tm, prompt=True, pos0=0),
        grid=(nseq, nj),
        in_specs=[pl.BlockSpec((tm, D_MODEL), row),
                  _const_spec((1, D_MODEL)),
                  _const_spec((_T1, D_MODEL)),
                  _const_spec((CONV_W, CONV_DIM)),
                  _const_spec((1, CONV_DIM)),
                  _const_spec((1, DT_PAD)),
                  _const_spec((1, DT_PAD)),
                  _const_spec((len(POOL_WINDOWS), POOL_GROUP_DIM, POOL_GROUP_DIM)),
                  _const_spec((1, POOL_WIDTH))],
        out_specs=[pl.BlockSpec((tm, SSD_WIDTH), row),
                   pl.BlockSpec((tm, BC_WIDTH), row),
                   pl.BlockSpec((2 * tm, LANES), row),
                   pl.BlockSpec((tm, DT_PAD), row),
                   pl.BlockSpec((tm, DT_PAD), row),
                   pl.BlockSpec((tm, DT_PAD), row),
                   pl.BlockSpec((tm, SSD_WIDTH), row),
                   pl.BlockSpec((tm, POOL_WIDTH), row),
                   pl.BlockSpec((1, SUBLANES, CONV_DIM), per_seq),
                   pl.BlockSpec((1, 2 * SUBLANES, POOL_WIDTH), per_seq)],
        out_shape=[jax.ShapeDtypeStruct((m, SSD_WIDTH), BF16),
                   jax.ShapeDtypeStruct((m, BC_WIDTH), F32),
                   jax.ShapeDtypeStruct((2 * m, LANES), F32),
                   jax.ShapeDtypeStruct((m, DT_PAD), F32),
                   jax.ShapeDtypeStruct((m, DT_PAD), F32),
                   jax.ShapeDtypeStruct((m, DT_PAD), F32),
                   jax.ShapeDtypeStruct((m, SSD_WIDTH), F32),
                   jax.ShapeDtypeStruct((m, POOL_WIDTH), BF16),
                   jax.ShapeDtypeStruct((nseq, SUBLANES, CONV_DIM), F32),
                   jax.ShapeDtypeStruct((nseq, 2 * SUBLANES, POOL_WIDTH), F32)],
        scratch_shapes=[pltpu.VMEM((SUBLANES, CONV_DIM), F32),
                        pltpu.VMEM((SUBLANES, POOL_WIDTH), F32),
                        pltpu.VMEM((SUBLANES, POOL_WIDTH), F32),
                        pltpu.VMEM((SUBLANES, POOL_WIDTH), F32),
                        pltpu.VMEM((SUBLANES, POOL_WIDTH), F32),
                        pltpu.VMEM((SUBLANES, CONV_DIM), F32),
                        pltpu.VMEM((D_MODEL, _T1), BF16)],
        compiler_params=pltpu.CompilerParams(dimension_semantics=("arbitrary", "arbitrary"),
                                             vmem_limit_bytes=VMEM_LIMIT),
        name="inproj_prompt",
    )(x2d, norm_w, w_cat, cw, cb, dtb, alog, pw, psc)


SSD_SEQS = 8


def _ssd_chunk(q, xs_ref, bc_ref, bmt_ref, acum_ref, ast_ref, dtt_ref, gate_ref, dskip, nw, y_ref, state_ref):
    xs = xs_ref[q]
    cm = bc_ref[q]
    bm_t = bmt_ref[q]
    acum = acum_ref[q]
    as_t = ast_ref[q]
    dt_t = dtt_ref[q]
    li = lax.broadcasted_iota(jnp.int32, (CHUNK, CHUNK), 0)
    si = lax.broadcasted_iota(jnp.int32, (CHUNK, CHUNK), 1)
    causal = li >= si
    first_half = si < HEAD_DIM
    first_half_n = lax.broadcasted_iota(jnp.int32, (D_STATE, LANES), 1) < HEAD_DIM
    first_half_row = lax.broadcasted_iota(jnp.int32, (1, LANES), 1) < HEAD_DIM
    zeros_n = jnp.zeros((D_STATE, LANES), BF16)
    updates = []

    for g in range(N_GROUPS):
        gs = slice(g * D_STATE, (g + 1) * D_STATE)
        c_g = cm[:, gs]
        c2 = jnp.concatenate([c_g, c_g], axis=1)
        bt_g = bm_t[gs, :]
        cb = jnp.dot(c_g.astype(BF16), bt_g.astype(BF16), preferred_element_type=F32)
        y_pairs = []
        for k in range(HEADS_PER_GROUP // 2):
            pair = (g * HEADS_PER_GROUP) // 2 + k
            x_pair = xs[:, pair * LANES:(pair + 1) * LANES]
            s_pair = state_ref[q, pair]
            s_b = s_pair.astype(BF16)
            heads = (2 * pair, 2 * pair + 1)
            a_l = [jnp.broadcast_to(acum[:, h:h + 1], (CHUNK, CHUNK)) for h in heads]
            cdec = (c2 * jnp.exp2(jnp.where(first_half, a_l[0], a_l[1]))).astype(BF16)
            rhs = [jnp.concatenate([x_pair, s_b, zeros_n], axis=0),
                   jnp.concatenate([x_pair, zeros_n, s_b], axis=0)]
            res, lhs_s, cds = [], [], []
            for i, h in enumerate(heads):
                a_s, dt_s = as_t[h:h + 1, :], dt_t[h:h + 1, :]
                mix = cb * jnp.exp2(jnp.where(causal, a_l[i] - a_s, -jnp.inf)) * dt_s
                lhs = jnp.concatenate([mix.astype(BF16), cdec], axis=1)
                res.append(jnp.dot(lhs, rhs[i], preferred_element_type=F32))
                a_last = acum[CHUNK - 1:CHUNK, h:h + 1]
                lhs_s.append(bt_g * (jnp.exp2(a_last - a_s) * dt_s))
                cds.append(jnp.exp2(a_last))
            y_pairs.append(jnp.where(first_half, res[0], res[1]))
            updates.append((pair, jnp.concatenate(lhs_s, axis=0).astype(BF16), x_pair,
                            jnp.where(first_half_row, cds[0], cds[1]) * s_pair))

        ns = slice(g * NORM_GROUP, (g + 1) * NORM_GROUP)
        y_g = jnp.concatenate(y_pairs, axis=1)
        y_g = (y_g + dskip[:, ns] * xs[:, ns].astype(F32)) * gate_ref[q, :, ns]
        y_ref[q, :, ns] = _rms(y_g, nw[:, ns]).astype(BF16)

    for pair, lhs_s, x_pair, decayed in updates:
        upd = jnp.dot(lhs_s, x_pair, preferred_element_type=F32)
        state_ref[q, pair] = decayed + jnp.where(first_half_n, upd[:D_STATE], upd[D_STATE:])


def _ssd_prompt_kernel(xs_ref, bc_ref, bmt_ref, acum_ref, ast_ref, dtt_ref, gate_ref, dskip_ref, nw_ref,
                       y_ref, hfin_ref, state_ref):
    c = pl.program_id(1)

    @pl.when(c == 0)
    def _():
        state_ref[...] = jnp.zeros(state_ref.shape, F32)

    for q in range(SSD_SEQS):
        _ssd_chunk(q, xs_ref, bc_ref, bmt_ref, acum_ref, ast_ref, dtt_ref, gate_ref, dskip_ref[...], nw_ref[...],
                   y_ref, state_ref)

    @pl.when(c == pl.num_programs(1) - 1)
    def _():
        for q in range(SSD_SEQS):
            for pair in range(HEAD_PAIRS):
                t = jnp.concatenate([state_ref[q, pair], jnp.zeros((LANES - D_STATE, LANES), F32)], axis=0).T
                hfin_ref[q, pair * LANES:(pair + 1) * LANES, :] = t[:, :D_STATE]


def _ssd_prompt(xs, bc, bmt, acum, ast, dtt, gate, dskip, nw, nseq, seq):
    nc = seq // CHUNK
    blk = lambda width, rows=CHUNK: pl.BlockSpec((SSD_SEQS, rows, width), lambda b, c: (b, c, 0))
    return pl.pallas_call(
        _ssd_prompt_kernel,
        grid=(nseq // SSD_SEQS, nc),
        in_specs=[blk(SSD_WIDTH), blk(BC_WIDTH), blk(LANES, BC_WIDTH), blk(DT_PAD), blk(DT_PAD), blk(DT_PAD),
                  blk(SSD_WIDTH),
                  _const_spec((1, SSD_WIDTH)),
                  _const_spec((1, SSD_WIDTH))],
        out_specs=[blk(SSD_WIDTH),
                   pl.BlockSpec((SSD_SEQS, SSD_WIDTH, D_STATE), lambda b, c: (b, 0, 0))],
        out_shape=[jax.ShapeDtypeStruct((nseq, seq, SSD_WIDTH), BF16),
                   jax.ShapeDtypeStruct((nseq, SSD_WIDTH, D_STATE), F32)],
        scratch_shapes=[pltpu.VMEM((SSD_SEQS, HEAD_PAIRS, D_STATE, LANES), F32)],
        compiler_params=pltpu.CompilerParams(dimension_semantics=("arbitrary", "arbitrary"),
                                             vmem_limit_bytes=VMEM_LIMIT),
        name="ssd_prompt",
    )(xs, bc, bmt, acum, ast, dtt, gate, dskip, nw)


STEP_HEADS = 4


def _step_mix_kernel(xbc_ref, z_ref, dt_ref, v_ref, cs_ref, ps_ref, h_ref,
                     cw_ref, cb_ref, dtb_ref, alog_ref, dskip_ref, nw_ref, pw_ref, psc_ref,
                     hout_ref, y_ref, yp_ref, cso_ref, pso_ref,
                     xs_ref, xdt_ref, bt_ref, ct_ref, dect_ref, yt_ref, *, pos0):
    h = pl.program_id(0)
    nprev = POOL_MAX - 1

    @pl.when(h == 0)
    def _prepare():
        xnew = xbc_ref[...]
        conv = cb_ref[...] + xnew * cw_ref[CONV_W - 1:CONV_W, :]
        for k in range(CONV_W - 1):
            conv = conv + cs_ref[k] * cw_ref[k:k + 1, :]
        xbc = _silu(conv)
        xs = xbc[:, :SSD_WIDTH]
        xs_ref[...] = xs
        dt = _softplus(dt_ref[...] + dtb_ref[...])
        dect_ref[...] = jnp.exp(dt * (-jnp.exp(alog_ref[...]))).T
        dt_t = dt.T
        for hh in range(N_HEADS):
            blk = xs[:, (hh // 2) * LANES:(hh // 2 + 1) * LANES].T
            half = blk[(hh % 2) * HEAD_DIM:(hh % 2 + 1) * HEAD_DIM, :]
            xdt_ref[hh * HEAD_DIM:(hh + 1) * HEAD_DIM, :] = half * dt_t[hh:hh + 1, :]
        for i in range(BC_WIDTH // LANES):
            sl = slice(i * LANES, (i + 1) * LANES)
            bt_ref[sl, :] = xbc[:, SSD_WIDTH + i * LANES:SSD_WIDTH + (i + 1) * LANES].T
            ct_ref[sl, :] = xbc[:, SSD_WIDTH + BC_WIDTH + i * LANES:SSD_WIDTH + BC_WIDTH + (i + 1) * LANES].T
        for k in range(CONV_W - 2):
            cso_ref[k] = cs_ref[k + 1]
        cso_ref[CONV_W - 2] = xnew
        for k in range(nprev - 1):
            pso_ref[k] = ps_ref[k + 1]
        pso_ref[nprev - 1] = v_ref[...]

    for i in range(STEP_HEADS):
        head = h * STEP_HEADS + i
        g = lax.div(head, HEADS_PER_GROUP)
        hrow = pl.multiple_of(head * HEAD_DIM, HEAD_DIM)
        grow = pl.multiple_of(g * D_STATE, D_STATE)
        b_t = bt_ref[pl.ds(grow, D_STATE), :]
        c_t = ct_ref[pl.ds(grow, D_STATE), :]
        dec = dect_ref[pl.ds(head, 1), :]

        def per_p(p, carry, i=i, hrow=hrow, b_t=b_t, c_t=c_t, dec=dec):
            xrow = xdt_ref[pl.ds(hrow + p, 1), :]
            hn = h_ref[i, p] * dec + xrow * b_t
            hout_ref[i, p] = hn
            yt_ref[pl.ds(hrow + p, 1), :] = jnp.sum(hn * c_t, axis=0, keepdims=True)
            return carry

        lax.fori_loop(0, HEAD_DIM, per_p, 0, unroll=4)

    @pl.when(h == pl.num_programs(0) - 1)
    def _finish():
        y0 = jnp.concatenate([yt_ref[i * LANES:(i + 1) * LANES, :].T for i in range(SSD_WIDTH // LANES)], axis=1)
        y = _gate_and_norm(y0, xs_ref[...], _silu(z_ref[...]), dskip_ref[...], nw_ref[...])
        y_ref[...] = y.astype(BF16)
        v = v_ref[...]
        sums, cnts = [], []
        for gi, w in enumerate(POOL_WINDOWS):
            sl = slice(gi * POOL_GROUP_DIM, (gi + 1) * POOL_GROUP_DIM)
            acc = v[:, sl]
            for j in range(1, w):
                acc = acc + ps_ref[nprev - j][:, sl]
            sums.append(acc)
            cnts.append(float(min(pos0 + 1, w)))
        yp_ref[...] = _pool_project(sums, v, cnts, pw_ref, psc_ref[...]).astype(BF16)


def _step_mix(xbc, z, dt, v, cs_t, ps_t, h_t, cw, cb, dtb, alog, dskip, nw, pw, psc, pos0):
    n = xbc.shape[0]
    nprev = POOL_MAX - 1
    full = lambda shape: pl.BlockSpec(shape, lambda i: (0,) * len(shape))
    hblk = pl.BlockSpec((STEP_HEADS, HEAD_DIM, D_STATE, n), lambda i: (i, 0, 0, 0))
    return pl.pallas_call(
        functools.partial(_step_mix_kernel, pos0=pos0),
        grid=(N_HEADS // STEP_HEADS,),
        in_specs=[full((n, CONV_DIM)), full((n, SSD_WIDTH)), full((n, DT_PAD)), full((n, POOL_WIDTH)),
                  _const_spec((CONV_W - 1, n, CONV_DIM)), _const_spec((nprev, n, POOL_WIDTH)), hblk,
                  full((CONV_W, CONV_DIM)), full((1, CONV_DIM)), full((1, DT_PAD)), full((1, DT_PAD)),
                  full((1, SSD_WIDTH)), full((1, SSD_WIDTH)),
                  full((len(POOL_WINDOWS), POOL_GROUP_DIM, POOL_GROUP_DIM)), full((1, POOL_WIDTH))],
        out_specs=[hblk, full((n, SSD_WIDTH)), full((n, POOL_WIDTH)),
                   full((CONV_W - 1, n, CONV_DIM)), full((nprev, n, POOL_WIDTH))],
        out_shape=[jax.ShapeDtypeStruct(h_t.shape, F32),
                   jax.ShapeDtypeStruct((n, SSD_WIDTH), BF16),
                   jax.ShapeDtypeStruct((n, POOL_WIDTH), BF16),
                   jax.ShapeDtypeStruct((CONV_W - 1, n, CONV_DIM), F32),
                   jax.ShapeDtypeStruct((nprev, n, POOL_WIDTH), F32)],
        scratch_shapes=[pltpu.VMEM((n, SSD_WIDTH), F32),
                        pltpu.VMEM((SSD_WIDTH, n), F32),
                        pltpu.VMEM((BC_WIDTH, n), F32),
                        pltpu.VMEM((BC_WIDTH, n), F32),
                        pltpu.VMEM((DT_PAD, n), F32),
                        pltpu.VMEM((SSD_WIDTH, n), F32)],
        compiler_params=pltpu.CompilerParams(dimension_semantics=("arbitrary",),
                                             vmem_limit_bytes=VMEM_LIMIT),
        name="step_mix",
    )(xbc, z, dt, v, cs_t, ps_t, h_t, cw, cb, dtb, alog, dskip, nw, pw, psc)


FFN_CHUNK = 256


def _outffn_kernel(*refs, tm, decode):
    if decode:
        (x_ref, ys_ref, yp_ref, st_ref, wout_ref, n2_ref, wup_ref, fcw_ref, fcb_ref, wdn_ref, fn_ref,
         y_ref, sto_ref) = refs
    else:
        (x_ref, ys_ref, yp_ref, wout_ref, n2_ref, wup_ref, fcw_ref, fcb_ref, wdn_ref, fn_ref,
         y_ref, ust_ref, carry_ref) = refs
        j = pl.program_id(1)

        @pl.when(j == 0)
        def _():
            carry_ref[...] = jnp.zeros(carry_ref.shape, F32)

    ymix = jnp.concatenate([ys_ref[...], yp_ref[...]], axis=1)
    x1 = x_ref[...] + jnp.dot(ymix, wout_ref[...], preferred_element_type=F32)
    hn = _rms(x1, n2_ref[...]).astype(BF16)
    acc = x1
    nchunk = D_FF // FFN_CHUNK

    def col_slices(k):
        return [slice(base + k * FFN_CHUNK, base + (k + 1) * FFN_CHUNK) for base in (0, D_FF)]

    def up_proj(k):
        return [jnp.dot(hn, wup_ref[:, cs], preferred_element_type=F32) for cs in col_slices(k)]

    u_next = up_proj(0)
    for k in range(nchunk):
        u_cur = u_next
        if k + 1 < nchunk:
            u_next = up_proj(k + 1)
        halves = []
        for u, cs in zip(u_cur, col_slices(k)):
            if decode:
                prev2, prev1 = st_ref[:, 0, cs], st_ref[:, 1, cs]
                sto_ref[:, 0, cs] = prev1
                sto_ref[:, 1, cs] = u
            else:
                prev8 = carry_ref[:, cs]
                prev1 = _delay_rows(u, prev8, 1)
                prev2 = _delay_rows(u, prev8, 2)
                carry_ref[:, cs] = u[tm - SUBLANES:tm]
            halves.append(fcb_ref[:, cs] + prev2 * fcw_ref[0:1, cs] + prev1 * fcw_ref[1:2, cs]
                          + u * fcw_ref[2:3, cs])
        act = (_silu_half(halves[0]) * halves[1]).astype(BF16)
        acc = acc + jnp.dot(act, wdn_ref[k * FFN_CHUNK:(k + 1) * FFN_CHUNK, :], preferred_element_type=F32)
    y_ref[...] = _rms(acc, fn_ref[...])
    if not decode:
        @pl.when(j == pl.num_programs(1) - 1)
        def _():
            ust_ref[0] = carry_ref[SUBLANES - (FFN_CONV_W - 1):SUBLANES, :]


def _outffn_prompt(x2d, ys, yp, wout, n2, wup, fcw, fcb, wdn, fn, nseq, seq, tm):
    nj = seq // tm
    row = lambda b, j: (b * nj + j, 0)
    return pl.pallas_call(
        functools.partial(_outffn_kernel, tm=tm, decode=False),
        grid=(nseq, nj),
        in_specs=[pl.BlockSpec((tm, D_MODEL), row),
                  pl.BlockSpec((tm, SSD_WIDTH), row),
                  pl.BlockSpec((tm, POOL_WIDTH), row),
                  _const_spec((D_MIX, D_MODEL)),
                  _const_spec((1, D_MODEL)),
                  _const_spec((D_MODEL, 2 * D_FF)),
                  _const_spec((FFN_CONV_W, 2 * D_FF)),
                  _const_spec((1, 2 * D_FF)),
                  _const_spec((D_FF, D_MODEL)),
                  _const_spec((1, D_MODEL))],
        out_specs=[pl.BlockSpec((tm, D_MODEL), row),
                   pl.BlockSpec((1, FFN_CONV_W - 1, 2 * D_FF), lambda b, j: (b, 0, 0))],
        out_shape=[jax.ShapeDtypeStruct((nseq * seq, D_MODEL), F32),
                   jax.ShapeDtypeStruct((nseq, FFN_CONV_W - 1, 2 * D_FF), F32)],
        scratch_shapes=[pltpu.VMEM((SUBLANES, 2 * D_FF), F32)],
        compiler_params=pltpu.CompilerParams(dimension_semantics=("arbitrary", "arbitrary"),
                                             vmem_limit_bytes=VMEM_LIMIT),
        name="outffn_prompt",
    )(x2d, ys, yp, wout, n2, wup, fcw, fcb, wdn, fn)


def _outffn_decode(x2d, ys, yp, ffn_state, wout, n2, wup, fcw, fcb, wdn, fn):
    n = x2d.shape[0]
    full = lambda shape: pl.BlockSpec(shape, lambda i: (0,) * len(shape))
    return pl.pallas_call(
        functools.partial(_outffn_kernel, tm=n, decode=True),
        grid=(1,),
        in_specs=[full((n, D_MODEL)), full((n, SSD_WIDTH)), full((n, POOL_WIDTH)), full(ffn_state.shape),
                  _const_spec((D_MIX, D_MODEL)),
                  _const_spec((1, D_MODEL)),
                  _const_spec((D_MODEL, 2 * D_FF)),
                  _const_spec((FFN_CONV_W, 2 * D_FF)),
                  _const_spec((1, 2 * D_FF)),
                  _const_spec((D_FF, D_MODEL)),
                  _const_spec((1, D_MODEL))],
        out_specs=[full((n, D_MODEL)), full(ffn_state.shape)],
        out_shape=[jax.ShapeDtypeStruct((n, D_MODEL), F32),
                   jax.ShapeDtypeStruct(ffn_state.shape, F32)],
        compiler_params=pltpu.CompilerParams(dimension_semantics=("arbitrary",),
                                             vmem_limit_bytes=VMEM_LIMIT),
        name="outffn_decode",
    )(x2d, ys, yp, ffn_state, wout, n2, wup, fcw, fcb, wdn, fn)


def kernel(x_prompt, x_sample, state_ssm, state_conv, state_pool, state_ffn_conv, norm1_w, w_in, conv_w, conv_b,
           dt_bias, a_log, d_skip, ssd_norm_w, pool_w, pool_scale, w_out, norm2_w, w_up, ffn_conv_w, ffn_conv_b,
           w_down, final_norm_w):
    assert w_in.shape[0] == 1, "single-layer model"
    nseq, seq, _ = x_prompt.shape
    nsmp, one, _ = x_sample.shape
    assert one == 1 and seq % CHUNK == 0 and nsmp == LANES

    wi = jnp.swapaxes(w_in[0], 0, 1)
    dt0 = SSD_WIDTH + CONV_DIM
    w_dt = jnp.pad(wi[dt0:dt0 + N_HEADS], ((0, DT_PAD - N_HEADS), (0, 0)))
    w_cat = jnp.concatenate([wi[:dt0], wi[dt0 + N_HEADS:], w_dt], axis=0).astype(BF16)
    n1 = norm1_w[0][None]
    cw, cb = conv_w[0], conv_b[0][None]
    dtb = jnp.pad(dt_bias[0], (0, DT_PAD - N_HEADS))[None]
    alog = jnp.pad(a_log[0], (0, DT_PAD - N_HEADS))[None]
    dskip = jnp.repeat(d_skip[0], HEAD_DIM)[None]
    nw = ssd_norm_w[0][None]
    pw = pool_w[0].astype(BF16)
    psc = pool_scale[0][None]
    wout = w_out[0].astype(BF16)
    n2 = norm2_w[0][None]
    wup = w_up[0].astype(BF16)
    gate_half = jnp.where(jnp.arange(2 * D_FF) < D_FF, 0.5, 1.0).astype(F32)
    fcw, fcb = ffn_conv_w[0] * gate_half, (ffn_conv_b[0] * gate_half)[None]
    wdn = w_down[0].astype(BF16)
    fn = final_norm_w[None]

    xp = x_prompt.reshape(nseq * seq, D_MODEL)
    xs_p, bc_p, bmt_p, acum_p, ast_p, dtt_p, gate_p, yp_p, ctail, vtail = _inproj_prompt(
        xp, n1, w_cat, 0.5 * cw, 0.5 * cb, dtb, alog, pw, psc, nseq, seq, tm=1024)
    per_seq = lambda a: a.reshape(nseq, -1, a.shape[-1])
    ys_p, hfin = _ssd_prompt(per_seq(xs_p), per_seq(bc_p), per_seq(bmt_p), per_seq(acum_p), per_seq(ast_p),
                             per_seq(dtt_p), per_seq(gate_p), dskip, nw, nseq, seq)
    ys_p = ys_p.reshape(nseq * seq, SSD_WIDTH)
    y_p, ffn_p = _outffn_prompt(xp, ys_p, yp_p, wout, n2, wup, fcw, fcb, wdn, fn, nseq, seq, tm=256)

    xs_in = x_sample.reshape(nsmp, D_MODEL)
    z_s, xbc_s, v_s, dt_s = _inproj(xs_in, n1, w_cat, tm=nsmp)
    h_t = jnp.transpose(state_ssm[0], (1, 2, 3, 0))
    cs_t = jnp.transpose(state_conv[0], (1, 0, 2))
    ps_t = jnp.transpose(state_pool[0], (1, 0, 2))
    hnew_t, ys_s, yp_s, cs_new, ps_new = _step_mix(xbc_s, z_s, dt_s, v_s, cs_t, ps_t, h_t, cw, cb, dtb, alog,
                                                   dskip, nw, pw, psc, PAST_LEN)
    y_s, ffn_new = _outffn_decode(xs_in, ys_s, yp_s, state_ffn_conv[0], wout, n2, wup, fcw, fcb, wdn, fn)

    return (y_p.reshape(nseq, seq, D_MODEL),
            y_s.reshape(nsmp, 1, D_MODEL),
            hfin.reshape(nseq, N_HEADS, HEAD_DIM, D_STATE)[None],
            jnp.transpose(hnew_t, (3, 0, 1, 2))[None],
            ctail[:, SUBLANES - (CONV_W - 1):][None],
            jnp.transpose(cs_new, (1, 0, 2))[None],
            vtail[:, 2 * SUBLANES - (POOL_MAX - 1):][None],
            jnp.transpose(ps_new, (1, 0, 2))[None],
            ffn_p[None],
            ffn_new[None])
```

```python
import functools

import jax
import jax.numpy as jnp
from jax import lax
from jax.experimental import pallas as pl
from jax.experimental.pallas import tpu as pltpu

F32 = jnp.float32
BF16 = jnp.bfloat16

D_MODEL = 1024
SSD_WIDTH = 1536
HEAD_DIM = 64
N_HEADS = SSD_WIDTH // HEAD_DIM
N_GROUPS = 4
HEADS_PER_GROUP = N_HEADS // N_GROUPS
D_STATE = 64
BC_WIDTH = N_GROUPS * D_STATE
CONV_DIM = SSD_WIDTH + 2 * BC_WIDTH
CONV_W = 4
CHUNK = 128
POOL_WIDTH = 512
POOL_WINDOWS = (2, 4, 8, 16)
POOL_GROUP_DIM = POOL_WIDTH // len(POOL_WINDOWS)
POOL_MAX = max(POOL_WINDOWS)
D_MIX = SSD_WIDTH + POOL_WIDTH
D_FF = 2816
FFN_CONV_W = 3
NORM_GROUP = SSD_WIDTH // N_GROUPS
EPS = 1e-6
LOG2_E = 1.4426950408889634
PAST_LEN = 16384

LANES = 128
SUBLANES = 8
DT_PAD = LANES
VMEM_LIMIT = 56 * 1024 * 1024
HEAD_PAIRS = N_HEADS // 2

_Z0, _Z1 = 0, SSD_WIDTH
_X0, _X1 = _Z1, _Z1 + CONV_DIM
_V0, _V1 = _X1, _X1 + POOL_WIDTH
_T0, _T1 = _V1, _V1 + DT_PAD


def _rms(x, w):
    ms = jnp.mean(x * x, axis=-1, keepdims=True)
    return x * lax.rsqrt(ms + EPS) * w


def _silu_half(h):
    return h + h * jnp.tanh(h)


def _silu(x):
    return _silu_half(0.5 * x)


def _softplus(x):
    return jnp.maximum(x, 0.0) + jnp.log1p(jnp.exp(-jnp.abs(x)))


def _const_spec(shape):
    nd = len(shape)
    return pl.BlockSpec(shape, lambda *_: (0,) * nd, pipeline_mode=pl.Buffered(1))


def _delay_rows(u, prev8, s):
    if s == SUBLANES:
        return jnp.concatenate([prev8, u[:u.shape[0] - SUBLANES]], axis=0)
    r = pltpu.roll(u, s, axis=0)
    c = pltpu.roll(prev8, s, axis=0)
    row = lax.broadcasted_iota(jnp.int32, prev8.shape, 0)
    top = jnp.where(row < s, c, r[0:SUBLANES])
    return jnp.concatenate([top, r[SUBLANES:]], axis=0)


def _gate_and_norm(y, xs, gate, dskip, nw):
    y = (y + dskip * xs) * gate
    outs = []
    for g in range(N_GROUPS):
        sl = slice(g * NORM_GROUP, (g + 1) * NORM_GROUP)
        outs.append(_rms(y[:, sl], nw[:, sl]))
    return jnp.concatenate(outs, axis=1)


def _pool_project(win_sums, v, cnts, pw_ref, pscale):
    outs = []
    for g in range(len(POOL_WINDOWS)):
        sl = slice(g * POOL_GROUP_DIM, (g + 1) * POOL_GROUP_DIM)
        m = win_sums[g] * (1.0 / cnts[g]) - v[:, sl]
        outs.append(jnp.dot(m.astype(BF16), pw_ref[g], preferred_element_type=F32))
    return jnp.concatenate(outs, axis=1) * pscale


def _cumsum_rows(x):
    n = x.shape[0]
    row = lax.broadcasted_iota(jnp.int32, x.shape, 0)
    k = 1
    while k < n:
        x = x + jnp.where(row >= k, pltpu.roll(x, k, axis=0), 0.0)
        k *= 2
    return x


def _inproj_kernel(*refs, tm, prompt, pos0):
    if prompt:
        (x_ref, nw_ref, w_ref, wtail_ref, cw_ref, cb_ref, dtb_ref, alog_ref, pw_ref, psc_ref,
         xs_ref, bc_ref, bmt_ref, acum_ref, ast_ref, dtt_ref, gate_ref, yp_ref, ctail_ref, vtail_ref,
         cc_ref, vc_ref, s2c_ref, s4c_ref, s8c_ref, oc_ref, wt_ref) = refs
    else:
        x_ref, nw_ref, w_ref, wtail_ref, z_ref, xbc_ref, v_ref, dt_ref = refs
    hn = _rms(x_ref[...], nw_ref[...]).astype(BF16)

    def tail_rows():
        w_v = wtail_ref[N_HEADS:, :].astype(BF16)
        dt_pad = jnp.zeros((DT_PAD - N_HEADS, D_MODEL), F32)
        return w_v, jnp.concatenate([wtail_ref[:N_HEADS, :], dt_pad], axis=0).astype(BF16)

    if not prompt:
        def nt(w):
            return lax.dot_general(hn, w, (((1,), (1,)), ((), ())), preferred_element_type=F32)

        w_v, w_dt = tail_rows()
        z_ref[...] = nt(w_ref[_Z0:_Z1, :])
        xbc_ref[...] = nt(w_ref[_X0:_X1, :])
        v_ref[...] = nt(w_v)
        dt_ref[...] = nt(w_dt)
        return

    @pl.when((pl.program_id(0) == 0) & (pl.program_id(1) == 0))
    def _():
        step = 2 * LANES
        for lo in range(0, _X1, step):
            wt_ref[:, lo:lo + step] = w_ref[lo:lo + step, :].T
        w_v, w_dt = tail_rows()
        for lo in range(0, POOL_WIDTH, step):
            wt_ref[:, _V0 + lo:_V0 + lo + step] = w_v[lo:lo + step, :].T
        wt_ref[:, _T0:_T1] = w_dt.T

    def mm(lo, hi):
        return jnp.dot(hn, wt_ref[:, lo:hi], preferred_element_type=F32)

    j = pl.program_id(1)
    carries = (cc_ref, vc_ref, s2c_ref, s4c_ref, s8c_ref)

    @pl.when(j == 0)
    def _():
        for r in carries + (oc_ref,):
            r[...] = jnp.zeros(r.shape, F32)

    last = j == pl.num_programs(1) - 1

    def conv_stage(lo, hi):
        def fn(xbc):
            assert CONV_W == 4
            w0, w1, w2, w3 = (cw_ref[k:k + 1, lo:hi] for k in range(CONV_W))
            x2 = _delay_rows(xbc, cc_ref[:, lo:hi], 2)
            odd = w2 * xbc + w0 * x2
            conv = cb_ref[:, lo:hi] + (w3 * xbc + w1 * x2) + _delay_rows(odd, oc_ref[:, lo:hi], 1)
            cc_ref[:, lo:hi] = xbc[tm - SUBLANES:tm]
            oc_ref[:, lo:hi] = odd[tm - SUBLANES:tm]
            act = _silu_half(conv)
            if hi <= SSD_WIDTH:
                xs_ref[:, lo:hi] = act.astype(BF16)
            else:
                bc_ref[...] = act[:, BC_WIDTH:]
                for c in range(tm // CHUNK):
                    for i in range(BC_WIDTH // LANES):
                        blk = act[c * CHUNK:(c + 1) * CHUNK, i * LANES:(i + 1) * LANES]
                        bmt_ref[c * BC_WIDTH + i * LANES:c * BC_WIDTH + (i + 1) * LANES, :] = blk.T
        return fn

    def pool_stage(v):
        s2 = v + _delay_rows(v, vc_ref[...], 1)
        s4 = s2 + _delay_rows(s2, s2c_ref[...], 2)
        s8 = s4 + _delay_rows(s4, s4c_ref[...], 4)
        s16 = s8 + _delay_rows(s8, s8c_ref[...], 8)
        for r, val in zip(carries[1:], (v, s2, s4, s8)):
            r[...] = val[tm - SUBLANES:tm]
        pos = pos0 + j * tm + lax.broadcasted_iota(jnp.int32, (tm, 1), 0)
        sums, cnts = [], []
        for g, (w, s) in enumerate(zip(POOL_WINDOWS, (s2, s4, s8, s16))):
            sums.append(s[:, g * POOL_GROUP_DIM:(g + 1) * POOL_GROUP_DIM])
            cnts.append(jnp.minimum(pos + 1, w).astype(F32))
        yp_ref[...] = _pool_project(sums, v, cnts, pw_ref, psc_ref[...]).astype(BF16)

        @pl.when(last)
        def _():
            vtail_ref[0] = v[tm - 2 * SUBLANES:tm]

    def gate_stage(lo, hi):
        def fn(z):
            gate_ref[:, lo:hi] = _silu(z)
        return fn

    def dt_stage(dt_raw):
        dt = _softplus(dt_raw + dtb_ref[...])
        neg_a = -jnp.exp(alog_ref[...])
        for c in range(tm // CHUNK):
            rows = slice(c * CHUNK, (c + 1) * CHUNK)
            acum = _cumsum_rows(dt[rows] * neg_a) * LOG2_E
            acum_ref[rows, :] = acum
            ast_ref[rows, :] = acum.T
            dtt_ref[rows, :] = dt[rows].T

    stages = [((_Z0, _Z1), gate_stage(0, SSD_WIDTH)),
              ((_T0, _T1), dt_stage),
              ((_X0, _X0 + SSD_WIDTH), conv_stage(0, SSD_WIDTH)),
              ((_X0 + SSD_WIDTH, _X1), conv_stage(SSD_WIDTH, CONV_DIM)),
              ((_V0, _V1), pool_stage)]
    for cols, fn in stages:
        fn(mm(*cols))

    @pl.when(last)
    def _():
        ctail_ref[0] = cc_ref[...]


def _inproj(x2d, norm_w, w_main, w_tail, tm):
    m = x2d.shape[0]
    row = lambda i: (i, 0)
    return pl.pallas_call(
        functools.partial(_inproj_kernel, tm=tm, prompt=False, pos0=0),
        grid=(m // tm,),
        in_specs=[pl.BlockSpec((tm, D_MODEL), row),
                  _const_spec((1, D_MODEL)),
                  _const_spec((_X1, D_MODEL)),
                  _const_spec((N_HEADS + POOL_WIDTH, D_MODEL))],
        out_specs=[pl.BlockSpec((tm, SSD_WIDTH), row),
                   pl.BlockSpec((tm, CONV_DIM), row),
                   pl.BlockSpec((tm, POOL_WIDTH), row),
                   pl.BlockSpec((tm, DT_PAD), row)],
        out_shape=[jax.ShapeDtypeStruct((m, SSD_WIDTH), F32),
                   jax.ShapeDtypeStruct((m, CONV_DIM), F32),
                   jax.ShapeDtypeStruct((m, POOL_WIDTH), F32),
                   jax.ShapeDtypeStruct((m, DT_PAD), F32)],
        compiler_params=pltpu.CompilerParams(dimension_semantics=("arbitrary",),
                                             vmem_limit_bytes=VMEM_LIMIT),
        name="inproj",
    )(x2d, norm_w, w_main, w_tail)


def _inproj_prompt(x2d, norm_w, w_main, w_tail, cw, cb, dtb, alog, pw, psc, nseq, seq, tm):
    nj = seq // tm
    m = nseq * seq
    row = lambda b, j: (b * nj + j, 0)
    per_seq = lambda b, j: (b, 0, 0)
    return pl.pallas_call(
        functools.partial(_inproj_kernel, tm=tm, prompt=True, pos0=0),
        grid=(nseq, nj),
        in_specs=[pl.BlockSpec((tm, D_MODEL), row),
                  _const_spec((1, D_MODEL)),
                  _const_spec((_X1, D_MODEL)),
                  _const_spec((N_HEADS + POOL_WIDTH, D_MODEL)),
                  _const_spec((CONV_W, CONV_DIM)),
                  _const_spec((1, CONV_DIM)),
                  _const_spec((1, DT_PAD)),
                  _const_spec((1, DT_PAD)),
                  _const_spec((len(POOL_WINDOWS), POOL_GROUP_DIM, POOL_GROUP_DIM)),
                  _const_spec((1, POOL_WIDTH))],
        out_specs=[pl.BlockSpec((tm, SSD_WIDTH), row),
                   pl.BlockSpec((tm, BC_WIDTH), row),
                   pl.BlockSpec((2 * tm, LANES), row),
                   pl.BlockSpec((tm, DT_PAD), row),
                   pl.BlockSpec((tm, DT_PAD), row),
                   pl.BlockSpec((tm, DT_PAD), row),
                   pl.BlockSpec((tm, SSD_WIDTH), row),
                   pl.BlockSpec((tm, POOL_WIDTH), row),
                   pl.BlockSpec((1, SUBLANES, CONV_DIM), per_seq),
                   pl.BlockSpec((1, 2 * SUBLANES, POOL_WIDTH), per_seq)],
        out_shape=[jax.ShapeDtypeStruct((m, SSD_WIDTH), BF16),
                   jax.ShapeDtypeStruct((m, BC_WIDTH), F32),
                   jax.ShapeDtypeStruct((2 * m, LANES), F32),
                   jax.ShapeDtypeStruct((m, DT_PAD), F32),
                   jax.ShapeDtypeStruct((m, DT_PAD), F32),
                   jax.ShapeDtypeStruct((m, DT_PAD), F32),
                   jax.ShapeDtypeStruct((m, SSD_WIDTH), F32),
                   jax.ShapeDtypeStruct((m, POOL_WIDTH), BF16),
                   jax.ShapeDtypeStruct((nseq, SUBLANES, CONV_DIM), F32),
                   jax.ShapeDtypeStruct((nseq, 2 * SUBLANES, POOL_WIDTH), F32)],
        scratch_shapes=[pltpu.VMEM((SUBLANES, CONV_DIM), F32),
                        pltpu.VMEM((SUBLANES, POOL_WIDTH), F32),
                        pltpu.VMEM((SUBLANES, POOL_WIDTH), F32),
                        pltpu.VMEM((SUBLANES, POOL_WIDTH), F32),
                        pltpu.VMEM((SUBLANES, POOL_WIDTH), F32),
                        pltpu.VMEM((SUBLANES, CONV_DIM), F32),
                        pltpu.VMEM((D_MODEL, _T1), BF16)],
        compiler_params=pltpu.CompilerParams(dimension_semantics=("arbitrary", "arbitrary"),
                                             vmem_limit_bytes=VMEM_LIMIT),
        name="inproj_prompt",
    )(x2d, norm_w, w_main, w_tail, cw, cb, dtb, alog, pw, psc)


SSD_SEQS = 8


def _ssd_chunk(q, xs_ref, bc_ref, bmt_ref, acum_ref, ast_ref, dtt_ref, gate_ref, dskip, nw, y_ref, state_ref):
    xs = xs_ref[q]
    cm = bc_ref[q]
    bm_t = bmt_ref[q]
    acum = acum_ref[q]
    as_t = ast_ref[q]
    dt_t = dtt_ref[q]
    li = lax.broadcasted_iota(jnp.int32, (CHUNK, CHUNK), 0)
    si = lax.broadcasted_iota(jnp.int32, (CHUNK, CHUNK), 1)
    causal = li >= si
    first_half = si < HEAD_DIM
    first_half_n = lax.broadcasted_iota(jnp.int32, (D_STATE, LANES), 1) < HEAD_DIM
    first_half_row = lax.broadcasted_iota(jnp.int32, (1, LANES), 1) < HEAD_DIM
    zeros_n = jnp.zeros((D_STATE, LANES), BF16)
    updates = []

    for g in range(N_GROUPS):
        gs = slice(g * D_STATE, (g + 1) * D_STATE)
        c_g = cm[:, gs]
        c2 = jnp.concatenate([c_g, c_g], axis=1)
        bt_g = bm_t[gs, :]
        cb = jnp.dot(c_g.astype(BF16), bt_g.astype(BF16), preferred_element_type=F32)
        y_pairs = []
        for k in range(HEADS_PER_GROUP // 2):
            pair = (g * HEADS_PER_GROUP) // 2 + k
            x_pair = xs[:, pair * LANES:(pair + 1) * LANES]
            s_pair = state_ref[q, pair]
            s_b = s_pair.astype(BF16)
            heads = (2 * pair, 2 * pair + 1)
            a_l = [jnp.broadcast_to(acum[:, h:h + 1], (CHUNK, CHUNK)) for h in heads]
            cdec = (c2 * jnp.exp2(jnp.where(first_half, a_l[0], a_l[1]))).astype(BF16)
            rhs = [jnp.concatenate([x_pair, s_b, zeros_n], axis=0),
                   jnp.concatenate([x_pair, zeros_n, s_b], axis=0)]
            res, lhs_s, cds = [], [], []
            for i, h in enumerate(heads):
                a_s, dt_s = as_t[h:h + 1, :], dt_t[h:h + 1, :]
                mix = cb * jnp.exp2(jnp.where(causal, a_l[i] - a_s, -jnp.inf)) * dt_s
                lhs = jnp.concatenate([mix.astype(BF16), cdec], axis=1)
                res.append(jnp.dot(lhs, rhs[i], preferred_element_type=F32))
                a_last = acum[CHUNK - 1:CHUNK, h:h + 1]
                lhs_s.append(bt_g * (jnp.exp2(a_last - a_s) * dt_s))
                cds.append(jnp.exp2(a_last))
            y_pairs.append(jnp.where(first_half, res[0], res[1]))
            updates.append((pair, jnp.concatenate(lhs_s, axis=0).astype(BF16), x_pair,
                            jnp.where(first_half_row, cds[0], cds[1]) * s_pair))

        ns = slice(g * NORM_GROUP, (g + 1) * NORM_GROUP)
        y_g = jnp.concatenate(y_pairs, axis=1)
        y_g = (y_g + dskip[:, ns] * xs[:, ns].astype(F32)) * gate_ref[q, :, ns]
        y_ref[q, :, ns] = _rms(y_g, nw[:, ns]).astype(BF16)

    for pair, lhs_s, x_pair, decayed in updates:
        upd = jnp.dot(lhs_s, x_pair, preferred_element_type=F32)
        state_ref[q, pair] = decayed + jnp.where(first_half_n, upd[:D_STATE], upd[D_STATE:])


def _ssd_prompt_kernel(xs_ref, bc_ref, bmt_ref, acum_ref, ast_ref, dtt_ref, gate_ref, dskip_ref, nw_ref,
                       y_ref, hfin_ref, state_ref):
    c = pl.program_id(1)

    @pl.when(c == 0)
    def _():
        state_ref[...] = jnp.zeros(state_ref.shape, F32)

    for q in range(SSD_SEQS):
        _ssd_chunk(q, xs_ref, bc_ref, bmt_ref, acum_ref, ast_ref, dtt_ref, gate_ref, dskip_ref[...], nw_ref[...],
                   y_ref, state_ref)

    @pl.when(c == pl.num_programs(1) - 1)
    def _():
        for q in range(SSD_SEQS):
            for pair in range(HEAD_PAIRS):
                t = jnp.concatenate([state_ref[q, pair], jnp.zeros((LANES - D_STATE, LANES), F32)], axis=0).T
                hfin_ref[q, pair * LANES:(pair + 1) * LANES, :] = t[:, :D_STATE]


def _ssd_prompt(xs, bc, bmt, acum, ast, dtt, gate, dskip, nw, nseq, seq):
    nc = seq // CHUNK
    blk = lambda width, rows=CHUNK: pl.BlockSpec((SSD_SEQS, rows, width), lambda b, c: (b, c, 0))
    return pl.pallas_call(
        _ssd_prompt_kernel,
        grid=(nseq // SSD_SEQS, nc),
        in_specs=[blk(SSD_WIDTH), blk(BC_WIDTH), blk(LANES, BC_WIDTH), blk(DT_PAD), blk(DT_PAD), blk(DT_PAD),
                  blk(SSD_WIDTH),
                  _const_spec((1, SSD_WIDTH)),
                  _const_spec((1, SSD_WIDTH))],
        out_specs=[blk(SSD_WIDTH),
                   pl.BlockSpec((SSD_SEQS, SSD_WIDTH, D_STATE), lambda b, c: (b, 0, 0))],
        out_shape=[jax.ShapeDtypeStruct((nseq, seq, SSD_WIDTH), BF16),
                   jax.ShapeDtypeStruct((nseq, SSD_WIDTH, D_STATE), F32)],
        scratch_shapes=[pltpu.VMEM((SSD_SEQS, HEAD_PAIRS, D_STATE, LANES), F32)],
        compiler_params=pltpu.CompilerParams(dimension_semantics=("arbitrary", "arbitrary"),
                                             vmem_limit_bytes=VMEM_LIMIT),
        name="ssd_prompt",
    )(xs, bc, bmt, acum, ast, dtt, gate, dskip, nw)


STEP_HEADS = 4


def _step_mix_kernel(xbc_ref, z_ref, dt_ref, v_ref, cs_ref, ps_ref, h_ref,
                     cw_ref, cb_ref, dtb_ref, alog_ref, dskip_ref, nw_ref, pw_ref, psc_ref,
                     hout_ref, y_ref, yp_ref, cso_ref, pso_ref,
                     xs_ref, xdt_ref, bt_ref, ct_ref, dect_ref, yt_ref, *, pos0):
    h = pl.program_id(0)
    nprev = POOL_MAX - 1

    @pl.when(h == 0)
    def _prepare():
        xnew = xbc_ref[...]
        conv = cb_ref[...] + xnew * cw_ref[CONV_W - 1:CONV_W, :]
        for k in range(CONV_W - 1):
            conv = conv + cs_ref[k] * cw_ref[k:k + 1, :]
        xbc = _silu(conv)
        xs = xbc[:, :SSD_WIDTH]
        xs_ref[...] = xs
        dt = _softplus(dt_ref[...] + dtb_ref[...])
        dect_ref[...] = jnp.exp(dt * (-jnp.exp(alog_ref[...]))).T
        dt_t = dt.T
        for hh in range(N_HEADS):
            blk = xs[:, (hh // 2) * LANES:(hh // 2 + 1) * LANES].T
            half = blk[(hh % 2) * HEAD_DIM:(hh % 2 + 1) * HEAD_DIM, :]
            xdt_ref[hh * HEAD_DIM:(hh + 1) * HEAD_DIM, :] = half * dt_t[hh:hh + 1, :]
        for i in range(BC_WIDTH // LANES):
            sl = slice(i * LANES, (i + 1) * LANES)
            bt_ref[sl, :] = xbc[:, SSD_WIDTH + i * LANES:SSD_WIDTH + (i + 1) * LANES].T
            ct_ref[sl, :] = xbc[:, SSD_WIDTH + BC_WIDTH + i * LANES:SSD_WIDTH + BC_WIDTH + (i + 1) * LANES].T
        for k in range(CONV_W - 2):
            cso_ref[k] = cs_ref[k + 1]
        cso_ref[CONV_W - 2] = xnew
        for k in range(nprev - 1):
            pso_ref[k] = ps_ref[k + 1]
        pso_ref[nprev - 1] = v_ref[...]

    for i in range(STEP_HEADS):
        head = h * STEP_HEADS + i
        g = lax.div(head, HEADS_PER_GROUP)
        hrow = pl.multiple_of(head * HEAD_DIM, HEAD_DIM)
        grow = pl.multiple_of(g * D_STATE, D_STATE)
        b_t = bt_ref[pl.ds(grow, D_STATE), :]
        c_t = ct_ref[pl.ds(grow, D_STATE), :]
        dec = dect_ref[pl.ds(head, 1), :]

        def per_p(p, carry, i=i, hrow=hrow, b_t=b_t, c_t=c_t, dec=dec):
            xrow = xdt_ref[pl.ds(hrow + p, 1), :]
            hn = h_ref[i, p] * dec + xrow * b_t
            hout_ref[i, p] = hn
            yt_ref[pl.ds(hrow + p, 1), :] = jnp.sum(hn * c_t, axis=0, keepdims=True)
            return carry

        lax.fori_loop(0, HEAD_DIM, per_p, 0, unroll=4)

    @pl.when(h == pl.num_programs(0) - 1)
    def _finish():
        y0 = jnp.concatenate([yt_ref[i * LANES:(i + 1) * LANES, :].T for i in range(SSD_WIDTH // LANES)], axis=1)
        y = _gate_and_norm(y0, xs_ref[...], _silu(z_ref[...]), dskip_ref[...], nw_ref[...])
        y_ref[...] = y.astype(BF16)
        v = v_ref[...]
        sums, cnts = [], []
        for gi, w in enumerate(POOL_WINDOWS):
            sl = slice(gi * POOL_GROUP_DIM, (gi + 1) * POOL_GROUP_DIM)
            acc = v[:, sl]
            for j in range(1, w):
                acc = acc + ps_ref[nprev - j][:, sl]
            sums.append(acc)
            cnts.append(float(min(pos0 + 1, w)))
        yp_ref[...] = _pool_project(sums, v, cnts, pw_ref, psc_ref[...]).astype(BF16)


def _step_mix(xbc, z, dt, v, cs_t, ps_t, h_t, cw, cb, dtb, alog, dskip, nw, pw, psc, pos0):
    n = xbc.shape[0]
    nprev = POOL_MAX - 1
    full = lambda shape: pl.BlockSpec(shape, lambda i: (0,) * len(shape))
    hblk = pl.BlockSpec((STEP_HEADS, HEAD_DIM, D_STATE, n), lambda i: (i, 0, 0, 0))
    return pl.pallas_call(
        functools.partial(_step_mix_kernel, pos0=pos0),
        grid=(N_HEADS // STEP_HEADS,),
        in_specs=[full((n, CONV_DIM)), full((n, SSD_WIDTH)), full((n, DT_PAD)), full((n, POOL_WIDTH)),
                  _const_spec((CONV_W - 1, n, CONV_DIM)), _const_spec((nprev, n, POOL_WIDTH)), hblk,
                  full((CONV_W, CONV_DIM)), full((1, CONV_DIM)), full((1, DT_PAD)), full((1, DT_PAD)),
                  full((1, SSD_WIDTH)), full((1, SSD_WIDTH)),
                  full((len(POOL_WINDOWS), POOL_GROUP_DIM, POOL_GROUP_DIM)), full((1, POOL_WIDTH))],
        out_specs=[hblk, full((n, SSD_WIDTH)), full((n, POOL_WIDTH)),
                   full((CONV_W - 1, n, CONV_DIM)), full((nprev, n, POOL_WIDTH))],
        out_shape=[jax.ShapeDtypeStruct(h_t.shape, F32),
                   jax.ShapeDtypeStruct((n, SSD_WIDTH), BF16),
                   jax.ShapeDtypeStruct((n, POOL_WIDTH), BF16),
                   jax.ShapeDtypeStruct((CONV_W - 1, n, CONV_DIM), F32),
                   jax.ShapeDtypeStruct((nprev, n, POOL_WIDTH), F32)],
        scratch_shapes=[pltpu.VMEM((n, SSD_WIDTH), F32),
                        pltpu.VMEM((SSD_WIDTH, n), F32),
                        pltpu.VMEM((BC_WIDTH, n), F32),
                        pltpu.VMEM((BC_WIDTH, n), F32),
                        pltpu.VMEM((DT_PAD, n), F32),
                        pltpu.VMEM((SSD_WIDTH, n), F32)],
        compiler_params=pltpu.CompilerParams(dimension_semantics=("arbitrary",),
                                             vmem_limit_bytes=VMEM_LIMIT),
        name="step_mix",
    )(xbc, z, dt, v, cs_t, ps_t, h_t, cw, cb, dtb, alog, dskip, nw, pw, psc)


FFN_CHUNK = 256


def _outffn_kernel(*refs, tm, decode):
    if decode:
        (x_ref, ys_ref, yp_ref, st_ref, wout_ref, n2_ref, wup_ref, fcw_ref, fcb_ref, wdn_ref, fn_ref,
         y_ref, sto_ref) = refs
    else:
        (x_ref, ys_ref, yp_ref, wout_ref, n2_ref, wup_ref, fcw_ref, fcb_ref, wdn_ref, fn_ref,
         y_ref, ust_ref, carry_ref) = refs
        j = pl.program_id(1)

        @pl.when(j == 0)
        def _():
            carry_ref[...] = jnp.zeros(carry_ref.shape, F32)

    ymix = jnp.concatenate([ys_ref[...], yp_ref[...]], axis=1)
    x1 = x_ref[...] + jnp.dot(ymix, wout_ref[...], preferred_element_type=F32)
    hn = _rms(x1, n2_ref[...]).astype(BF16)
    acc = x1
    nchunk = D_FF // FFN_CHUNK

    def col_slices(k):
        return [slice(base + k * FFN_CHUNK, base + (k + 1) * FFN_CHUNK) for base in (0, D_FF)]

    def up_proj(k):
        return [jnp.dot(hn, wup_ref[:, cs], preferred_element_type=F32) for cs in col_slices(k)]

    u_next = up_proj(0)
    for k in range(nchunk):
        u_cur = u_next
        if k + 1 < nchunk:
            u_next = up_proj(k + 1)
        halves = []
        for u, cs in zip(u_cur, col_slices(k)):
            if decode:
                prev2, prev1 = st_ref[:, 0, cs], st_ref[:, 1, cs]
                sto_ref[:, 0, cs] = prev1
                sto_ref[:, 1, cs] = u
            else:
                prev8 = carry_ref[:, cs]
                prev1 = _delay_rows(u, prev8, 1)
                prev2 = _delay_rows(u, prev8, 2)
                carry_ref[:, cs] = u[tm - SUBLANES:tm]
            halves.append(fcb_ref[:, cs] + prev2 * fcw_ref[0:1, cs] + prev1 * fcw_ref[1:2, cs]
                          + u * fcw_ref[2:3, cs])
        act = (_silu_half(halves[0]) * halves[1]).astype(BF16)
        acc = acc + jnp.dot(act, wdn_ref[k * FFN_CHUNK:(k + 1) * FFN_CHUNK, :], preferred_element_type=F32)
    y_ref[...] = _rms(acc, fn_ref[...])
    if not decode:
        @pl.when(j == pl.num_programs(1) - 1)
        def _():
            ust_ref[0] = carry_ref[SUBLANES - (FFN_CONV_W - 1):SUBLANES, :]


def _outffn_prompt(x2d, ys, yp, wout, n2, wup, fcw, fcb, wdn, fn, nseq, seq, tm):
    nj = seq // tm
    row = lambda b, j: (b * nj + j, 0)
    return pl.pallas_call(
        functools.partial(_outffn_kernel, tm=tm, decode=False),
        grid=(nseq, nj),
        in_specs=[pl.BlockSpec((tm, D_MODEL), row),
                  pl.BlockSpec((tm, SSD_WIDTH), row),
                  pl.BlockSpec((tm, POOL_WIDTH), row),
                  _const_spec((D_MIX, D_MODEL)),
                  _const_spec((1, D_MODEL)),
                  _const_spec((D_MODEL, 2 * D_FF)),
                  _const_spec((FFN_CONV_W, 2 * D_FF)),
                  _const_spec((1, 2 * D_FF)),
                  _const_spec((D_FF, D_MODEL)),
                  _const_spec((1, D_MODEL))],
        out_specs=[pl.BlockSpec((tm, D_MODEL), row),
                   pl.BlockSpec((1, FFN_CONV_W - 1, 2 * D_FF), lambda b, j: (b, 0, 0))],
        out_shape=[jax.ShapeDtypeStruct((nseq * seq, D_MODEL), F32),
                   jax.ShapeDtypeStruct((nseq, FFN_CONV_W - 1, 2 * D_FF), F32)],
        scratch_shapes=[pltpu.VMEM((SUBLANES, 2 * D_FF), F32)],
        compiler_params=pltpu.CompilerParams(dimension_semantics=("arbitrary", "arbitrary"),
                                             vmem_limit_bytes=VMEM_LIMIT),
        name="outffn_prompt",
    )(x2d, ys, yp, wout, n2, wup, fcw, fcb, wdn, fn)


def _outffn_decode(x2d, ys, yp, ffn_state, wout, n2, wup, fcw, fcb, wdn, fn):
    n = x2d.shape[0]
    full = lambda shape: pl.BlockSpec(shape, lambda i: (0,) * len(shape))
    return pl.pallas_call(
        functools.partial(_outffn_kernel, tm=n, decode=True),
        grid=(1,),
        in_specs=[full((n, D_MODEL)), full((n, SSD_WIDTH)), full((n, POOL_WIDTH)), full(ffn_state.shape),
                  _const_spec((D_MIX, D_MODEL)),
                  _const_spec((1, D_MODEL)),
                  _const_spec((D_MODEL, 2 * D_FF)),
                  _const_spec((FFN_CONV_W, 2 * D_FF)),
                  _const_spec((1, 2 * D_FF)),
                  _const_spec((D_FF, D_MODEL)),
                  _const_spec((1, D_MODEL))],
        out_specs=[full((n, D_MODEL)), full(ffn_state.shape)],
        out_shape=[jax.ShapeDtypeStruct((n, D_MODEL), F32),
                   jax.ShapeDtypeStruct(ffn_state.shape, F32)],
        compiler_params=pltpu.CompilerParams(dimension_semantics=("arbitrary",),
                                             vmem_limit_bytes=VMEM_LIMIT),
        name="outffn_decode",
    )(x2d, ys, yp, ffn_state, wout, n2, wup, fcw, fcb, wdn, fn)


def kernel(x_prompt, x_sample, state_ssm, state_conv, state_pool, state_ffn_conv, norm1_w, w_in, conv_w, conv_b,
           dt_bias, a_log, d_skip, ssd_norm_w, pool_w, pool_scale, w_out, norm2_w, w_up, ffn_conv_w, ffn_conv_b,
           w_down, final_norm_w):
    assert w_in.shape[0] == 1, "single-layer model"
    nseq, seq, _ = x_prompt.shape
    nsmp, one, _ = x_sample.shape
    assert one == 1 and seq % CHUNK == 0 and nsmp == LANES

    wi = jnp.swapaxes(w_in[0], 0, 1)
    w_main = wi[:_X1].astype(BF16)
    w_tail = wi[_X1:]
    n1 = norm1_w[0][None]
    cw, cb = conv_w[0], conv_b[0][None]
    dtb = jnp.pad(dt_bias[0], (0, DT_PAD - N_HEADS))[None]
    alog = jnp.pad(a_log[0], (0, DT_PAD - N_HEADS))[None]
    dskip = jnp.repeat(d_skip[0], HEAD_DIM)[None]
    nw = ssd_norm_w[0][None]
    pw = pool_w[0].astype(BF16)
    psc = pool_scale[0][None]
    wout = w_out[0].astype(BF16)
    n2 = norm2_w[0][None]
    wup = w_up[0].astype(BF16)
    gate_half = jnp.where(jnp.arange(2 * D_FF) < D_FF, 0.5, 1.0).astype(F32)
    fcw, fcb = ffn_conv_w[0] * gate_half, (ffn_conv_b[0] * gate_half)[None]
    wdn = w_down[0].astype(BF16)
    fn = final_norm_w[None]

    xp = x_prompt.reshape(nseq * seq, D_MODEL)
    xs_p, bc_p, bmt_p, acum_p, ast_p, dtt_p, gate_p, yp_p, ctail, vtail = _inproj_prompt(
        xp, n1, w_main, w_tail, 0.5 * cw, 0.5 * cb, dtb, alog, pw, psc, nseq, seq, tm=1024)
    per_seq = lambda a: a.reshape(nseq, -1, a.shape[-1])
    ys_p, hfin = _ssd_prompt(per_seq(xs_p), per_seq(bc_p), per_seq(bmt_p), per_seq(acum_p), per_seq(ast_p),
                             per_seq(dtt_p), per_seq(gate_p), dskip, nw, nseq, seq)
    ys_p = ys_p.reshape(nseq * seq, SSD_WIDTH)
    y_p, ffn_p = _outffn_prompt(xp, ys_p, yp_p, wout, n2, wup, fcw, fcb, wdn, fn, nseq, seq, tm=256)

    xs_in = x_sample.reshape(nsmp, D_MODEL)
    z_s, xbc_s, v_s, dt_s = _inproj(xs_in, n1, w_main, w_tail, tm=nsmp)
    h_t = jnp.transpose(state_ssm[0], (1, 2, 3, 0))
    cs_t = jnp.transpose(state_conv[0], (1, 0, 2))
    ps_t = jnp.transpose(state_pool[0], (1, 0, 2))
    hnew_t, ys_s, yp_s, cs_new, ps_new = _step_mix(xbc_s, z_s, dt_s, v_s, cs_t, ps_t, h_t, cw, cb, dtb, alog,
                                                   dskip, nw, pw, psc, PAST_LEN)
    y_s, ffn_new = _outffn_decode(xs_in, ys_s, yp_s, state_ffn_conv[0], wout, n2, wup, fcw, fcb, wdn, fn)

    return (y_p.reshape(nseq, seq, D_MODEL),
            y_s.reshape(nsmp, 1, D_MODEL),
            hfin.reshape(nseq, N_HEADS, HEAD_DIM, D_STATE)[None],
            jnp.transpose(hnew_t, (3, 0, 1, 2))[None],
            ctail[:, SUBLANES - (CONV_W - 1):][None],
            jnp.transpose(cs_new, (1, 0, 2))[None],
            vtail[:, 2 * SUBLANES - (POOL_MAX - 1):][None],
            jnp.transpose(ps_new, (1, 0, 2))[None],
            ffn_p[None],
            ffn_new[None])
```

```python
import functools

import jax
import jax.numpy as jnp
from jax import lax
from jax.experimental import pallas as pl
from jax.experimental.pallas import tpu as pltpu

F32 = jnp.float32
BF16 = jnp.bfloat16

D_MODEL = 1024
SSD_WIDTH = 1536
HEAD_DIM = 64
N_HEADS = SSD_WIDTH // HEAD_DIM
N_GROUPS = 4
HEADS_PER_GROUP = N_HEADS // N_GROUPS
D_STATE = 64
BC_WIDTH = N_GROUPS * D_STATE
CONV_DIM = SSD_WIDTH + 2 * BC_WIDTH
CONV_W = 4
CHUNK = 128
POOL_WIDTH = 512
POOL_WINDOWS = (2, 4, 8, 16)
POOL_GROUP_DIM = POOL_WIDTH // len(POOL_WINDOWS)
POOL_MAX = max(POOL_WINDOWS)
D_MIX = SSD_WIDTH + POOL_WIDTH
D_FF = 2816
FFN_CONV_W = 3
NORM_GROUP = SSD_WIDTH // N_GROUPS
EPS = 1e-6
LOG2_E = 1.4426950408889634
PAST_LEN = 16384

LANES = 128
SUBLANES = 8
DT_PAD = LANES
VMEM_LIMIT = 56 * 1024 * 1024
HEAD_PAIRS = N_HEADS // 2

_Z0, _Z1 = 0, SSD_WIDTH
_X0, _X1 = _Z1, _Z1 + CONV_DIM
_V0, _V1 = _X1, _X1 + POOL_WIDTH
_T0, _T1 = _V1, _V1 + DT_PAD


def _rms(x, w):
    ms = jnp.mean(x * x, axis=-1, keepdims=True)
    return x * lax.rsqrt(ms + EPS) * w


def _silu_half(h):
    return h + h * jnp.tanh(h)


def _silu(x):
    return _silu_half(0.5 * x)


def _softplus(x):
    return jnp.maximum(x, 0.0) + jnp.log1p(jnp.exp(-jnp.abs(x)))


def _const_spec(shape):
    nd = len(shape)
    return pl.BlockSpec(shape, lambda *_: (0,) * nd, pipeline_mode=pl.Buffered(1))


def _delay_rows(u, prev8, s):
    if s == SUBLANES:
        return jnp.concatenate([prev8, u[:u.shape[0] - SUBLANES]], axis=0)
    r = pltpu.roll(u, s, axis=0)
    c = pltpu.roll(prev8, s, axis=0)
    row = lax.broadcasted_iota(jnp.int32, prev8.shape, 0)
    top = jnp.where(row < s, c, r[0:SUBLANES])
    return jnp.concatenate([top, r[SUBLANES:]], axis=0)


def _gate_and_norm(y, xs, gate, dskip, nw):
    y = (y + dskip * xs) * gate
    outs = []
    for g in range(N_GROUPS):
        sl = slice(g * NORM_GROUP, (g + 1) * NORM_GROUP)
        outs.append(_rms(y[:, sl], nw[:, sl]))
    return jnp.concatenate(outs, axis=1)


def _pool_project(win_sums, v, cnts, pw_ref, pscale):
    outs = []
    for g in range(len(POOL_WINDOWS)):
        sl = slice(g * POOL_GROUP_DIM, (g + 1) * POOL_GROUP_DIM)
        m = win_sums[g] * (1.0 / cnts[g]) - v[:, sl]
        outs.append(jnp.dot(m.astype(BF16), pw_ref[g], preferred_element_type=F32))
    return jnp.concatenate(outs, axis=1) * pscale


def _cumsum_rows(x):
    n = x.shape[0]
    row = lax.broadcasted_iota(jnp.int32, x.shape, 0)
    k = 1
    while k < n:
        x = x + jnp.where(row >= k, pltpu.roll(x, k, axis=0), 0.0)
        k *= 2
    return x


def _inproj_kernel(*refs, tm, prompt, pos0):
    if prompt:
        (x_ref, nw_ref, w_ref, wtail_ref, cw_ref, cb_ref, dtb_ref, alog_ref, pw_ref, psc_ref,
         xs_ref, bc_ref, bmt_ref, acum_ref, ast_ref, dtt_ref, gate_ref, yp_ref, ctail_ref, vtail_ref,
         cc_ref, vc_ref, s2c_ref, s4c_ref, s8c_ref, oc_ref, wt_ref) = refs
    else:
        x_ref, nw_ref, w_ref, wtail_ref, z_ref, xbc_ref, v_ref, dt_ref = refs
    hn = _rms(x_ref[...], nw_ref[...]).astype(BF16)

    def tail_rows():
        w_v = wtail_ref[N_HEADS:, :].astype(BF16)
        dt_pad = jnp.zeros((DT_PAD - N_HEADS, D_MODEL), F32)
        return w_v, jnp.concatenate([wtail_ref[:N_HEADS, :], dt_pad], axis=0).astype(BF16)

    if not prompt:
        def nt(w):
            return lax.dot_general(hn, w, (((1,), (1,)), ((), ())), preferred_element_type=F32)

        w_v, w_dt = tail_rows()
        z_ref[...] = nt(w_ref[_Z0:_Z1, :])
        xbc_ref[...] = nt(w_ref[_X0:_X1, :])
        v_ref[...] = nt(w_v)
        dt_ref[...] = nt(w_dt)
        return

    @pl.when((pl.program_id(0) == 0) & (pl.program_id(1) == 0))
    def _():
        step = 2 * LANES
        for lo in range(0, _X1, step):
            wt_ref[:, lo:lo + step] = w_ref[lo:lo + step, :].T
        w_v, w_dt = tail_rows()
        for lo in range(0, POOL_WIDTH, step):
            wt_ref[:, _V0 + lo:_V0 + lo + step] = w_v[lo:lo + step, :].T
        wt_ref[:, _T0:_T1] = w_dt.T

    def mm(lo, hi):
        return jnp.dot(hn, wt_ref[:, lo:hi], preferred_element_type=F32)

    j = pl.program_id(1)
    carries = (cc_ref, vc_ref, s2c_ref, s4c_ref, s8c_ref)

    @pl.when(j == 0)
    def _():
        for r in carries + (oc_ref,):
            r[...] = jnp.zeros(r.shape, F32)

    last = j == pl.num_programs(1) - 1

    def conv_stage(lo, hi):
        def fn(xbc):
            assert CONV_W == 4
            w0, w1, w2, w3 = (cw_ref[k:k + 1, lo:hi] for k in range(CONV_W))
            x2 = _delay_rows(xbc, cc_ref[:, lo:hi], 2)
            odd = w2 * xbc + w0 * x2
            conv = cb_ref[:, lo:hi] + (w3 * xbc + w1 * x2) + _delay_rows(odd, oc_ref[:, lo:hi], 1)
            cc_ref[:, lo:hi] = xbc[tm - SUBLANES:tm]
            oc_ref[:, lo:hi] = odd[tm - SUBLANES:tm]
            act = _silu_half(conv)
            if hi <= SSD_WIDTH:
                xs_ref[:, lo:hi] = act.astype(BF16)
            else:
                bc_ref[...] = act[:, BC_WIDTH:]
                for c in range(tm // CHUNK):
                    for i in range(BC_WIDTH // LANES):
                        blk = act[c * CHUNK:(c + 1) * CHUNK, i * LANES:(i + 1) * LANES]
                        bmt_ref[c * BC_WIDTH + i * LANES:c * BC_WIDTH + (i + 1) * LANES, :] = blk.T
        return fn

    def pool_stage(v):
        s2 = v + _delay_rows(v, vc_ref[...], 1)
        s4 = s2 + _delay_rows(s2, s2c_ref[...], 2)
        s8 = s4 + _delay_rows(s4, s4c_ref[...], 4)
        s16 = s8 + _delay_rows(s8, s8c_ref[...], 8)
        for r, val in zip(carries[1:], (v, s2, s4, s8)):
            r[...] = val[tm - SUBLANES:tm]
        pos = pos0 + j * tm + lax.broadcasted_iota(jnp.int32, (tm, 1), 0)
        sums, cnts = [], []
        for g, (w, s) in enumerate(zip(POOL_WINDOWS, (s2, s4, s8, s16))):
            sums.append(s[:, g * POOL_GROUP_DIM:(g + 1) * POOL_GROUP_DIM])
            cnts.append(jnp.minimum(pos + 1, w).astype(F32))
        yp_ref[...] = _pool_project(sums, v, cnts, pw_ref, psc_ref[...]).astype(BF16)

        @pl.when(last)
        def _():
            vtail_ref[0] = v[tm - 2 * SUBLANES:tm]

    def gate_stage(lo, hi):
        def fn(z):
            gate_ref[:, lo:hi] = _silu(z)
        return fn

    def dt_stage(dt_raw):
        dt = _softplus(dt_raw + dtb_ref[...])
        neg_a = -jnp.exp(alog_ref[...])
        for c in range(tm // CHUNK):
            rows = slice(c * CHUNK, (c + 1) * CHUNK)
            acum = _cumsum_rows(dt[rows] * neg_a) * LOG2_E
            acum_ref[rows, :] = acum
            ast_ref[rows, :] = acum.T
            dtt_ref[rows, :] = dt[rows].T

    stages = [((_Z0, _Z1), gate_stage(0, SSD_WIDTH)),
              ((_T0, _T1), dt_stage),
              ((_X0, _X0 + SSD_WIDTH), conv_stage(0, SSD_WIDTH)),
              ((_X0 + SSD_WIDTH, _X1), conv_stage(SSD_WIDTH, CONV_DIM)),
              ((_V0, _V1), pool_stage)]
    for cols, fn in stages:
        fn(mm(*cols))

    @pl.when(last)
    def _():
        ctail_ref[0] = cc_ref[...]


def _inproj(x2d, norm_w, w_main, w_tail, tm):
    m = x2d.shape[0]
    row = lambda i: (i, 0)
    return pl.pallas_call(
        functools.partial(_inproj_kernel, tm=tm, prompt=False, pos0=0),
        grid=(m // tm,),
        in_specs=[pl.BlockSpec((tm, D_MODEL), row),
                  _const_spec((1, D_MODEL)),
                  _const_spec((_X1, D_MODEL)),
                  _const_spec((N_HEADS + POOL_WIDTH, D_MODEL))],
        out_specs=[pl.BlockSpec((tm, SSD_WIDTH), row),
                   pl.BlockSpec((tm, CONV_DIM), row),
                   pl.BlockSpec((tm, POOL_WIDTH), row),
                   pl.BlockSpec((tm, DT_PAD), row)],
        out_shape=[jax.ShapeDtypeStruct((m, SSD_WIDTH), F32),
                   jax.ShapeDtypeStruct((m, CONV_DIM), F32),
                   jax.ShapeDtypeStruct((m, POOL_WIDTH), F32),
                   jax.ShapeDtypeStruct((m, DT_PAD), F32)],
        compiler_params=pltpu.CompilerParams(dimension_semantics=("arbitrary",),
                                             vmem_limit_bytes=VMEM_LIMIT),
        name="inproj",
    )(x2d, norm_w, w_main, w_tail)


def _inproj_prompt(x2d, norm_w, w_main, w_tail, cw, cb, dtb, alog, pw, psc, nseq, seq, tm):
    nj = seq // tm
    m = nseq * seq
    row = lambda b, j: (b * nj + j, 0)
    per_seq = lambda b, j: (b, 0, 0)
    return pl.pallas_call(
        functools.partial(_inproj_kernel, tm=tm, prompt=True, pos0=0),
        grid=(nseq, nj),
        in_specs=[pl.BlockSpec((tm, D_MODEL), row),
                  _const_spec((1, D_MODEL)),
                  _const_spec((_X1, D_MODEL)),
                  _const_spec((N_HEADS + POOL_WIDTH, D_MODEL)),
                  _const_spec((CONV_W, CONV_DIM)),
                  _const_spec((1, CONV_DIM)),
                  _const_spec((1, DT_PAD)),
                  _const_spec((1, DT_PAD)),
                  _const_spec((len(POOL_WINDOWS), POOL_GROUP_DIM, POOL_GROUP_DIM)),
                  _const_spec((1, POOL_WIDTH))],
        out_specs=[pl.BlockSpec((tm, SSD_WIDTH), row),
                   pl.BlockSpec((tm, BC_WIDTH), row),
                   pl.BlockSpec((2 * tm, LANES), row),
                   pl.BlockSpec((tm, DT_PAD), row),
                   pl.BlockSpec((tm, DT_PAD), row),
                   pl.BlockSpec((tm, DT_PAD), row),
                   pl.BlockSpec((tm, SSD_WIDTH), row),
                   pl.BlockSpec((tm, POOL_WIDTH), row),
                   pl.BlockSpec((1, SUBLANES, CONV_DIM), per_seq),
                   pl.BlockSpec((1, 2 * SUBLANES, POOL_WIDTH), per_seq)],
        out_shape=[jax.ShapeDtypeStruct((m, SSD_WIDTH), BF16),
                   jax.ShapeDtypeStruct((m, BC_WIDTH), F32),
                   jax.ShapeDtypeStruct((2 * m, LANES), F32),
                   jax.ShapeDtypeStruct((m, DT_PAD), F32),
                   jax.ShapeDtypeStruct((m, DT_PAD), F32),
                   jax.ShapeDtypeStruct((m, DT_PAD), F32),
                   jax.ShapeDtypeStruct((m, SSD_WIDTH), F32),
                   jax.ShapeDtypeStruct((m, POOL_WIDTH), BF16),
                   jax.ShapeDtypeStruct((nseq, SUBLANES, CONV_DIM), F32),
                   jax.ShapeDtypeStruct((nseq, 2 * SUBLANES, POOL_WIDTH), F32)],
        scratch_shapes=[pltpu.VMEM((SUBLANES, CONV_DIM), F32),
                        pltpu.VMEM((SUBLANES, POOL_WIDTH), F32),
                        pltpu.VMEM((SUBLANES, POOL_WIDTH), F32),
                        pltpu.VMEM((SUBLANES, POOL_WIDTH), F32),
                        pltpu.VMEM((SUBLANES, POOL_WIDTH), F32),
                        pltpu.VMEM((SUBLANES, CONV_DIM), F32),
                        pltpu.VMEM((D_MODEL, _T1), BF16)],
        compiler_params=pltpu.CompilerParams(dimension_semantics=("arbitrary", "arbitrary"),
                                             vmem_limit_bytes=VMEM_LIMIT),
        name="inproj_prompt",
    )(x2d, norm_w, w_main, w_tail, cw, cb, dtb, alog, pw, psc)


SSD_SEQS = 8


def _ssd_chunk(q, xs_ref, bc_ref, bmt_ref, acum_ref, ast_ref, dtt_ref, gate_ref, dskip, nw, y_ref, state_ref):
    xs = xs_ref[q]
    cm = bc_ref[q]
    bm_t = bmt_ref[q]
    acum = acum_ref[q]
    as_t = ast_ref[q]
    dt_t = dtt_ref[q]
    li = lax.broadcasted_iota(jnp.int32, (CHUNK, CHUNK), 0)
    si = lax.broadcasted_iota(jnp.int32, (CHUNK, CHUNK), 1)
    causal = li >= si
    first_half = si < HEAD_DIM
    first_half_n = lax.broadcasted_iota(jnp.int32, (D_STATE, LANES), 1) < HEAD_DIM
    first_half_row = lax.broadcasted_iota(jnp.int32, (1, LANES), 1) < HEAD_DIM
    zeros_n = jnp.zeros((D_STATE, LANES), BF16)
    updates = []

    for g in range(N_GROUPS):
        gs = slice(g * D_STATE, (g + 1) * D_STATE)
        c_g = cm[:, gs]
        c2 = jnp.concatenate([c_g, c_g], axis=1)
        bt_g = bm_t[gs, :]
        cb = jnp.dot(c_g.astype(BF16), bt_g.astype(BF16), preferred_element_type=F32)
        y_pairs = []
        for k in range(HEADS_PER_GROUP // 2):
            pair = (g * HEADS_PER_GROUP) // 2 + k
            x_pair = xs[:, pair * LANES:(pair + 1) * LANES]
            s_pair = state_ref[q, pair]
            s_b = s_pair.astype(BF16)
            heads = (2 * pair, 2 * pair + 1)
            a_l = [jnp.broadcast_to(acum[:, h:h + 1], (CHUNK, CHUNK)) for h in heads]
            cdec = (c2 * jnp.exp2(jnp.where(first_half, a_l[0], a_l[1]))).astype(BF16)
            rhs = [jnp.concatenate([x_pair, s_b, zeros_n], axis=0),
                   jnp.concatenate([x_pair, zeros_n, s_b], axis=0)]
            res, lhs_s, cds = [], [], []
            for i, h in enumerate(heads):
                a_s, dt_s = as_t[h:h + 1, :], dt_t[h:h + 1, :]
                mix = cb * jnp.exp2(jnp.where(causal, a_l[i] - a_s, -jnp.inf)) * dt_s
                lhs = jnp.concatenate([mix.astype(BF16), cdec], axis=1)
                res.append(jnp.dot(lhs, rhs[i], preferred_element_type=F32))
                a_last = acum[CHUNK - 1:CHUNK, h:h + 1]
                lhs_s.append(bt_g * (jnp.exp2(a_last - a_s) * dt_s))
                cds.append(jnp.exp2(a_last))
            y_pairs.append(jnp.where(first_half, res[0], res[1]))
            updates.append((pair, jnp.concatenate(lhs_s, axis=0).astype(BF16), x_pair,
                            jnp.where(first_half_row, cds[0], cds[1]) * s_pair))

        ns = slice(g * NORM_GROUP, (g + 1) * NORM_GROUP)
        y_g = jnp.concatenate(y_pairs, axis=1)
        y_g = (y_g + dskip[:, ns] * xs[:, ns].astype(F32)) * gate_ref[q, :, ns]
        y_ref[q, :, ns] = _rms(y_g, nw[:, ns]).astype(BF16)

    for pair, lhs_s, x_pair, decayed in updates:
        upd = jnp.dot(lhs_s, x_pair, preferred_element_type=F32)
        state_ref[q, pair] = decayed + jnp.where(first_half_n, upd[:D_STATE], upd[D_STATE:])


def _ssd_prompt_kernel(xs_ref, bc_ref, bmt_ref, acum_ref, ast_ref, dtt_ref, gate_ref, dskip_ref, nw_ref,
                       y_ref, hfin_ref, state_ref):
    c = pl.program_id(1)

    @pl.when(c == 0)
    def _():
        state_ref[...] = jnp.zeros(state_ref.shape, F32)

    for q in range(SSD_SEQS):
        _ssd_chunk(q, xs_ref, bc_ref, bmt_ref, acum_ref, ast_ref, dtt_ref, gate_ref, dskip_ref[...], nw_ref[...],
                   y_ref, state_ref)

    @pl.when(c == pl.num_programs(1) - 1)
    def _():
        for q in range(SSD_SEQS):
            for pair in range(HEAD_PAIRS):
                t = jnp.concatenate([state_ref[q, pair], jnp.zeros((LANES - D_STATE, LANES), F32)], axis=0).T
                hfin_ref[q, pair * LANES:(pair + 1) * LANES, :] = t[:, :D_STATE]


def _ssd_prompt(xs, bc, bmt, acum, ast, dtt, gate, dskip, nw, nseq, seq):
    nc = seq // CHUNK
    blk = lambda width, rows=CHUNK: pl.BlockSpec((SSD_SEQS, rows, width), lambda b, c: (b, c, 0))
    return pl.pallas_call(
        _ssd_prompt_kernel,
        grid=(nseq // SSD_SEQS, nc),
        in_specs=[blk(SSD_WIDTH), blk(BC_WIDTH), blk(LANES, BC_WIDTH), blk(DT_PAD), blk(DT_PAD), blk(DT_PAD),
                  blk(SSD_WIDTH),
                  _const_spec((1, SSD_WIDTH)),
                  _const_spec((1, SSD_WIDTH))],
        out_specs=[blk(SSD_WIDTH),
                   pl.BlockSpec((SSD_SEQS, SSD_WIDTH, D_STATE), lambda b, c: (b, 0, 0))],
        out_shape=[jax.ShapeDtypeStruct((nseq, seq, SSD_WIDTH), BF16),
                   jax.ShapeDtypeStruct((nseq, SSD_WIDTH, D_STATE), F32)],
        scratch_shapes=[pltpu.VMEM((SSD_SEQS, HEAD_PAIRS, D_STATE, LANES), F32)],
        compiler_params=pltpu.CompilerParams(dimension_semantics=("arbitrary", "arbitrary"),
                                             vmem_limit_bytes=VMEM_LIMIT),
        name="ssd_prompt",
    )(xs, bc, bmt, acum, ast, dtt, gate, dskip, nw)


STEP_HEADS = 4


def _step_mix_kernel(xbc_ref, z_ref, dt_ref, v_ref, cs_ref, ps_ref, h_ref,
                     cw_ref, cb_ref, dtb_ref, alog_ref, dskip_ref, nw_ref, pw_ref, psc_ref,
                     hout_ref, y_ref, yp_ref, cso_ref, pso_ref,
                     xs_ref, xdt_ref, bt_ref, ct_ref, dect_ref, yt_ref, *, pos0):
    h = pl.program_id(0)
    nprev = POOL_MAX - 1

    @pl.when(h == 0)
    def _prepare():
        xnew = xbc_ref[...]
        conv = cb_ref[...] + xnew * cw_ref[CONV_W - 1:CONV_W, :]
        for k in range(CONV_W - 1):
            conv = conv + cs_ref[k] * cw_ref[k:k + 1, :]
        xbc = _silu(conv)
        xs = xbc[:, :SSD_WIDTH]
        xs_ref[...] = xs
        dt = _softplus(dt_ref[...] + dtb_ref[...])
        dect_ref[...] = jnp.exp(dt * (-jnp.exp(alog_ref[...]))).T
        dt_t = dt.T
        for hh in range(N_HEADS):
            blk = xs[:, (hh // 2) * LANES:(hh // 2 + 1) * LANES].T
            half = blk[(hh % 2) * HEAD_DIM:(hh % 2 + 1) * HEAD_DIM, :]
            xdt_ref[hh * HEAD_DIM:(hh + 1) * HEAD_DIM, :] = half * dt_t[hh:hh + 1, :]
        for i in range(BC_WIDTH // LANES):
            sl = slice(i * LANES, (i + 1) * LANES)
            bt_ref[sl, :] = xbc[:, SSD_WIDTH + i * LANES:SSD_WIDTH + (i + 1) * LANES].T
            ct_ref[sl, :] = xbc[:, SSD_WIDTH + BC_WIDTH + i * LANES:SSD_WIDTH + BC_WIDTH + (i + 1) * LANES].T
        for k in range(CONV_W - 2):
            cso_ref[k] = cs_ref[k + 1]
        cso_ref[CONV_W - 2] = xnew
        for k in range(nprev - 1):
            pso_ref[k] = ps_ref[k + 1]
        pso_ref[nprev - 1] = v_ref[...]

    for i in range(STEP_HEADS):
        head = h * STEP_HEADS + i
        g = lax.div(head, HEADS_PER_GROUP)
        hrow = pl.multiple_of(head * HEAD_DIM, HEAD_DIM)
        grow = pl.multiple_of(g * D_STATE, D_STATE)
        b_t = bt_ref[pl.ds(grow, D_STATE), :]
        c_t = ct_ref[pl.ds(grow, D_STATE), :]
        dec = dect_ref[pl.ds(head, 1), :]

        def per_p(p, carry, i=i, hrow=hrow, b_t=b_t, c_t=c_t, dec=dec):
            xrow = xdt_ref[pl.ds(hrow + p, 1), :]
            hn = h_ref[i, p] * dec + xrow * b_t
            hout_ref[i, p] = hn
            yt_ref[pl.ds(hrow + p, 1), :] = jnp.sum(hn * c_t, axis=0, keepdims=True)
            return carry

        lax.fori_loop(0, HEAD_DIM, per_p, 0, unroll=4)

    @pl.when(h == pl.num_programs(0) - 1)
    def _finish():
        y0 = jnp.concatenate([yt_ref[i * LANES:(i + 1) * LANES, :].T for i in range(SSD_WIDTH // LANES)], axis=1)
        y = _gate_and_norm(y0, xs_ref[...], _silu(z_ref[...]), dskip_ref[...], nw_ref[...])
        y_ref[...] = y.astype(BF16)
        v = v_ref[...]
        sums, cnts = [], []
        for gi, w in enumerate(POOL_WINDOWS):
            sl = slice(gi * POOL_GROUP_DIM, (gi + 1) * POOL_GROUP_DIM)
            acc = v[:, sl]
            for j in range(1, w):
                acc = acc + ps_ref[nprev - j][:, sl]
            sums.append(acc)
            cnts.append(float(min(pos0 + 1, w)))
        yp_ref[...] = _pool_project(sums, v, cnts, pw_ref, psc_ref[...]).astype(BF16)


def _step_mix(xbc, z, dt, v, cs_t, ps_t, h_t, cw, cb, dtb, alog, dskip, nw, pw, psc, pos0):
    n = xbc.shape[0]
    nprev = POOL_MAX - 1
    full = lambda shape: pl.BlockSpec(shape, lambda i: (0,) * len(shape))
    hblk = pl.BlockSpec((STEP_HEADS, HEAD_DIM, D_STATE, n), lambda i: (i, 0, 0, 0))
    return pl.pallas_call(
        functools.partial(_step_mix_kernel, pos0=pos0),
        grid=(N_HEADS // STEP_HEADS,),
        in_specs=[full((n, CONV_DIM)), full((n, SSD_WIDTH)), full((n, DT_PAD)), full((n, POOL_WIDTH)),
                  _const_spec((CONV_W - 1, n, CONV_DIM)), _const_spec((nprev, n, POOL_WIDTH)), hblk,
                  full((CONV_W, CONV_DIM)), full((1, CONV_DIM)), full((1, DT_PAD)), full((1, DT_PAD)),
                  full((1, SSD_WIDTH)), full((1, SSD_WIDTH)),
                  full((len(POOL_WINDOWS), POOL_GROUP_DIM, POOL_GROUP_DIM)), full((1, POOL_WIDTH))],
        out_specs=[hblk, full((n, SSD_WIDTH)), full((n, POOL_WIDTH)),
                   full((CONV_W - 1, n, CONV_DIM)), full((nprev, n, POOL_WIDTH))],
        out_shape=[jax.ShapeDtypeStruct(h_t.shape, F32),
                   jax.ShapeDtypeStruct((n, SSD_WIDTH), BF16),
                   jax.ShapeDtypeStruct((n, POOL_WIDTH), BF16),
                   jax.ShapeDtypeStruct((CONV_W - 1, n, CONV_DIM), F32),
                   jax.ShapeDtypeStruct((nprev, n, POOL_WIDTH), F32)],
        scratch_shapes=[pltpu.VMEM((n, SSD_WIDTH), F32),
                        pltpu.VMEM((SSD_WIDTH, n), F32),
                        pltpu.VMEM((BC_WIDTH, n), F32),
                        pltpu.VMEM((BC_WIDTH, n), F32),
                        pltpu.VMEM((DT_PAD, n), F32),
                        pltpu.VMEM((SSD_WIDTH, n), F32)],
        compiler_params=pltpu.CompilerParams(dimension_semantics=("arbitrary",),
                                             vmem_limit_bytes=VMEM_LIMIT),
        name="step_mix",
    )(xbc, z, dt, v, cs_t, ps_t, h_t, cw, cb, dtb, alog, dskip, nw, pw, psc)


FFN_CHUNK = 256


def _outffn_kernel(*refs, tm, decode):
    if decode:
        (x_ref, ys_ref, yp_ref, st_ref, wout_ref, n2_ref, wup_ref, fcw_ref, fcb_ref, wdn_ref, fn_ref,
         y_ref, sto_ref) = refs
    else:
        (x_ref, ys_ref, yp_ref, wout_ref, n2_ref, wup_ref, fcw_ref, fcb_ref, wdn_ref, fn_ref,
         y_ref, ust_ref, carry_ref) = refs
        j = pl.program_id(1)

        @pl.when(j == 0)
        def _():
            carry_ref[...] = jnp.zeros(carry_ref.shape, F32)

    ymix = jnp.concatenate([ys_ref[...], yp_ref[...]], axis=1)
    x1 = x_ref[...] + jnp.dot(ymix, wout_ref[...], preferred_element_type=F32)
    hn = _rms(x1, n2_ref[...]).astype(BF16)
    acc = x1
    nchunk = D_FF // FFN_CHUNK

    def col_slices(k):
        return [slice(base + k * FFN_CHUNK, base + (k + 1) * FFN_CHUNK) for base in (0, D_FF)]

    def up_proj(k):
        return [jnp.dot(hn, wup_ref[:, cs], preferred_element_type=F32) for cs in col_slices(k)]

    u_next = up_proj(0)
    for k in range(nchunk):
        u_cur = u_next
        if k + 1 < nchunk:
            u_next = up_proj(k + 1)
        halves = []
        for u, cs in zip(u_cur, col_slices(k)):
            if decode:
                prev2, prev1 = st_ref[:, 0, cs], st_ref[:, 1, cs]
                sto_ref[:, 0, cs] = prev1
                sto_ref[:, 1, cs] = u
            else:
                prev8 = carry_ref[:, cs]
                prev1 = _delay_rows(u, prev8, 1)
                prev2 = _delay_rows(u, prev8, 2)
                carry_ref[:, cs] = u[tm - SUBLANES:tm]
            halves.append(fcb_ref[:, cs] + prev2 * fcw_ref[0:1, cs] + prev1 * fcw_ref[1:2, cs]
                          + u * fcw_ref[2:3, cs])
        act = (_silu_half(halves[0]) * halves[1]).astype(BF16)
        acc = acc + jnp.dot(act, wdn_ref[k * FFN_CHUNK:(k + 1) * FFN_CHUNK, :], preferred_element_type=F32)
    y_ref[...] = _rms(acc, fn_ref[...])
    if not decode:
        @pl.when(j == pl.num_programs(1) - 1)
        def _():
            ust_ref[0] = carry_ref[SUBLANES - (FFN_CONV_W - 1):SUBLANES, :]


def _outffn_prompt(x2d, ys, yp, wout, n2, wup, fcw, fcb, wdn, fn, nseq, seq, tm):
    nj = seq // tm
    row = lambda b, j: (b * nj + j, 0)
    return pl.pallas_call(
        functools.partial(_outffn_kernel, tm=tm, decode=False),
        grid=(nseq, nj),
        in_specs=[pl.BlockSpec((tm, D_MODEL), row),
                  pl.BlockSpec((tm, SSD_WIDTH), row),
                  pl.BlockSpec((tm, POOL_WIDTH), row),
                  _const_spec((D_MIX, D_MODEL)),
                  _const_spec((1, D_MODEL)),
                  _const_spec((D_MODEL, 2 * D_FF)),
                  _const_spec((FFN_CONV_W, 2 * D_FF)),
                  _const_spec((1, 2 * D_FF)),
                  _const_spec((D_FF, D_MODEL)),
                  _const_spec((1, D_MODEL))],
        out_specs=[pl.BlockSpec((tm, D_MODEL), row),
                   pl.BlockSpec((1, FFN_CONV_W - 1, 2 * D_FF), lambda b, j: (b, 0, 0))],
        out_shape=[jax.ShapeDtypeStruct((nseq * seq, D_MODEL), F32),
                   jax.ShapeDtypeStruct((nseq, FFN_CONV_W - 1, 2 * D_FF), F32)],
        scratch_shapes=[pltpu.VMEM((SUBLANES, 2 * D_FF), F32)],
        compiler_params=pltpu.CompilerParams(dimension_semantics=("arbitrary", "arbitrary"),
                                             vmem_limit_bytes=VMEM_LIMIT),
        name="outffn_prompt",
    )(x2d, ys, yp, wout, n2, wup, fcw, fcb, wdn, fn)


def _outffn_decode(x2d, ys, yp, ffn_state, wout, n2, wup, fcw, fcb, wdn, fn):
    n = x2d.shape[0]
    full = lambda shape: pl.BlockSpec(shape, lambda i: (0,) * len(shape))
    return pl.pallas_call(
        functools.partial(_outffn_kernel, tm=n, decode=True),
        grid=(1,),
        in_specs=[full((n, D_MODEL)), full((n, SSD_WIDTH)), full((n, POOL_WIDTH)), full(ffn_state.shape),
                  _const_spec((D_MIX, D_MODEL)),
                  _const_spec((1, D_MODEL)),
                  _const_spec((D_MODEL, 2 * D_FF)),
                  _const_spec((FFN_CONV_W, 2 * D_FF)),
                  _const_spec((1, 2 * D_FF)),
                  _const_spec((D_FF, D_MODEL)),
                  _const_spec((1, D_MODEL))],
        out_specs=[full((n, D_MODEL)), full(ffn_state.shape)],
        out_shape=[jax.ShapeDtypeStruct((n, D_MODEL), F32),
                   jax.ShapeDtypeStruct(ffn_state.shape, F32)],
        compiler_params=pltpu.CompilerParams(dimension_semantics=("arbitrary",),
                                             vmem_limit_bytes=VMEM_LIMIT),
        name="outffn_decode",
    )(x2d, ys, yp, ffn_state, wout, n2, wup, fcw, fcb, wdn, fn)


def kernel(x_prompt, x_sample, state_ssm, state_conv, state_pool, state_ffn_conv, norm1_w, w_in, conv_w, conv_b,
           dt_bias, a_log, d_skip, ssd_norm_w, pool_w, pool_scale, w_out, norm2_w, w_up, ffn_conv_w, ffn_conv_b,
           w_down, final_norm_w):
    assert w_in.shape[0] == 1, "single-layer model"
    nseq, seq, _ = x_prompt.shape
    nsmp, one, _ = x_sample.shape
    assert one == 1 and seq % CHUNK == 0 and nsmp == LANES

    wi = jnp.swapaxes(w_in[0], 0, 1)
    w_main = wi.astype(BF16)
    w_tail = wi[_X1:]
    n1 = norm1_w[0][None]
    cw, cb = conv_w[0], conv_b[0][None]
    dtb = jnp.pad(dt_bias[0], (0, DT_PAD - N_HEADS))[None]
    alog = jnp.pad(a_log[0], (0, DT_PAD - N_HEADS))[None]
    dskip = jnp.repeat(d_skip[0], HEAD_DIM)[None]
    nw = ssd_norm_w[0][None]
    pw = pool_w[0].astype(BF16)
    psc = pool_scale[0][None]
    wout = w_out[0].astype(BF16)
    n2 = norm2_w[0][None]
    wup = w_up[0].astype(BF16)
    gate_half = jnp.where(jnp.arange(2 * D_FF) < D_FF, 0.5, 1.0).astype(F32)
    fcw, fcb = ffn_conv_w[0] * gate_half, (ffn_conv_b[0] * gate_half)[None]
    wdn = w_down[0].astype(BF16)
    fn = final_norm_w[None]

    xp = x_prompt.reshape(nseq * seq, D_MODEL)
    xs_p, bc_p, bmt_p, acum_p, ast_p, dtt_p, gate_p, yp_p, ctail, vtail = _inproj_prompt(
        xp, n1, w_main, w_tail, 0.5 * cw, 0.5 * cb, dtb, alog, pw, psc, nseq, seq, tm=1024)
    per_seq = lambda a: a.reshape(nseq, -1, a.shape[-1])
    ys_p, hfin = _ssd_prompt(per_seq(xs_p), per_seq(bc_p), per_seq(bmt_p), per_seq(acum_p), per_seq(ast_p),
                             per_seq(dtt_p), per_seq(gate_p), dskip, nw, nseq, seq)
    ys_p = ys_p.reshape(nseq * seq, SSD_WIDTH)
    y_p, ffn_p = _outffn_prompt(xp, ys_p, yp_p, wout, n2, wup, fcw, fcb, wdn, fn, nseq, seq, tm=256)

    xs_in = x_sample.reshape(nsmp, D_MODEL)
    z_s, xbc_s, v_s, dt_s = _inproj(xs_in, n1, w_main, w_tail, tm=nsmp)
    h_t = jnp.transpose(state_ssm[0], (1, 2, 3, 0))
    cs_t = jnp.transpose(state_conv[0], (1, 0, 2))
    ps_t = jnp.transpose(state_pool[0], (1, 0, 2))
    hnew_t, ys_s, yp_s, cs_new, ps_new = _step_mix(xbc_s, z_s, dt_s, v_s, cs_t, ps_t, h_t, cw, cb, dtb, alog,
                                                   dskip, nw, pw, psc, PAST_LEN)
    y_s, ffn_new = _outffn_decode(xs_in, ys_s, yp_s, state_ffn_conv[0], wout, n2, wup, fcw, fcb, wdn, fn)

    return (y_p.reshape(nseq, seq, D_MODEL),
            y_s.reshape(nsmp, 1, D_MODEL),
            hfin.reshape(nseq, N_HEADS, HEAD_DIM, D_STATE)[None],
            jnp.transpose(hnew_t, (3, 0, 1, 2))[None],
            ctail[:, SUBLANES - (CONV_W - 1):][None],
            jnp.transpose(cs_new, (1, 0, 2))[None],
            vtail[:, 2 * SUBLANES - (POOL_MAX - 1):][None],
            jnp.transpose(ps_new, (1, 0, 2))[None],
            ffn_p[None],
            ffn_new[None])
```
